```python
import math
import jax
import jax.numpy as jnp
from jax import lax
import numpy as np

D_MODEL = 2048
BATCH = 1
SEQ = 8192
DEPTH = 2

ROPE_THETA = 500000.0
QUERY_BLOCK = 128
LN_EPS = 1e-5
RMS_EPS = 1e-6
NEG_BIG = -1e30

MLA_HEADS = 8
MLA_Q_LORA = 512
MLA_KV_LORA = 256
MLA_NOPE = 128
MLA_ROPE = 64
MLA_V = 128

SSM_D_INNER = D_MODEL
SSM_HEADDIM = 64
SSM_HEADS = SSM_D_INNER // SSM_HEADDIM
SSM_GROUPS = 4
SSM_STATE = 128
SSM_CONV = 4
SSM_CHUNK = 256
SSM_CONV_DIM = SSM_D_INNER + 2 * SSM_GROUPS * SSM_STATE

MOBA_HEADS = 8
MOBA_HEAD_DIM = 128
MOBA_ROT = MOBA_HEAD_DIM // 4
MOBA_BLOCK = 256
MOBA_TOPK = 3

D_FF = 5632
FFN_CONV = 3

N_BRANCH = 3
DEEPNORM_ALPHA = (2 * DEPTH) ** 0.25
DEEPNORM_BETA = (8 * DEPTH) ** -0.25

IN_SIZES = (MLA_Q_LORA, MLA_KV_LORA, MLA_ROPE,
            SSM_D_INNER, SSM_CONV_DIM, SSM_HEADS,
            MOBA_HEADS * MOBA_HEAD_DIM, MOBA_HEADS * MOBA_HEAD_DIM, MOBA_HEADS * MOBA_HEAD_DIM,
            N_BRANCH * D_MODEL)
D_IN = sum(IN_SIZES)

kernel_name = 'hybrid_mla_ssd_moba_convglu_deepnorm'


def layer_norm(x, g, b):
    xf = x.astype(jnp.float32)
    mu = jnp.mean(xf, axis=-1, keepdims=True)
    var = jnp.mean(jnp.square(xf - mu), axis=-1, keepdims=True)
    return ((xf - mu) * lax.rsqrt(var + LN_EPS) * g + b).astype(x.dtype)


def rms_norm(x, g):
    xf = x.astype(jnp.float32)
    return (xf * lax.rsqrt(jnp.mean(xf * xf, axis=-1, keepdims=True) + RMS_EPS) * g).astype(x.dtype)


def rope(x, pos, rot_dim):
    half = rot_dim // 2
    inv_freq = ROPE_THETA ** (-jnp.arange(half, dtype=jnp.float32) / half)
    ang = pos.astype(jnp.float32)[:, None] * inv_freq[None, :]
    cos = jnp.cos(ang)[:, None, :]
    sin = jnp.sin(ang)[:, None, :]
    x1 = x[..., :half].astype(jnp.float32)
    x2 = x[..., half:rot_dim].astype(jnp.float32)
    rot = jnp.concatenate([x1 * cos - x2 * sin, x2 * cos + x1 * sin], axis=-1).astype(x.dtype)
    return jnp.concatenate([rot, x[..., rot_dim:]], axis=-1)


def causal_dwconv(x, w, b):
    K = w.shape[0]
    y = lax.conv_general_dilated(x, w[:, None, :].astype(x.dtype), window_strides=(1,),
                                 padding=[(K - 1, 0)], dimension_numbers=('NWC', 'WIO', 'NWC'),
                                 feature_group_count=x.shape[-1])
    return y + b


def dense_causal_attention(q, k, v):
    B, S, H, Dk = q.shape
    scale = Dk ** -0.5
    kpos = jnp.arange(S)

    def one_block(i):
        start = i * QUERY_BLOCK
        qb = lax.dynamic_slice_in_dim(q, start, QUERY_BLOCK, axis=1)
        s = jnp.einsum('bqhd,bkhd->bhqk', qb, k).astype(jnp.float32) * scale
        qpos = start + jnp.arange(QUERY_BLOCK)
        s = jnp.where(kpos[None, :] <= qpos[:, None], s, -jnp.inf)
        p = jax.nn.softmax(s, axis=-1).astype(v.dtype)
        return jnp.einsum('bhqk,bkhd->bqhd', p, v)

    out = lax.map(one_block, jnp.arange(S // QUERY_BLOCK))
    return out.transpose(1, 0, 2, 3, 4).reshape(B, S, H, v.shape[-1])


def mla_branch(c_q, c_kv, k_r, q_norm, w_uq, kv_norm, w_ukv, pos):
    B, S, _ = c_q.shape
    q = (rms_norm(c_q, q_norm) @ w_uq).reshape(B, S, MLA_HEADS, MLA_ROPE + MLA_NOPE)
    q = rope(q, pos, MLA_ROPE)
    kv = (rms_norm(c_kv, kv_norm) @ w_ukv).reshape(B, S, MLA_HEADS, MLA_NOPE + MLA_V)
    k_nope, v = kv[..., :MLA_NOPE], kv[..., MLA_NOPE:]
    k_pe = rope(k_r.reshape(B, S, 1, MLA_ROPE), pos, MLA_ROPE)
    k = jnp.concatenate([jnp.broadcast_to(k_pe, (B, S, MLA_HEADS, MLA_ROPE)), k_nope], axis=-1)
    return dense_causal_attention(q, k, v).reshape(B, S, MLA_HEADS * MLA_V)


def ssd_chunked(x, dt, A, Bm, Cm):
    Bsz, S, H, P = x.shape
    G, N = Bm.shape[-2], Bm.shape[-1]
    R = H // G
    L = SSM_CHUNK
    nc = -(-S // L)
    pad = nc * L - S

    def padt(t):
        return jnp.pad(t, [(0, 0), (0, pad)] + [(0, 0)] * (t.ndim - 2))

    xc = (padt(x).astype(jnp.float32) * padt(dt)[..., None]).reshape(Bsz, nc, L, G, R, P)
    a = (padt(dt) * A).reshape(Bsz, nc, L, G, R)
    a_cum = jnp.cumsum(a, axis=2)
    Bc = padt(Bm).astype(jnp.float32).reshape(Bsz, nc, L, G, N)
    Cc = padt(Cm).astype(jnp.float32).reshape(Bsz, nc, L, G, N)
    seg = a_cum[:, :, :, None] - a_cum[:, :, None, :]
    causal = jnp.tril(jnp.ones((L, L), dtype=bool))[:, :, None, None]
    decay = jnp.exp(jnp.where(causal, seg, -jnp.inf))
    cb = jnp.einsum('bclgn,bcsgn->bclsg', Cc, Bc)
    y_diag = jnp.einsum('bclsg,bclsgr,bcsgrp->bclgrp', cb, decay, xc)
    decay_to_end = jnp.exp(a_cum[:, :, -1:] - a_cum)
    states = jnp.einsum('bclgn,bclgr,bclgrp->bcgrpn', Bc, decay_to_end, xc)
    chunk_decay = jnp.exp(a_cum[:, :, -1])

    def step(h, inp):
        dec, st = inp
        return h * dec[..., None, None] + st, h

    h0 = jnp.zeros((Bsz, G, R, P, N), jnp.float32)
    _, prev = lax.scan(step, h0, (chunk_decay.swapaxes(0, 1), states.swapaxes(0, 1)))
    prev = prev.swapaxes(0, 1)
    y_off = jnp.einsum('bclgn,bcgrpn,bclgr->bclgrp', Cc, prev, jnp.exp(a_cum))
    return (y_diag + y_off).reshape(Bsz, nc * L, H, P)[:, :S]


def mamba2_branch(z, xbc, dt_raw, conv_w, conv_b, dt_bias, a_log, d_skip, norm_w):
    B, S, _ = z.shape
    xbc = jax.nn.silu(causal_dwconv(xbc, conv_w, conv_b))
    xs, Bm, Cm = jnp.split(xbc, [SSM_D_INNER, SSM_D_INNER + SSM_GROUPS * SSM_STATE], axis=-1)
    dt = jax.nn.softplus((dt_raw + dt_bias).astype(jnp.float32))
    A = -jnp.exp(a_log.astype(jnp.float32))
    xh = xs.reshape(B, S, SSM_HEADS, SSM_HEADDIM)
    y = ssd_chunked(xh, dt, A, Bm.reshape(B, S, SSM_GROUPS, SSM_STATE),
                    Cm.reshape(B, S, SSM_GROUPS, SSM_STATE))
    y = y + xh.astype(jnp.float32) * d_skip[:, None]
    y = y.reshape(B, S, SSM_D_INNER) * jax.nn.silu(z.astype(jnp.float32))
    yg = y.reshape(B, S, SSM_GROUPS, SSM_D_INNER // SSM_GROUPS)
    yg = yg * lax.rsqrt(jnp.mean(yg * yg, axis=-1, keepdims=True) + RMS_EPS)
    return (yg.reshape(B, S, SSM_D_INNER) * norm_w).astype(z.dtype)


def moba_branch(q, k, v, pos):
    B, S, H, D = q.shape
    BLK = MOBA_BLOCK
    q = rope(q, pos, MOBA_ROT)
    k = rope(k, pos, MOBA_ROT)
    nb = -(-S // BLK)
    padw = [(0, 0), (0, nb * BLK - S), (0, 0), (0, 0)]
    kp = jnp.pad(k, padw)
    vp = jnp.pad(v, padw)
    kb = kp.reshape(B, nb, BLK, H, D).transpose(0, 3, 1, 2, 4)
    vb = vp.reshape(B, nb, BLK, H, D).transpose(0, 3, 1, 2, 4)
    kmean = jnp.mean(kb.astype(jnp.float32), axis=3)
    gate = jnp.einsum('bshd,bhnd->bhsn', q.astype(jnp.float32), kmean)
    qblk = pos // BLK
    past = jnp.arange(nb)[None, :] < qblk[:, None]
    gate = jnp.where(past, gate, NEG_BIG)
    n_sel = min(MOBA_TOPK, nb)
    _, idx = lax.top_k(gate, n_sel)
    valid = idx < qblk[:, None]
    scale = D ** -0.5
    b_ix = jnp.arange(B)[:, None, None, None]
    h_ix = jnp.arange(H)[None, :, None, None]

    def one_block(i):
        start = i * QUERY_BLOCK
        qc = lax.dynamic_slice_in_dim(q, start, QUERY_BLOCK, axis=1)
        ic = lax.dynamic_slice_in_dim(idx, start, QUERY_BLOCK, axis=2)
        mc = lax.dynamic_slice_in_dim(valid, start, QUERY_BLOCK, axis=2)
        ks = kb[b_ix, h_ix, ic]
        vs = vb[b_ix, h_ix, ic]
        s_sel = jnp.einsum('bqhd,bhqkld->bhqkl', qc, ks).astype(jnp.float32) * scale
        s_sel = jnp.where(mc[..., None], s_sel, -jnp.inf).reshape(B, H, QUERY_BLOCK, n_sel * BLK)
        own = start // BLK
        ko = lax.dynamic_slice_in_dim(kp, own * BLK, BLK, axis=1)
        vo = lax.dynamic_slice_in_dim(vp, own * BLK, BLK, axis=1)
        s_own = jnp.einsum('bqhd,blhd->bhql', qc, ko).astype(jnp.float32) * scale
        qpos = start + jnp.arange(QUERY_BLOCK)
        kpos = own * BLK + jnp.arange(BLK)
        s_own = jnp.where(kpos[None, :] <= qpos[:, None], s_own, -jnp.inf)
        p = jax.nn.softmax(jnp.concatenate([s_sel, s_own], axis=-1), axis=-1).astype(v.dtype)
        p_sel = p[..., :n_sel * BLK].reshape(B, H, QUERY_BLOCK, n_sel, BLK)
        p_own = p[..., n_sel * BLK:]
        return (jnp.einsum('bhqkl,bhqkld->bqhd', p_sel, vs)
                + jnp.einsum('bhql,blhd->bqhd', p_own, vo))

    out = lax.map(one_block, jnp.arange(S // QUERY_BLOCK))
    return out.transpose(1, 0, 2, 3, 4).reshape(B, S, H * D)


def hybrid_mixer(x, w_in, mla_q_norm, mla_w_uq, mla_kv_norm, mla_w_ukv,
                 ssm_conv_w, ssm_conv_b, ssm_dt_bias, ssm_a_log, ssm_d, ssm_norm,
                 w_branch_a, w_branch_b, w_branch_c, gate_bias, w_out, pos):
    B, S, _ = x.shape
    h = x @ w_in
    splits = np.cumsum(IN_SIZES)[:-1].tolist()
    c_q, c_kv, k_r, z, xbc, dt_raw, q_c, k_c, v_c, g = jnp.split(h, splits, axis=-1)
    y_a = mla_branch(c_q, c_kv, k_r, mla_q_norm, mla_w_uq, mla_kv_norm, mla_w_ukv, pos) @ w_branch_a
    y_b = mamba2_branch(z, xbc, dt_raw, ssm_conv_w, ssm_conv_b, ssm_dt_bias,
                        ssm_a_log, ssm_d, ssm_norm) @ w_branch_b
    hc = (B, S, MOBA_HEADS, MOBA_HEAD_DIM)
    y_c = moba_branch(q_c.reshape(hc), k_c.reshape(hc), v_c.reshape(hc), pos) @ w_branch_c
    gates = jax.nn.sigmoid((g + gate_bias).astype(jnp.float32)).reshape(B, S, N_BRANCH, D_MODEL)
    merged = gates[:, :, 0] * y_a + gates[:, :, 1] * y_b + gates[:, :, 2] * y_c
    return merged.astype(x.dtype) @ w_out


def conv_glu(x, w_up, conv_w, conv_b, w_down):
    u = causal_dwconv(x @ w_up, conv_w, conv_b)
    gate, up = jnp.split(u, 2, axis=-1)
    return (jax.nn.silu(gate) * up) @ w_down


def setup_inputs(seed: int = 0) -> dict:
    key = jax.random.key(seed)
    ks = iter(jax.random.split(key, 40))
    f32 = jnp.float32
    L = DEPTH
    beta = DEEPNORM_BETA

    def nrm(shape, scale):
        return jax.random.normal(next(ks), shape, f32) * scale

    def gain(shape):
        return 1.0 + nrm(shape, 0.02)

    dt0 = jnp.exp(jax.random.uniform(next(ks), (L, SSM_HEADS), f32, math.log(1e-3), math.log(1e-1)))
    return {
        'x': nrm((BATCH, SEQ, D_MODEL), 1.0),
        'w_in': nrm((L, D_MODEL, D_IN), D_MODEL ** -0.5),
        'mla_q_norm': gain((L, MLA_Q_LORA)),
        'mla_w_uq': nrm((L, MLA_Q_LORA, MLA_HEADS * (MLA_ROPE + MLA_NOPE)), MLA_Q_LORA ** -0.5),
        'mla_kv_norm': gain((L, MLA_KV_LORA)),
        'mla_w_ukv': nrm((L, MLA_KV_LORA, MLA_HEADS * (MLA_NOPE + MLA_V)), MLA_KV_LORA ** -0.5),
        'ssm_conv_w': nrm((L, SSM_CONV, SSM_CONV_DIM), SSM_CONV ** -0.5),
        'ssm_conv_b': nrm((L, SSM_CONV_DIM), 0.02),
        'ssm_dt_bias': dt0 + jnp.log(-jnp.expm1(-dt0)),
        'ssm_a_log': jnp.log(jax.random.uniform(next(ks), (L, SSM_HEADS), f32, 1.0, 16.0)),
        'ssm_d': gain((L, SSM_HEADS)),
        'ssm_norm': gain((L, SSM_D_INNER)),
        'w_branch_a': nrm((L, MLA_HEADS * MLA_V, D_MODEL), beta * (MLA_HEADS * MLA_V) ** -0.5),
        'w_branch_b': nrm((L, SSM_D_INNER, D_MODEL), beta * SSM_D_INNER ** -0.5),
        'w_branch_c': nrm((L, MOBA_HEADS * MOBA_HEAD_DIM, D_MODEL), beta * (MOBA_HEADS * MOBA_HEAD_DIM) ** -0.5),
        'gate_bias': nrm((L, N_BRANCH * D_MODEL), 0.1),
        'w_out': nrm((L, D_MODEL, D_MODEL), beta * D_MODEL ** -0.5),
        'ln1_g': gain((L, D_MODEL)),
        'ln1_b': nrm((L, D_MODEL), 0.02),
        'ffn_w_up': nrm((L, D_MODEL, 2 * D_FF), D_MODEL ** -0.5),
        'ffn_conv_w': nrm((L, FFN_CONV, 2 * D_FF), FFN_CONV ** -0.5),
        'ffn_conv_b': nrm((L, 2 * D_FF), 0.02),
        'ffn_w_down': nrm((L, D_FF, D_MODEL), beta * D_FF ** -0.5),
        'ln2_g': gain((L, D_MODEL)),
        'ln2_b': nrm((L, D_MODEL), 0.02),
    }


def reference(x, w_in, mla_q_norm, mla_w_uq, mla_kv_norm, mla_w_ukv,
              ssm_conv_w, ssm_conv_b, ssm_dt_bias, ssm_a_log, ssm_d, ssm_norm,
              w_branch_a, w_branch_b, w_branch_c, gate_bias, w_out, ln1_g, ln1_b,
              ffn_w_up, ffn_conv_w, ffn_conv_b, ffn_w_down, ln2_g, ln2_b):
    pos = jnp.arange(x.shape[1], dtype=jnp.int32)
    for l in range(DEPTH):
        mix = hybrid_mixer(x, w_in[l], mla_q_norm[l], mla_w_uq[l], mla_kv_norm[l], mla_w_ukv[l],
                           ssm_conv_w[l], ssm_conv_b[l], ssm_dt_bias[l], ssm_a_log[l], ssm_d[l],
                           ssm_norm[l], w_branch_a[l], w_branch_b[l], w_branch_c[l], gate_bias[l],
                           w_out[l], pos)
        x = layer_norm(DEEPNORM_ALPHA * x + mix, ln1_g[l], ln1_b[l])
        ffn = conv_glu(x, ffn_w_up[l], ffn_conv_w[l], ffn_conv_b[l], ffn_w_down[l])
        x = layer_norm(DEEPNORM_ALPHA * x + ffn, ln2_g[l], ln2_b[l])
    return x
```

```python
import functools

import jax
import jax.numpy as jnp
from jax import lax
from jax.experimental import pallas as pl
from jax.experimental.pallas import tpu as pltpu

F32 = jnp.float32
BF16 = jnp.bfloat16

D_MODEL = 2048
DEPTH = 2
ROPE_THETA = 500000.0
LN_EPS = 1e-5
RMS_EPS = 1e-6
NEG_BIG = -1e30

MLA_HEADS = 8
MLA_Q_LORA = 512
MLA_KV_LORA = 256
MLA_NOPE = 128
MLA_ROPE = 64
MLA_V = 128
MLA_DK = MLA_NOPE + MLA_ROPE

SSM_D_INNER = D_MODEL
SSM_HEADDIM = 64
SSM_HEADS = SSM_D_INNER // SSM_HEADDIM
SSM_GROUPS = 4
SSM_STATE = 128
SSM_CONV = 4
SSM_CHUNK = 256
SSM_CONV_DIM = SSM_D_INNER + 2 * SSM_GROUPS * SSM_STATE
SSM_GROUP_W = SSM_D_INNER // SSM_GROUPS
SSM_GROUP_HEADS = SSM_HEADS // SSM_GROUPS

MOBA_HEADS = 8
MOBA_HEAD_DIM = 128
MOBA_ROT = MOBA_HEAD_DIM // 4
MOBA_BLOCK = 256
MOBA_TOPK = 3
MOBA_W = MOBA_HEADS * MOBA_HEAD_DIM

D_FF = 5632
FFN_CONV = 3
N_BRANCH = 3
DEEPNORM_ALPHA = (2 * DEPTH) ** 0.25

LANES = 128
SUBLANES = 8
BF16_ROWS = 16
VMEM_LIMIT = 56 * 1024 * 1024

COL_CQ = 0
COL_CKV = COL_CQ + MLA_Q_LORA
COL_KR = COL_CKV + MLA_KV_LORA
COL_DT = COL_KR + LANES
COL_Z = COL_DT + LANES
COL_XBC = COL_Z + SSM_D_INNER
COL_B = COL_XBC + SSM_D_INNER
COL_C = COL_B + SSM_GROUPS * SSM_STATE
COL_MOBA = COL_XBC + SSM_CONV_DIM
COL_G = COL_MOBA + 3 * MOBA_W
H_COLS = COL_G + N_BRANCH * D_MODEL


def _params(*sem):
    return pltpu.CompilerParams(dimension_semantics=sem, vmem_limit_bytes=VMEM_LIMIT)


def _sigmoid(x):
    return 1.0 / (1.0 + jnp.exp(-x))


def _silu(x):
    return x * _sigmoid(x)


def _split3(a):
    a1 = a.astype(BF16)
    r1 = a - a1.astype(F32)
    a2 = r1.astype(BF16)
    a3 = (r1 - a2.astype(F32)).astype(BF16)
    return a1, a2, a3


def _dot_nn(a, b):
    return jnp.dot(a, b, preferred_element_type=F32)


def _dot_nt(a, b):
    return lax.dot_general(a, b, (((1,), (1,)), ((), ())), preferred_element_type=F32)


def _layer_norm(y, g, b):
    mu = jnp.mean(y, axis=-1, keepdims=True)
    d = y - mu
    var = jnp.mean(d * d, axis=-1, keepdims=True)
    return d * lax.rsqrt(var + LN_EPS) * g + b


def _mm_kernel(x_ref, w_ref, o_ref):
    o_ref[...] = _dot_nn(x_ref[...], w_ref[...]).astype(o_ref.dtype)


def _matmul(x, w, out_dtype, tm, tn, name):
    m, k = x.shape
    n = w.shape[1]
    return pl.pallas_call(
        _mm_kernel,
        grid=(n // tn, m // tm),
        in_specs=[pl.BlockSpec((tm, k), lambda j, i: (i, 0)),
                  pl.BlockSpec((k, tn), lambda j, i: (0, j))],
        out_specs=pl.BlockSpec((tm, tn), lambda j, i: (i, j)),
        out_shape=jax.ShapeDtypeStruct((m, n), out_dtype),
        compiler_params=_params("parallel", "parallel"),
        name=name,
    )(x, w)


def _rope_swap(xs, half):
    lane = lax.broadcasted_iota(jnp.int32, xs.shape, 1)
    first = (lane & (2 * half - 1)) < half
    return jnp.where(first, pltpu.roll(xs, LANES - half, 1), pltpu.roll(xs, half, 1))


def _mla_prep_kernel(h_ref, qn_ref, kvn_ref, wuq_ref, wukv_ref, cos_ref, sin_ref,
                     q_ref, k_ref, v_ref):
    hm = h_ref[...]
    cq = hm[:, COL_CQ:COL_CQ + MLA_Q_LORA]
    ckv = hm[:, COL_CKV:COL_CKV + MLA_KV_LORA]
    kr = hm[:, COL_KR:COL_KR + LANES]
    nq = cq * lax.rsqrt(jnp.mean(cq * cq, axis=-1, keepdims=True) + RMS_EPS) * qn_ref[...]
    nkv = ckv * lax.rsqrt(jnp.mean(ckv * ckv, axis=-1, keepdims=True) + RMS_EPS) * kvn_ref[...]
    qu = _dot_nn(nq.astype(BF16), wuq_ref[...])
    kvu = _dot_nn(nkv.astype(BF16), wukv_ref[...])
    cos = cos_ref[...]
    sin = sin_ref[...]

    def rope(xs):
        return xs * cos + _rope_swap(xs, MLA_ROPE // 2) * sin

    scale = MLA_DK ** -0.5
    kpe = rope(kr)[:, :MLA_ROPE].astype(BF16)
    nope0 = MLA_HEADS * LANES
    for h in range(MLA_HEADS):
        lo, hi = h * LANES, (h + 1) * LANES
        qr = rope(qu[:, lo:hi])[:, :MLA_ROPE]
        q_ref[h, :, 0:MLA_NOPE] = (qu[:, nope0 + lo:nope0 + hi] * scale).astype(BF16)
        q_ref[h, :, MLA_NOPE:MLA_DK] = (qr * scale).astype(BF16)
        k_ref[h, :, 0:MLA_NOPE] = kvu[:, lo:hi].astype(BF16)
        k_ref[h, :, MLA_NOPE:MLA_DK] = kpe
        v_ref[h] = kvu[:, nope0 + lo:nope0 + hi].astype(BF16)


def _mla_prep(h, q_norm, kv_norm, wuq_p, wukv_p, cos_t, sin_t, tm):
    s = h.shape[0]
    hd = MLA_HEADS
    return pl.pallas_call(
        _mla_prep_kernel,
        grid=(s // tm,),
        in_specs=[pl.BlockSpec((tm, COL_Z), lambda i: (i, 0)),
                  pl.BlockSpec((1, MLA_Q_LORA), lambda i: (0, 0)),
                  pl.BlockSpec((1, MLA_KV_LORA), lambda i: (0, 0)),
                  pl.BlockSpec(wuq_p.shape, lambda i: (0, 0)),
                  pl.BlockSpec(wukv_p.shape, lambda i: (0, 0)),
                  pl.BlockSpec((tm, LANES), lambda i: (i, 0)),
                  pl.BlockSpec((tm, LANES), lambda i: (i, 0))],
        out_specs=[pl.BlockSpec((hd, tm, MLA_DK), lambda i: (0, i, 0)),
                   pl.BlockSpec((hd, tm, MLA_DK), lambda i: (0, i, 0)),
                   pl.BlockSpec((hd, tm, MLA_V), lambda i: (0, i, 0))],
        out_shape=[jax.ShapeDtypeStruct((hd, s, MLA_DK), BF16),
                   jax.ShapeDtypeStruct((hd, s, MLA_DK), BF16),
                   jax.ShapeDtypeStruct((hd, s, MLA_V), BF16)],
        compiler_params=_params("parallel"),
        name="mla_prep",
    )(h, q_norm, kv_norm, wuq_p, wukv_p, cos_t, sin_t)


def _softmax_step(s, vs, m, l, acc):
    m_new = jnp.maximum(m, jnp.max(s, axis=-1, keepdims=True))
    alpha = jnp.exp(m - m_new)
    p = jnp.exp(s - m_new)
    l_new = alpha * l + jnp.sum(p, axis=-1, keepdims=True)
    acc_new = alpha * acc + _dot_nn(p.astype(BF16), vs)
    return m_new, l_new, acc_new


def _mla_attn_kernel(q_ref, k_ref, v_ref, o_ref, *, t):
    i = pl.program_id(1)
    q = q_ref[0]

    def past(j, carry):
        off = pl.multiple_of(j * t, t)
        s = _dot_nt(q, k_ref[0, pl.ds(off, t), :])
        return _softmax_step(s, v_ref[0, pl.ds(off, t), :], *carry)

    init = (jnp.full((t, 1), -jnp.inf, F32), jnp.zeros((t, 1), F32), jnp.zeros((t, MLA_V), F32))
    carry = lax.fori_loop(0, i, past, init)
    off = pl.multiple_of(i * t, t)
    s = _dot_nt(q, k_ref[0, pl.ds(off, t), :])
    row = lax.broadcasted_iota(jnp.int32, (t, t), 0)
    col = lax.broadcasted_iota(jnp.int32, (t, t), 1)
    s = jnp.where(col <= row, s, -jnp.inf)
    _, l, acc = _softmax_step(s, v_ref[0, pl.ds(off, t), :], *carry)
    o_ref[...] = (acc / l).astype(o_ref.dtype)


def _mla_attn(q, k, v, t):
    hd, s, _ = q.shape
    return pl.pallas_call(
        functools.partial(_mla_attn_kernel, t=t),
        grid=(hd, s // t),
        in_specs=[pl.BlockSpec((1, t, MLA_DK), lambda h, i: (h, i, 0)),
                  pl.BlockSpec((1, s, MLA_DK), lambda h, i: (h, 0, 0)),
                  pl.BlockSpec((1, s, MLA_V), lambda h, i: (h, 0, 0))],
        out_specs=pl.BlockSpec((t, MLA_V), lambda h, i: (i, h)),
        out_shape=jax.ShapeDtypeStruct((s, hd * MLA_V), BF16),
        compiler_params=_params("parallel", "parallel"),
        name="mla_attn",
    )(q, k, v)


def _moba_prep_kernel(h_ref, cos_ref, sin_ref, q_ref, k_ref, v_ref, km_ref):
    cos = cos_ref[...]
    sin = sin_ref[...]

    def rope(xs):
        return xs * cos + _rope_swap(xs, MOBA_ROT // 2) * sin

    for h in range(MOBA_HEADS):
        lo, hi = h * LANES, (h + 1) * LANES
        q_ref[:, lo:hi] = rope(h_ref[:, lo:hi])
        kr = rope(h_ref[:, MOBA_W + lo:MOBA_W + hi])
        k_ref[:, lo:hi] = kr.astype(BF16)
        km_ref[0, :, lo:hi] = jnp.mean(kr, axis=0, keepdims=True)
    v_ref[...] = h_ref[:, 2 * MOBA_W:3 * MOBA_W].astype(BF16)


def _moba_prep(h, cos_t, sin_t):
    s = h.shape[0]
    t = MOBA_BLOCK
    nb = s // t
    return pl.pallas_call(
        _moba_prep_kernel,
        grid=(nb,),
        in_specs=[pl.BlockSpec((t, 3 * MOBA_W), lambda i: (i, COL_MOBA // (3 * MOBA_W))),
                  pl.BlockSpec((t, LANES), lambda i: (i, 0)),
                  pl.BlockSpec((t, LANES), lambda i: (i, 0))],
        out_specs=[pl.BlockSpec((t, MOBA_W), lambda i: (i, 0)),
                   pl.BlockSpec((t, MOBA_W), lambda i: (i, 0)),
                   pl.BlockSpec((t, MOBA_W), lambda i: (i, 0)),
                   pl.BlockSpec((1, 1, MOBA_W), lambda i: (i, 0, 0))],
        out_shape=[jax.ShapeDtypeStruct((s, MOBA_W), F32),
                   jax.ShapeDtypeStruct((s, MOBA_W), BF16),
                   jax.ShapeDtypeStruct((s, MOBA_W), BF16),
                   jax.ShapeDtypeStruct((nb, 1, MOBA_W), F32)],
        compiler_params=_params("parallel"),
        name="moba_prep",
    )(h, cos_t, sin_t)


def _moba_attn_kernel(q_ref, k_ref, v_ref, km_ref, oh_ref, o_ref, *, tk):
    t = MOBA_BLOCK
    own = pl.program_id(1)
    qf = q_ref[...]
    q1, q2, q3 = _split3(qf)
    k1, k2, k3 = _split3(km_ref[...])
    gate = (_dot_nt(q1, k1) + _dot_nt(q1, k2) + _dot_nt(q2, k1)
            + _dot_nt(q1, k3) + _dot_nt(q2, k2) + _dot_nt(q3, k1))
    lane = lax.broadcasted_iota(jnp.int32, (t, LANES), 1)
    g = jnp.where(lane < own, gate, NEG_BIG)
    picked = lane < 0
    for _ in range(MOBA_TOPK):
        mx = jnp.max(g, axis=-1, keepdims=True)
        idx = jnp.min(jnp.where(g == mx, lane, LANES), axis=-1, keepdims=True)
        hit = lane == idx
        picked = picked | hit
        g = jnp.where(hit, -jnp.inf, g)
    selb = jnp.where(picked & (lane < own), 0.0, NEG_BIG).astype(BF16)
    qb = (qf * (MOBA_HEAD_DIM ** -0.5)).astype(BF16)
    qx = jnp.concatenate([qb, selb], axis=1)

    off = pl.multiple_of(own * t, t)
    s = _dot_nt(qb, k_ref[pl.ds(off, t), :])
    row = lax.broadcasted_iota(jnp.int32, (t, t), 0)
    col = lax.broadcasted_iota(jnp.int32, (t, t), 1)
    s = jnp.where(col <= row, s, -jnp.inf)
    init = (jnp.full((t, 1), -jnp.inf, F32), jnp.zeros((t, 1), F32),
            jnp.zeros((t, MOBA_HEAD_DIM), F32))
    carry = _softmax_step(s, v_ref[pl.ds(off, t), :], *init)

    def past(j, carry):
        off = pl.multiple_of(j * tk, tk)
        kx = jnp.concatenate([k_ref[pl.ds(off, tk), :], oh_ref[pl.ds(off, tk), :]], axis=1)
        s = _dot_nt(qx, kx)
        return _softmax_step(s, v_ref[pl.ds(off, tk), :], *carry)

    n_chunks = (own * t + tk - 1) // tk
    _, l, acc = lax.fori_loop(0, n_chunks, past, carry)
    o_ref[...] = (acc / l).astype(o_ref.dtype)


def _moba_attn(q, k, v, kmean, onehot, tk):
    s = q.shape[0]
    t = MOBA_BLOCK
    return pl.pallas_call(
        functools.partial(_moba_attn_kernel, tk=tk),
        grid=(MOBA_HEADS, s // t),
        in_specs=[pl.BlockSpec((t, LANES), lambda h, i: (i, h)),
                  pl.BlockSpec((s, LANES), lambda h, i: (0, h)),
                  pl.BlockSpec((s, LANES), lambda h, i: (0, h)),
                  pl.BlockSpec((LANES, LANES), lambda h, i: (0, h)),
                  pl.BlockSpec((s, LANES), lambda h, i: (0, 0))],
        out_specs=pl.BlockSpec((t, LANES), lambda h, i: (i, h)),
        out_shape=jax.ShapeDtypeStruct((s, MOBA_W), BF16),
        compiler_params=_params("parallel", "parallel"),
        name="moba_attn",
    )(q, k, v, kmean, onehot)


def _softplus(x):
    return jnp.maximum(x, 0.0) + jnp.log1p(jnp.exp(-jnp.abs(x)))


def _ssd_kernel(xp_ref, x_ref, bp_ref, b_ref, cp_ref, c_ref, dt_ref, z_ref,
                cwx_ref, cbx_ref, cwb_ref, cbb_ref, cwc_ref, cbc_ref,
                dtb_ref, alog_ref, dexp_ref, nw_ref, o_ref,
                xs_scr, bs_scr, cs_scr, st_scr):
    c = pl.program_id(0)
    g = pl.program_id(1)
    L = SSM_CHUNK
    P = SSM_HEADDIM
    halo = SUBLANES

    def conv_silu(prev_ref, cur_ref, scr, w_ref, bias_ref):
        scr[0:halo, :] = jnp.where(c > 0, prev_ref[...], 0.0)
        scr[halo:halo + L, :] = cur_ref[...]
        acc = bias_ref[...]
        for k in range(SSM_CONV):
            o = halo - (SSM_CONV - 1) + k
            acc = acc + w_ref[k:k + 1, :] * scr[o:o + L, :]
        return _silu(acc)

    xs = conv_silu(xp_ref, x_ref, xs_scr, cwx_ref, cbx_ref)
    bm = conv_silu(bp_ref, b_ref, bs_scr, cwb_ref, cbb_ref)
    cm = conv_silu(cp_ref, c_ref, cs_scr, cwc_ref, cbc_ref)

    dtv = _softplus(dt_ref[...] + dtb_ref[...])
    av = dtv * (-jnp.exp(alog_ref[...]))
    row = lax.broadcasted_iota(jnp.int32, (L, L), 0)
    col = lax.broadcasted_iota(jnp.int32, (L, L), 1)
    tril = col <= row
    ones_tril = jnp.where(tril, 1.0, 0.0).astype(BF16)
    a1, a2, a3 = _split3(av)
    acum = _dot_nn(ones_tril, a1) + _dot_nn(ones_tril, a2) + _dot_nn(ones_tril, a3)

    eh = lax.broadcasted_iota(jnp.int32, (LANES, SSM_GROUP_W), 0)
    ej = lax.broadcasted_iota(jnp.int32, (LANES, SSM_GROUP_W), 1)
    expand = jnp.where(eh == g * SSM_GROUP_HEADS + jnp.right_shift(ej, P.bit_length() - 1),
                       1.0, 0.0).astype(BF16)

    def expand3(v):
        v1, v2, v3 = _split3(v)
        return _dot_nn(v1, expand) + _dot_nn(v2, expand) + _dot_nn(v3, expand)

    dt_e = expand3(dtv)
    ac_e = expand3(acum)
    sr = lax.broadcasted_iota(jnp.int32, (BF16_ROWS, LANES), 0)
    sl = lax.broadcasted_iota(jnp.int32, (BF16_ROWS, LANES), 1)
    pick = jnp.where(sl == g * SSM_GROUP_HEADS + sr, 1.0, 0.0).astype(BF16)
    c1, c2, c3 = _split3(acum)
    ac_t = _dot_nt(pick, c1) + _dot_nt(pick, c2) + _dot_nt(pick, c3)

    bb = bm.astype(BF16)
    cb = cm.astype(BF16)
    gmat = _dot_nt(cb, bb)
    xdt = xs * dt_e
    xdt_b = xdt.astype(BF16)
    lane = lax.broadcasted_iota(jnp.int32, (L, LANES), 1)
    lo_half = lane < P
    parts = []
    for pr in range(SSM_GROUP_HEADS // 2):
        xpair = xdt_b[:, pr * LANES:(pr + 1) * LANES]
        acc = None
        for half in range(2):
            r = 2 * pr + half
            seg = ac_e[:, r * P:r * P + 1] - ac_t[r:r + 1, :]
            mh = (gmat * jnp.exp(jnp.where(tril, seg, -jnp.inf))).astype(BF16)
            xh = jnp.where(lo_half if half == 0 else jnp.logical_not(lo_half), xpair,
                           jnp.zeros_like(xpair))
            term = _dot_nn(mh, xh)
            acc = term if acc is None else acc + term
        parts.append(acc)
    y = jnp.concatenate(parts, axis=1)

    last = ac_e[L - 1:L, :]
    xw = (xdt * jnp.exp(last - ac_e)).astype(BF16)
    s_new = _dot_nn(bm.T.astype(BF16), xw)

    @pl.when(c == 0)
    def _():
        st_scr[g] = jnp.zeros((SSM_STATE, SSM_GROUP_W), F32)

    prev = st_scr[g]
    y = y + _dot_nn(cb, prev.astype(BF16)) * jnp.exp(ac_e)
    st_scr[g] = prev * jnp.exp(last) + s_new

    y = y + xs * dexp_ref[...]
    y = y * _silu(z_ref[...])
    y = y * lax.rsqrt(jnp.mean(y * y, axis=-1, keepdims=True) + RMS_EPS) * nw_ref[...]
    o_ref[...] = y.astype(o_ref.dtype)


def _ssd(h, conv_w, conv_b, dt_bias_p, a_log_p, d_exp, norm_w):
    s = h.shape[0]
    L = SSM_CHUNK
    nc = s // L
    gw = SSM_GROUP_W
    n = SSM_STATE
    rb = L // SUBLANES

    def prev_map(cblk):
        return lambda c, g: (jnp.maximum(c * rb - 1, 0), cblk + g)

    def cur_map(cblk):
        return lambda c, g: (c, cblk + g)

    xblk, bblk, cblk = COL_XBC // gw, COL_B // n, COL_C // n
    wb, wc = SSM_D_INNER // n, (SSM_D_INNER + SSM_GROUPS * n) // n
    in_specs = [
        pl.BlockSpec((SUBLANES, gw), prev_map(xblk)), pl.BlockSpec((L, gw), cur_map(xblk)),
        pl.BlockSpec((SUBLANES, n), prev_map(bblk)), pl.BlockSpec((L, n), cur_map(bblk)),
        pl.BlockSpec((SUBLANES, n), prev_map(cblk)), pl.BlockSpec((L, n), cur_map(cblk)),
        pl.BlockSpec((L, LANES), lambda c, g: (c, COL_DT // LANES)),
        pl.BlockSpec((L, gw), cur_map(COL_Z // gw)),
        pl.BlockSpec((SSM_CONV, gw), lambda c, g: (0, g)), pl.BlockSpec((1, gw), lambda c, g: (0, g)),
        pl.BlockSpec((SSM_CONV, n), lambda c, g: (0, wb + g)), pl.BlockSpec((1, n), lambda c, g: (0, wb + g)),
        pl.BlockSpec((SSM_CONV, n), lambda c, g: (0, wc + g)), pl.BlockSpec((1, n), lambda c, g: (0, wc + g)),
        pl.BlockSpec((1, LANES), lambda c, g: (0, 0)),
        pl.BlockSpec((1, LANES), lambda c, g: (0, 0)),
        pl.BlockSpec((1, gw), lambda c, g: (0, g)),
        pl.BlockSpec((1, gw), lambda c, g: (0, g)),
    ]
    return pl.pallas_call(
        _ssd_kernel,
        grid=(nc, SSM_GROUPS),
        in_specs=in_specs,
        out_specs=pl.BlockSpec((L, gw), lambda c, g: (c, g)),
        out_shape=jax.ShapeDtypeStruct((s, SSM_D_INNER), BF16),
        scratch_shapes=[pltpu.VMEM((L + SUBLANES, gw), F32),
                        pltpu.VMEM((L + SUBLANES, n), F32),
                        pltpu.VMEM((L + SUBLANES, n), F32),
                        pltpu.VMEM((SSM_GROUPS, n, gw), F32)],
        compiler_params=_params("arbitrary", "arbitrary"),
        name="ssd",
    )(h, h, h, h, h, h, h, h, conv_w, conv_b, conv_w, conv_b, conv_w, conv_b,
      dt_bias_p, a_log_p, d_exp, norm_w)


def _merge_kernel(ya_ref, yb_ref, yc_ref, wa_ref, wb_ref, wc_ref,
                  g0_ref, g1_ref, g2_ref, b0_ref, b1_ref, b2_ref, o_ref):
    def branch(y_ref, w_ref, g_ref, b_ref):
        return _sigmoid(g_ref[...] + b_ref[...]) * _dot_nn(y_ref[...], w_ref[...])

    out = (branch(ya_ref, wa_ref, g0_ref, b0_ref) + branch(yb_ref, wb_ref, g1_ref, b1_ref)
           + branch(yc_ref, wc_ref, g2_ref, b2_ref))
    o_ref[...] = out.astype(o_ref.dtype)


def _merge(ya, yb, yc, wa, wb, wc, h, gate_bias, tm, tn):
    s = ya.shape[0]
    nt = D_MODEL // tn
    gblk = COL_G // tn

    def gmap(b):
        return lambda i, j: (i, gblk + b * nt + j)

    def bmap(b):
        return lambda i, j: (0, b * nt + j)

    return pl.pallas_call(
        _merge_kernel,
        grid=(s // tm, nt),
        in_specs=[pl.BlockSpec((tm, ya.shape[1]), lambda i, j: (i, 0)),
                  pl.BlockSpec((tm, yb.shape[1]), lambda i, j: (i, 0)),
                  pl.BlockSpec((tm, yc.shape[1]), lambda i, j: (i, 0)),
                  pl.BlockSpec((wa.shape[0], tn), lambda i, j: (0, j)),
                  pl.BlockSpec((wb.shape[0], tn), lambda i, j: (0, j)),
                  pl.BlockSpec((wc.shape[0], tn), lambda i, j: (0, j)),
                  pl.BlockSpec((tm, tn), gmap(0)), pl.BlockSpec((tm, tn), gmap(1)),
                  pl.BlockSpec((tm, tn), gmap(2)),
                  pl.BlockSpec((1, tn), bmap(0)), pl.BlockSpec((1, tn), bmap(1)),
                  pl.BlockSpec((1, tn), bmap(2))],
        out_specs=pl.BlockSpec((tm, tn), lambda i, j: (i, j)),
        out_shape=jax.ShapeDtypeStruct((s, D_MODEL), BF16),
        compiler_params=_params("parallel", "parallel"),
        name="branch_merge",
    )(ya, yb, yc, wa, wb, wc, h, h, h, gate_bias, gate_bias, gate_bias)


def _proj_ln_kernel(m_ref, w_ref, x_ref, g_ref, b_ref, of_ref, ob_ref):
    y = DEEPNORM_ALPHA * x_ref[...] + _dot_nn(m_ref[...], w_ref[...])
    out = _layer_norm(y, g_ref[...], b_ref[...])
    of_ref[...] = out
    ob_ref[...] = out.astype(BF16)


def _proj_ln(m, w, x, g, b, tm):
    s, d = x.shape
    return pl.pallas_call(
        _proj_ln_kernel,
        grid=(s // tm,),
        in_specs=[pl.BlockSpec((tm, m.shape[1]), lambda i: (i, 0)),
                  pl.BlockSpec(w.shape, lambda i: (0, 0)),
                  pl.BlockSpec((tm, d), lambda i: (i, 0)),
                  pl.BlockSpec((1, d), lambda i: (0, 0)),
                  pl.BlockSpec((1, d), lambda i: (0, 0))],
        out_specs=[pl.BlockSpec((tm, d), lambda i: (i, 0)),
                   pl.BlockSpec((tm, d), lambda i: (i, 0))],
        out_shape=[jax.ShapeDtypeStruct((s, d), F32), jax.ShapeDtypeStruct((s, d), BF16)],
        compiler_params=_params("parallel"),
        name="out_proj_ln",
    )(m, w, x, g, b)


def _ffn_up_kernel(xp_ref, x_ref, wg_ref, wu_ref, cwg_ref, cbg_ref, cwu_ref, cbu_ref, o_ref,
                   xx_scr, ug_scr, uu_scr, *, tm):
    i = pl.program_id(0)
    j = pl.program_id(1)
    halo = BF16_ROWS

    @pl.when(j == 0)
    def _():
        xx_scr[0:halo, :] = jnp.where(i > 0, xp_ref[...], jnp.zeros_like(xp_ref[...]))
        xx_scr[halo:halo + tm, :] = x_ref[...]

    xx = xx_scr[...]
    ug_scr[...] = _dot_nn(xx, wg_ref[...])
    uu_scr[...] = _dot_nn(xx, wu_ref[...])

    def conv(scr, w_ref, bias_ref):
        acc = bias_ref[...]
        for k in range(FFN_CONV):
            o = halo - (FFN_CONV - 1) + k
            acc = acc + w_ref[k:k + 1, :] * scr[o:o + tm, :]
        return acc

    gate = conv(ug_scr, cwg_ref, cbg_ref)
    up = conv(uu_scr, cwu_ref, cbu_ref)
    o_ref[...] = (_silu(gate) * up).astype(o_ref.dtype)


def _ffn_up(xb, w_up, conv_w, conv_b, tm, tj):
    s, d = xb.shape
    nj = D_FF // tj
    rb = tm // BF16_ROWS
    return pl.pallas_call(
        functools.partial(_ffn_up_kernel, tm=tm),
        grid=(s // tm, nj),
        in_specs=[pl.BlockSpec((BF16_ROWS, d), lambda i, j: (jnp.maximum(i * rb - 1, 0), 0)),
                  pl.BlockSpec((tm, d), lambda i, j: (i, 0)),
                  pl.BlockSpec((d, tj), lambda i, j: (0, j)),
                  pl.BlockSpec((d, tj), lambda i, j: (0, nj + j)),
                  pl.BlockSpec((FFN_CONV, tj), lambda i, j: (0, j)),
                  pl.BlockSpec((1, tj), lambda i, j: (0, j)),
                  pl.BlockSpec((FFN_CONV, tj), lambda i, j: (0, nj + j)),
                  pl.BlockSpec((1, tj), lambda i, j: (0, nj + j))],
        out_specs=pl.BlockSpec((tm, tj), lambda i, j: (i, j)),
        out_shape=jax.ShapeDtypeStruct((s, D_FF), BF16),
        scratch_shapes=[pltpu.VMEM((tm + BF16_ROWS, d), BF16),
                        pltpu.VMEM((tm + BF16_ROWS, tj), F32),
                        pltpu.VMEM((tm + BF16_ROWS, tj), F32)],
        compiler_params=_params("parallel", "arbitrary"),
        name="ffn_up_glu",
    )(xb, xb, w_up, w_up, conv_w, conv_b, conv_w, conv_b)


def _ffn_down_kernel(a_ref, w_ref, x_ref, g_ref, b_ref, of_ref, ob_ref, acc_ref):
    k = pl.program_id(1)

    @pl.when(k == 0)
    def _():
        acc_ref[...] = DEEPNORM_ALPHA * x_ref[...]

    acc_ref[...] += _dot_nn(a_ref[...], w_ref[...])

    @pl.when(k == pl.num_programs(1) - 1)
    def _():
        out = _layer_norm(acc_ref[...], g_ref[...], b_ref[...])
        of_ref[...] = out
        ob_ref[...] = out.astype(BF16)


def _ffn_down(a, w, x, g, b, tm, tk):
    s, d = x.shape
    return pl.pallas_call(
        _ffn_down_kernel,
        grid=(s // tm, a.shape[1] // tk),
        in_specs=[pl.BlockSpec((tm, tk), lambda i, k: (i, k)),
                  pl.BlockSpec((tk, d), lambda i, k: (k, 0)),
                  pl.BlockSpec((tm, d), lambda i, k: (i, 0)),
                  pl.BlockSpec((1, d), lambda i, k: (0, 0)),
                  pl.BlockSpec((1, d), lambda i, k: (0, 0))],
        out_specs=[pl.BlockSpec((tm, d), lambda i, k: (i, 0)),
                   pl.BlockSpec((tm, d), lambda i, k: (i, 0))],
        out_shape=[jax.ShapeDtypeStruct((s, d), F32), jax.ShapeDtypeStruct((s, d), BF16)],
        scratch_shapes=[pltpu.VMEM((tm, d), F32)],
        compiler_params=_params("parallel", "arbitrary"),
        name="ffn_down_ln",
    )(a, w, x, g, b)


def _pad_cols(w, width):
    return jnp.pad(w, ((0, 0), (0, width - w.shape[1])))


def _pack_w_in(w_in):
    sizes = (MLA_Q_LORA, MLA_KV_LORA, MLA_ROPE, SSM_D_INNER, SSM_CONV_DIM, SSM_HEADS,
             MOBA_W, MOBA_W, MOBA_W, N_BRANCH * D_MODEL)
    offs = [0]
    for sz in sizes:
        offs.append(offs[-1] + sz)
    p = [w_in[:, offs[i]:offs[i + 1]] for i in range(len(sizes))]
    cq, ckv, kr, z, xbc, dt, qc, kc, vc, g = p
    out = jnp.concatenate([cq, ckv, _pad_cols(kr, LANES), _pad_cols(dt, LANES), z, xbc, qc, kc, vc, g],
                          axis=1)
    assert out.shape[1] == H_COLS
    return out.astype(BF16)


def _pack_mla_weights(w_uq, w_ukv):
    wq = w_uq.reshape(MLA_Q_LORA, MLA_HEADS, MLA_DK)
    rope_part = jnp.pad(wq[:, :, :MLA_ROPE], ((0, 0), (0, 0), (0, LANES - MLA_ROPE)))
    nope_part = wq[:, :, MLA_ROPE:]
    wq_p = jnp.concatenate([rope_part.reshape(MLA_Q_LORA, -1), nope_part.reshape(MLA_Q_LORA, -1)], axis=1)
    wkv = w_ukv.reshape(MLA_KV_LORA, MLA_HEADS, MLA_NOPE + MLA_V)
    wkv_p = jnp.concatenate([wkv[:, :, :MLA_NOPE].reshape(MLA_KV_LORA, -1),
                             wkv[:, :, MLA_NOPE:].reshape(MLA_KV_LORA, -1)], axis=1)
    return wq_p.astype(BF16), wkv_p.astype(BF16)


def _rope_tables(s, rot_dim):
    half = rot_dim // 2
    inv_freq = ROPE_THETA ** (-jnp.arange(half, dtype=F32) / half)
    ang = jnp.arange(s, dtype=jnp.int32).astype(F32)[:, None] * inv_freq[None, :]
    cos, sin = jnp.cos(ang), jnp.sin(ang)
    cos_g = jnp.concatenate([cos, cos], axis=1)
    sin_g = jnp.concatenate([-sin, sin], axis=1)
    if rot_dim == MLA_ROPE:
        reps = LANES // rot_dim
        return jnp.tile(cos_g, (1, reps)), jnp.tile(sin_g, (1, reps))
    rest = LANES - rot_dim
    return (jnp.concatenate([cos_g, jnp.ones((s, rest), F32)], axis=1),
            jnp.concatenate([sin_g, jnp.zeros((s, rest), F32)], axis=1))


def _layer(x, xb, p, tabs):
    s = x.shape[0]
    mla_cos, mla_sin, moba_cos, moba_sin, onehot = tabs
    h = _matmul(xb, p["w_in"], F32, min(s, 1024), 1024, "in_proj")

    q, k, v = _mla_prep(h, p["mla_q_norm"], p["mla_kv_norm"], p["w_uq"], p["w_ukv"],
                        mla_cos, mla_sin, 256)
    ya = _mla_attn(q, k, v, 512)

    yb = _ssd(h, p["ssm_conv_w"], p["ssm_conv_b"], p["ssm_dt_bias"], p["ssm_a_log"],
              p["ssm_d"], p["ssm_norm"])

    mq, mk, mv, km = _moba_prep(h, moba_cos, moba_sin)
    nb = s // MOBA_BLOCK
    km = jnp.pad(km.reshape(nb, MOBA_W), ((0, LANES - nb), (0, 0)))
    yc = _moba_attn(mq, mk, mv, km, onehot, min(s, 1024))

    merged = _merge(ya, yb, yc, p["w_branch_a"], p["w_branch_b"], p["w_branch_c"], h,
                    p["gate_bias"], min(s, 512), 512)
    x1, x1b = _proj_ln(merged, p["w_out"], x, p["ln1_g"], p["ln1_b"], 256)
    a = _ffn_up(x1b, p["ffn_w_up"], p["ffn_conv_w"], p["ffn_conv_b"], min(s, 1024), 512)
    return _ffn_down(a, p["ffn_w_down"], x1, p["ln2_g"], p["ln2_b"], min(s, 512), 512)


def kernel(x, w_in, mla_q_norm, mla_w_uq, mla_kv_norm, mla_w_ukv, ssm_conv_w, ssm_conv_b, ssm_dt_bias, ssm_a_log, ssm_d, ssm_norm, w_branch_a, w_branch_b, w_branch_c, gate_bias, w_out, ln1_g, ln1_b, ffn_w_up, ffn_conv_w, ffn_conv_b, ffn_w_down, ln2_g, ln2_b):
    b, s, d = x.shape
    assert b == 1 and d == D_MODEL
    assert s % 1024 == 0 and s // MOBA_BLOCK <= LANES
    depth = w_in.shape[0]

    tabs = _rope_tables(s, MLA_ROPE) + _rope_tables(s, MOBA_ROT)
    blk = jnp.arange(s, dtype=jnp.int32)[:, None] // MOBA_BLOCK
    onehot = (blk == jnp.arange(LANES, dtype=jnp.int32)[None, :]).astype(BF16)
    tabs = tabs + (onehot,)

    xf = x.reshape(s, d)
    xb = xf.astype(BF16)
    for l in range(depth):
        wuq, wukv = _pack_mla_weights(mla_w_uq[l], mla_w_ukv[l])
        p = {
            "w_in": _pack_w_in(w_in[l]),
            "mla_q_norm": mla_q_norm[l].reshape(1, -1),
            "mla_kv_norm": mla_kv_norm[l].reshape(1, -1),
            "w_uq": wuq, "w_ukv": wukv,
            "ssm_conv_w": ssm_conv_w[l],
            "ssm_conv_b": ssm_conv_b[l].reshape(1, -1),
            "ssm_dt_bias": _pad_cols(ssm_dt_bias[l].reshape(1, -1), LANES),
            "ssm_a_log": _pad_cols(ssm_a_log[l].reshape(1, -1), LANES),
            "ssm_d": jnp.repeat(ssm_d[l], SSM_HEADDIM).reshape(1, -1),
            "ssm_norm": ssm_norm[l].reshape(1, -1),
            "w_branch_a": w_branch_a[l].astype(BF16),
            "w_branch_b": w_branch_b[l].astype(BF16),
            "w_branch_c": w_branch_c[l].astype(BF16),
            "gate_bias": gate_bias[l].reshape(1, -1),
            "w_out": w_out[l].astype(BF16),
            "ln1_g": ln1_g[l].reshape(1, -1), "ln1_b": ln1_b[l].reshape(1, -1),
            "ffn_w_up": ffn_w_up[l].astype(BF16),
            "ffn_conv_w": ffn_conv_w[l],
            "ffn_conv_b": ffn_conv_b[l].reshape(1, -1),
            "ffn_w_down": ffn_w_down[l].astype(BF16),
            "ln2_g": ln2_g[l].reshape(1, -1), "ln2_b": ln2_b[l].reshape(1, -1),
        }
        xf, xb = _layer(xf, xb, p, tabs)
    return xf.reshape(b, s, d)
```

```python
import functools

import jax
import jax.numpy as jnp
from jax import lax
from jax.experimental import pallas as pl
from jax.experimental.pallas import tpu as pltpu

F32 = jnp.float32
BF16 = jnp.bfloat16

D_MODEL = 2048
DEPTH = 2
ROPE_THETA = 500000.0
LN_EPS = 1e-5
RMS_EPS = 1e-6
NEG_BIG = -1e30
LOG2E = 1.4426950408889634

MLA_HEADS = 8
MLA_Q_LORA = 512
MLA_KV_LORA = 256
MLA_NOPE = 128
MLA_ROPE = 64
MLA_V = 128
MLA_DK = MLA_NOPE + MLA_ROPE

SSM_D_INNER = D_MODEL
SSM_HEADDIM = 64
SSM_HEADS = SSM_D_INNER // SSM_HEADDIM
SSM_GROUPS = 4
SSM_STATE = 128
SSM_CONV = 4
SSM_CHUNK = 256
SSM_CONV_DIM = SSM_D_INNER + 2 * SSM_GROUPS * SSM_STATE
SSM_GROUP_W = SSM_D_INNER // SSM_GROUPS
SSM_GROUP_HEADS = SSM_HEADS // SSM_GROUPS

MOBA_HEADS = 8
MOBA_HEAD_DIM = 128
MOBA_ROT = MOBA_HEAD_DIM // 4
MOBA_BLOCK = 256
MOBA_TOPK = 3
MOBA_W = MOBA_HEADS * MOBA_HEAD_DIM

D_FF = 5632
FFN_CONV = 3
N_BRANCH = 3
DEEPNORM_ALPHA = (2 * DEPTH) ** 0.25

LANES = 128
SUBLANES = 8
BF16_ROWS = 16
VMEM_LIMIT = 56 * 1024 * 1024

COL_CQ = 0
COL_CKV = COL_CQ + MLA_Q_LORA
COL_KR = COL_CKV + MLA_KV_LORA
COL_DT = COL_KR + LANES
COL_Z = COL_DT + LANES
COL_XBC = COL_Z + SSM_D_INNER
COL_B = COL_XBC + SSM_D_INNER
COL_C = COL_B + SSM_GROUPS * SSM_STATE
COL_MOBA = COL_XBC + SSM_CONV_DIM
COL_G = COL_MOBA + 3 * MOBA_W
H_COLS = COL_G + N_BRANCH * D_MODEL


def _params(*sem):
    return pltpu.CompilerParams(dimension_semantics=sem, vmem_limit_bytes=VMEM_LIMIT)


def _sigmoid(x):
    return 1.0 / (1.0 + jnp.exp(-x))


def _silu(x):
    return x * _sigmoid(x)


def _split3(a):
    a1 = a.astype(BF16)
    r1 = a - a1.astype(F32)
    a2 = r1.astype(BF16)
    a3 = (r1 - a2.astype(F32)).astype(BF16)
    return a1, a2, a3


def _dot_nn(a, b):
    return jnp.dot(a, b, preferred_element_type=F32)


def _dot_nt(a, b):
    return lax.dot_general(a, b, (((1,), (1,)), ((), ())), preferred_element_type=F32)


def _layer_norm(y, g, b):
    mu = jnp.mean(y, axis=-1, keepdims=True)
    d = y - mu
    var = jnp.mean(d * d, axis=-1, keepdims=True)
    return d * lax.rsqrt(var + LN_EPS) * g + b


def _mm_kernel(x_ref, w_ref, o_ref):
    o_ref[...] = _dot_nn(x_ref[...], w_ref[...]).astype(o_ref.dtype)


def _matmul(x, w, out_dtype, tm, tn, name):
    m, k = x.shape
    n = w.shape[1]
    return pl.pallas_call(
        _mm_kernel,
        grid=(n // tn, m // tm),
        in_specs=[pl.BlockSpec((tm, k), lambda j, i: (i, 0)),
                  pl.BlockSpec((k, tn), lambda j, i: (0, j))],
        out_specs=pl.BlockSpec((tm, tn), lambda j, i: (i, j)),
        out_shape=jax.ShapeDtypeStruct((m, n), out_dtype),
        compiler_params=_params("parallel", "parallel"),
        name=name,
    )(x, w)


def _rope_swap(xs, half):
    lane = lax.broadcasted_iota(jnp.int32, xs.shape, 1)
    first = (lane & (2 * half - 1)) < half
    return jnp.where(first, pltpu.roll(xs, LANES - half, 1), pltpu.roll(xs, half, 1))


def _mla_prep_kernel(h_ref, qn_ref, kvn_ref, wuq_ref, wukv_ref, cos_ref, sin_ref,
                     qt_ref, k_ref, vt_ref):
    hm = h_ref[...]
    cq = hm[:, COL_CQ:COL_CQ + MLA_Q_LORA]
    ckv = hm[:, COL_CKV:COL_CKV + MLA_KV_LORA]
    kr = hm[:, COL_KR:COL_KR + LANES]
    nq = cq * lax.rsqrt(jnp.mean(cq * cq, axis=-1, keepdims=True) + RMS_EPS) * qn_ref[...]
    nkv = ckv * lax.rsqrt(jnp.mean(ckv * ckv, axis=-1, keepdims=True) + RMS_EPS) * kvn_ref[...]
    qu = _dot_nn(nq.astype(BF16), wuq_ref[...])
    kvu = _dot_nn(nkv.astype(BF16), wukv_ref[...])
    cos = cos_ref[...]
    sin = sin_ref[...]

    def rope(xs):
        return xs * cos + _rope_swap(xs, MLA_ROPE // 2) * sin

    scale = MLA_DK ** -0.5 * LOG2E
    kpe = rope(kr)[:, :MLA_ROPE].astype(BF16)
    nope0 = MLA_HEADS * LANES
    for h in range(MLA_HEADS):
        lo, hi = h * LANES, (h + 1) * LANES
        qr = rope(qu[:, lo:hi])
        qt_ref[h, 0:MLA_NOPE, :] = (qu[:, nope0 + lo:nope0 + hi] * scale).T.astype(BF16)
        qt_ref[h, MLA_NOPE:MLA_DK, :] = (qr * scale).T[:MLA_ROPE, :].astype(BF16)
        k_ref[h, :, 0:MLA_NOPE] = kvu[:, lo:hi].astype(BF16)
        k_ref[h, :, MLA_NOPE:MLA_DK] = kpe
        vt_ref[h] = kvu[:, nope0 + lo:nope0 + hi].T.astype(BF16)


def _mla_prep(h, q_norm, kv_norm, wuq_p, wukv_p, cos_t, sin_t, tm):
    s = h.shape[0]
    hd = MLA_HEADS
    return pl.pallas_call(
        _mla_prep_kernel,
        grid=(s // tm,),
        in_specs=[pl.BlockSpec((tm, COL_Z), lambda i: (i, 0)),
                  pl.BlockSpec((1, MLA_Q_LORA), lambda i: (0, 0)),
                  pl.BlockSpec((1, MLA_KV_LORA), lambda i: (0, 0)),
                  pl.BlockSpec(wuq_p.shape, lambda i: (0, 0)),
                  pl.BlockSpec(wukv_p.shape, lambda i: (0, 0)),
                  pl.BlockSpec((tm, LANES), lambda i: (i, 0)),
                  pl.BlockSpec((tm, LANES), lambda i: (i, 0))],
        out_specs=[pl.BlockSpec((hd, MLA_DK, tm), lambda i: (0, 0, i)),
                   pl.BlockSpec((hd, tm, MLA_DK), lambda i: (0, i, 0)),
                   pl.BlockSpec((hd, MLA_V, tm), lambda i: (0, 0, i))],
        out_shape=[jax.ShapeDtypeStruct((hd, MLA_DK, s), BF16),
                   jax.ShapeDtypeStruct((hd, s, MLA_DK), BF16),
                   jax.ShapeDtypeStruct((hd, MLA_V, s), BF16)],
        compiler_params=_params("parallel"),
        name="mla_prep",
    )(h, q_norm, kv_norm, wuq_p, wukv_p, cos_t, sin_t)


SOFTMAX_ROWS = 64


def _fold_rows(x, op):
    out = x[0:SUBLANES]
    for r in range(1, x.shape[0] // SUBLANES):
        out = op(out, x[r * SUBLANES:(r + 1) * SUBLANES])
    return out


def _softmax_tile(s_ref, p_ref, m, l, t, causal):
    rows = SOFTMAX_ROWS

    def chunk(c):
        blk = s_ref[c * rows:(c + 1) * rows, :]
        if causal:
            key = c * rows + lax.broadcasted_iota(jnp.int32, (rows, t), 0)
            qry = lax.broadcasted_iota(jnp.int32, (rows, t), 1)
            blk = jnp.where(key <= qry, blk, -jnp.inf)
        return blk

    mx = None
    for c in range(t // rows):
        part = _fold_rows(chunk(c), jnp.maximum)
        mx = part if mx is None else jnp.maximum(mx, part)
    m_new = jnp.maximum(m, jnp.max(mx, axis=0, keepdims=True))
    alpha = jnp.exp2(m - m_new)
    tot = None
    for c in range(t // rows):
        p = jnp.exp2(chunk(c) - m_new)
        part = _fold_rows(p, jnp.add)
        tot = part if tot is None else tot + part
        p_ref[c * rows:(c + 1) * rows, :] = p.astype(BF16)
    l_new = alpha * l + jnp.sum(tot, axis=0, keepdims=True)
    return alpha, m_new, l_new


def _causal_flash(scores, values, write_out, i, t, scratch):
    s_bufs, p_bufs, acc_scr = scratch[0:2], scratch[2:4], scratch[4]
    s_bufs[0][...] = scores(0)
    p_bufs[1][...] = jnp.zeros((t, t), BF16)
    acc_scr[...] = jnp.zeros(acc_scr.shape, F32)

    def accumulate(a_prev, j_prev, p_ref):
        acc_scr[...] = a_prev * acc_scr[...] + _dot_nn(values(jnp.maximum(j_prev, 0)), p_ref[...])

    def step(j, carry, cur):
        a_prev, m, l = carry
        s_bufs[1 - cur][...] = scores(j + 1)
        accumulate(a_prev, j - 1, p_bufs[1 - cur])
        return _softmax_tile(s_bufs[cur], p_bufs[cur], m, l, t, causal=False)

    def pair(jj, carry):
        return step(2 * jj + 1, step(2 * jj, carry, 0), 1)

    def finish(carry, cur):
        a_prev, m, l = carry
        accumulate(a_prev, i - 1, p_bufs[1 - cur])
        alpha, m, l = _softmax_tile(s_bufs[cur], p_bufs[cur], m, l, t, causal=True)
        accumulate(alpha, i, p_bufs[cur])
        write_out(acc_scr[...] / l)

    init = (jnp.ones((1, t), F32), jnp.full((1, t), -jnp.inf, F32), jnp.zeros((1, t), F32))
    carry = lax.fori_loop(0, i // 2, pair, init)

    @pl.when(i % 2 == 0)
    def _():
        finish(carry, 0)

    @pl.when(i % 2 == 1)
    def _():
        finish(step(i - 1, carry, 0), 1)


def _flash_scratch(t, dv):
    return [pltpu.VMEM((t, t), F32), pltpu.VMEM((t, t), F32),
            pltpu.VMEM((t, t), BF16), pltpu.VMEM((t, t), BF16), pltpu.VMEM((dv, t), F32)]


def _mla_attn_kernel(qt_ref, k_ref, vt_ref, o_ref, *scratch, t):
    i = pl.program_id(1)
    qt = qt_ref[0]

    def scores(j):
        return _dot_nn(k_ref[0, pl.ds(pl.multiple_of(j * t, t), t), :], qt)

    def values(j):
        return vt_ref[0, :, pl.ds(pl.multiple_of(j * t, t), t)]

    def write_out(out):
        o_ref[...] = out.T.astype(o_ref.dtype)

    _causal_flash(scores, values, write_out, i, t, scratch)


def _mla_attn(qt, k, vt, t):
    hd, _, s = qt.shape
    return pl.pallas_call(
        functools.partial(_mla_attn_kernel, t=t),
        grid=(hd, s // t),
        in_specs=[pl.BlockSpec((1, MLA_DK, t), lambda h, i: (h, 0, i)),
                  pl.BlockSpec((1, s, MLA_DK), lambda h, i: (h, 0, 0)),
                  pl.BlockSpec((1, MLA_V, s), lambda h, i: (h, 0, 0))],
        out_specs=pl.BlockSpec((t, MLA_V), lambda h, i: (i, h)),
        out_shape=jax.ShapeDtypeStruct((s, hd * MLA_V), BF16),
        scratch_shapes=_flash_scratch(t, MLA_V),
        compiler_params=_params("parallel", "parallel"),
        name="mla_attn",
    )(qt, k, vt)


def _moba_prep_kernel(h_ref, cos_ref, sin_ref, q_ref, k_ref, vt_ref, km_ref):
    cos = cos_ref[...]
    sin = sin_ref[...]

    def rope(xs):
        return xs * cos + _rope_swap(xs, MOBA_ROT // 2) * sin

    for h in range(MOBA_HEADS):
        lo, hi = h * LANES, (h + 1) * LANES
        q_ref[:, lo:hi] = rope(h_ref[:, lo:hi])
        kr = rope(h_ref[:, MOBA_W + lo:MOBA_W + hi])
        k_ref[:, lo:hi] = kr.astype(BF16)
        km_ref[0, :, lo:hi] = jnp.mean(kr, axis=0, keepdims=True)
        vt_ref[lo:hi, :] = h_ref[:, 2 * MOBA_W + lo:2 * MOBA_W + hi].T.astype(BF16)


def _moba_prep(h, cos_t, sin_t):
    s = h.shape[0]
    t = MOBA_BLOCK
    nb = s // t
    return pl.pallas_call(
        _moba_prep_kernel,
        grid=(nb,),
        in_specs=[pl.BlockSpec((t, 3 * MOBA_W), lambda i: (i, COL_MOBA // (3 * MOBA_W))),
                  pl.BlockSpec((t, LANES), lambda i: (i, 0)),
                  pl.BlockSpec((t, LANES), lambda i: (i, 0))],
        out_specs=[pl.BlockSpec((t, MOBA_W), lambda i: (i, 0)),
                   pl.BlockSpec((t, MOBA_W), lambda i: (i, 0)),
                   pl.BlockSpec((MOBA_W, t), lambda i: (0, i)),
                   pl.BlockSpec((1, 1, MOBA_W), lambda i: (i, 0, 0))],
        out_shape=[jax.ShapeDtypeStruct((s, MOBA_W), F32),
                   jax.ShapeDtypeStruct((s, MOBA_W), BF16),
                   jax.ShapeDtypeStruct((MOBA_W, s), BF16),
                   jax.ShapeDtypeStruct((nb, 1, MOBA_W), F32)],
        compiler_params=_params("parallel"),
        name="moba_prep",
    )(h, cos_t, sin_t)


def _moba_attn_kernel(q_ref, k_ref, vt_ref, km_ref, oh_ref, o_ref, *scratch, t):
    i = pl.program_id(1)
    slots = km_ref.shape[0]
    qt = q_ref[...].T
    q1, q2, q3 = _split3(qt)
    k1, k2, k3 = _split3(km_ref[...])
    gate = (_dot_nn(k1, q1) + _dot_nn(k2, q1) + _dot_nn(k1, q2)
            + _dot_nn(k3, q1) + _dot_nn(k2, q2) + _dot_nn(k1, q3))
    slot = lax.broadcasted_iota(jnp.int32, (slots, t), 0)
    slot_f = slot.astype(F32)
    qpos = i * t + lax.broadcasted_iota(jnp.int32, (1, t), 1)
    own = jnp.right_shift(qpos, MOBA_BLOCK.bit_length() - 1)
    g = jnp.where(slot < own, gate, NEG_BIG)
    picked = slot < 0
    for _ in range(MOBA_TOPK):
        mx = jnp.max(g, axis=0, keepdims=True)
        first = jnp.min(jnp.where(g == mx, slot_f, float(slots)), axis=0, keepdims=True)
        hit = slot_f == first
        picked = picked | hit
        g = jnp.where(hit, -jnp.inf, g)
    visible = (picked & (slot < own)) | (slot == own)
    parts = [qt * (MOBA_HEAD_DIM ** -0.5 * LOG2E), jnp.where(visible, 0.0, NEG_BIG)]
    if slots < LANES:
        parts.append(jnp.zeros((LANES - slots, t), F32))
    qxt = jnp.concatenate(parts, axis=0).astype(BF16)

    def scores(j):
        off = pl.multiple_of(j * t, t)
        kx = jnp.concatenate([k_ref[pl.ds(off, t), :], oh_ref[pl.ds(off, t), :]], axis=1)
        return _dot_nn(kx, qxt)

    def values(j):
        return vt_ref[:, pl.ds(pl.multiple_of(j * t, t), t)]

    def write_out(out):
        o_ref[...] = out.T.astype(o_ref.dtype)

    _causal_flash(scores, values, write_out, i, t, scratch)


def _moba_attn(q, k, vt, kmean, onehot, t):
    s = q.shape[0]
    assert t % MOBA_BLOCK == 0
    return pl.pallas_call(
        functools.partial(_moba_attn_kernel, t=t),
        grid=(MOBA_HEADS, s // t),
        in_specs=[pl.BlockSpec((t, LANES), lambda h, i: (i, h)),
                  pl.BlockSpec((s, LANES), lambda h, i: (0, h)),
                  pl.BlockSpec((LANES, s), lambda h, i: (h, 0)),
                  pl.BlockSpec((kmean.shape[0], LANES), lambda h, i: (0, h)),
                  pl.BlockSpec((s, LANES), lambda h, i: (0, 0))],
        out_specs=pl.BlockSpec((t, LANES), lambda h, i: (i, h)),
        out_shape=jax.ShapeDtypeStruct((s, MOBA_W), BF16),
        scratch_shapes=_flash_scratch(t, MOBA_HEAD_DIM),
        compiler_params=_params("parallel", "parallel"),
        name="moba_attn",
    )(q, k, vt, kmean, onehot)


def _softplus(x):
    return jnp.maximum(x, 0.0) + jnp.log1p(jnp.exp(-jnp.abs(x)))


def _ssd_kernel(xp_ref, x_ref, bp_ref, b_ref, cp_ref, c_ref, dt_ref, z_ref,
                cwx_ref, cbx_ref, cwb_ref, cbb_ref, cwc_ref, cbc_ref,
                dtb_ref, alog_ref, dexp_ref, nw_ref, o_ref,
                xs_scr, bs_scr, cs_scr, st_scr):
    c = pl.program_id(0)
    g = pl.program_id(1)
    L = SSM_CHUNK
    P = SSM_HEADDIM
    halo = SUBLANES

    def conv_silu(prev_ref, cur_ref, scr, w_ref, bias_ref):
        scr[0:halo, :] = jnp.where(c > 0, prev_ref[...], 0.0)
        scr[halo:halo + L, :] = cur_ref[...]
        acc = bias_ref[...]
        for k in range(SSM_CONV):
            o = halo - (SSM_CONV - 1) + k
            acc = acc + w_ref[k:k + 1, :] * scr[o:o + L, :]
        return _silu(acc)

    xs = conv_silu(xp_ref, x_ref, xs_scr, cwx_ref, cbx_ref)
    bm = conv_silu(bp_ref, b_ref, bs_scr, cwb_ref, cbb_ref)
    cm = conv_silu(cp_ref, c_ref, cs_scr, cwc_ref, cbc_ref)

    dtv = _softplus(dt_ref[...] + dtb_ref[...])
    av = dtv * (-jnp.exp(alog_ref[...]))
    row = lax.broadcasted_iota(jnp.int32, (L, L), 0)
    col = lax.broadcasted_iota(jnp.int32, (L, L), 1)
    tril = col <= row
    ones_tril = jnp.where(tril, 1.0, 0.0).astype(BF16)
    a1, a2, a3 = _split3(av)
    acum = _dot_nn(ones_tril, a1) + _dot_nn(ones_tril, a2) + _dot_nn(ones_tril, a3)

    eh = lax.broadcasted_iota(jnp.int32, (LANES, SSM_GROUP_W), 0)
    ej = lax.broadcasted_iota(jnp.int32, (LANES, SSM_GROUP_W), 1)
    expand = jnp.where(eh == g * SSM_GROUP_HEADS + jnp.right_shift(ej, P.bit_length() - 1),
                       1.0, 0.0).astype(BF16)

    def expand3(v):
        v1, v2, v3 = _split3(v)
        return _dot_nn(v1, expand) + _dot_nn(v2, expand) + _dot_nn(v3, expand)

    dt_e = expand3(dtv)
    ac_e = expand3(acum)
    sr = lax.broadcasted_iota(jnp.int32, (BF16_ROWS, LANES), 0)
    sl = lax.broadcasted_iota(jnp.int32, (BF16_ROWS, LANES), 1)
    pick = jnp.where(sl == g * SSM_GROUP_HEADS + sr, 1.0, 0.0).astype(BF16)
    c1, c2, c3 = _split3(acum)
    ac_t = _dot_nt(pick, c1) + _dot_nt(pick, c2) + _dot_nt(pick, c3)

    bb = bm.astype(BF16)
    cb = cm.astype(BF16)
    gmat = _dot_nt(cb, bb)
    xdt = xs * dt_e
    xdt_b = xdt.astype(BF16)
    lane = lax.broadcasted_iota(jnp.int32, (L, LANES), 1)
    lo_half = lane < P
    parts = []
    for pr in range(SSM_GROUP_HEADS // 2):
        xpair = xdt_b[:, pr * LANES:(pr + 1) * LANES]
        acc = None
        for half in range(2):
            r = 2 * pr + half
            seg = ac_e[:, r * P:r * P + 1] - ac_t[r:r + 1, :]
            mh = (gmat * jnp.exp(jnp.where(tril, seg, -jnp.inf))).astype(BF16)
            xh = jnp.where(lo_half if half == 0 else jnp.logical_not(lo_half), xpair,
                           jnp.zeros_like(xpair))
            term = _dot_nn(mh, xh)
            acc = term if acc is None else acc + term
        parts.append(acc)
    y = jnp.concatenate(parts, axis=1)

    last = ac_e[L - 1:L, :]
    xw = (xdt * jnp.exp(last - ac_e)).astype(BF16)
    s_new = _dot_nn(bm.T.astype(BF16), xw)

    @pl.when(c == 0)
    def _():
        st_scr[g] = jnp.zeros((SSM_STATE, SSM_GROUP_W), F32)

    prev = st_scr[g]
    y = y + _dot_nn(cb, prev.astype(BF16)) * jnp.exp(ac_e)
    st_scr[g] = prev * jnp.exp(last) + s_new

    y = y + xs * dexp_ref[...]
    y = y * _silu(z_ref[...])
    y = y * lax.rsqrt(jnp.mean(y * y, axis=-1, keepdims=True) + RMS_EPS) * nw_ref[...]
    o_ref[...] = y.astype(o_ref.dtype)


def _ssd(h, conv_w, conv_b, dt_bias_p, a_log_p, d_exp, norm_w):
    s = h.shape[0]
    L = SSM_CHUNK
    nc = s // L
    gw = SSM_GROUP_W
    n = SSM_STATE
    rb = L // SUBLANES

    def prev_map(cblk):
        return lambda c, g: (jnp.maximum(c * rb - 1, 0), cblk + g)

    def cur_map(cblk):
        return lambda c, g: (c, cblk + g)

    xblk, bblk, cblk = COL_XBC // gw, COL_B // n, COL_C // n
    wb, wc = SSM_D_INNER // n, (SSM_D_INNER + SSM_GROUPS * n) // n
    in_specs = [
        pl.BlockSpec((SUBLANES, gw), prev_map(xblk)), pl.BlockSpec((L, gw), cur_map(xblk)),
        pl.BlockSpec((SUBLANES, n), prev_map(bblk)), pl.BlockSpec((L, n), cur_map(bblk)),
        pl.BlockSpec((SUBLANES, n), prev_map(cblk)), pl.BlockSpec((L, n), cur_map(cblk)),
        pl.BlockSpec((L, LANES), lambda c, g: (c, COL_DT // LANES)),
        pl.BlockSpec((L, gw), cur_map(COL_Z // gw)),
        pl.BlockSpec((SSM_CONV, gw), lambda c, g: (0, g)), pl.BlockSpec((1, gw), lambda c, g: (0, g)),
        pl.BlockSpec((SSM_CONV, n), lambda c, g: (0, wb + g)), pl.BlockSpec((1, n), lambda c, g: (0, wb + g)),
        pl.BlockSpec((SSM_CONV, n), lambda c, g: (0, wc + g)), pl.BlockSpec((1, n), lambda c, g: (0, wc + g)),
        pl.BlockSpec((1, LANES), lambda c, g: (0, 0)),
        pl.BlockSpec((1, LANES), lambda c, g: (0, 0)),
        pl.BlockSpec((1, gw), lambda c, g: (0, g)),
        pl.BlockSpec((1, gw), lambda c, g: (0, g)),
    ]
    return pl.pallas_call(
        _ssd_kernel,
        grid=(nc, SSM_GROUPS),
        in_specs=in_specs,
        out_specs=pl.BlockSpec((L, gw), lambda c, g: (c, g)),
        out_shape=jax.ShapeDtypeStruct((s, SSM_D_INNER), BF16),
        scratch_shapes=[pltpu.VMEM((L + SUBLANES, gw), F32),
                        pltpu.VMEM((L + SUBLANES, n), F32),
                        pltpu.VMEM((L + SUBLANES, n), F32),
                        pltpu.VMEM((SSM_GROUPS, n, gw), F32)],
        compiler_params=_params("arbitrary", "arbitrary"),
        name="ssd",
    )(h, h, h, h, h, h, h, h, conv_w, conv_b, conv_w, conv_b, conv_w, conv_b,
      dt_bias_p, a_log_p, d_exp, norm_w)


def _merge_kernel(ya_ref, yb_ref, yc_ref, wa_ref, wb_ref, wc_ref,
                  g0_ref, g1_ref, g2_ref, b0_ref, b1_ref, b2_ref, o_ref):
    def branch(y_ref, w_ref, g_ref, b_ref):
        return _sigmoid(g_ref[...] + b_ref[...]) * _dot_nn(y_ref[...], w_ref[...])

    out = (branch(ya_ref, wa_ref, g0_ref, b0_ref) + branch(yb_ref, wb_ref, g1_ref, b1_ref)
           + branch(yc_ref, wc_ref, g2_ref, b2_ref))
    o_ref[...] = out.astype(o_ref.dtype)


def _merge(ya, yb, yc, wa, wb, wc, h, gate_bias, tm, tn):
    s = ya.shape[0]
    nt = D_MODEL // tn
    gblk = COL_G // tn

    def gmap(b):
        return lambda i, j: (i, gblk + b * nt + j)

    def bmap(b):
        return lambda i, j: (0, b * nt + j)

    return pl.pallas_call(
        _merge_kernel,
        grid=(s // tm, nt),
        in_specs=[pl.BlockSpec((tm, ya.shape[1]), lambda i, j: (i, 0)),
                  pl.BlockSpec((tm, yb.shape[1]), lambda i, j: (i, 0)),
                  pl.BlockSpec((tm, yc.shape[1]), lambda i, j: (i, 0)),
                  pl.BlockSpec((wa.shape[0], tn), lambda i, j: (0, j)),
                  pl.BlockSpec((wb.shape[0], tn), lambda i, j: (0, j)),
                  pl.BlockSpec((wc.shape[0], tn), lambda i, j: (0, j)),
                  pl.BlockSpec((tm, tn), gmap(0)), pl.BlockSpec((tm, tn), gmap(1)),
                  pl.BlockSpec((tm, tn), gmap(2)),
                  pl.BlockSpec((1, tn), bmap(0)), pl.BlockSpec((1, tn), bmap(1)),
                  pl.BlockSpec((1, tn), bmap(2))],
        out_specs=pl.BlockSpec((tm, tn), lambda i, j: (i, j)),
        out_shape=jax.ShapeDtypeStruct((s, D_MODEL), BF16),
        compiler_params=_params("parallel", "parallel"),
        name="branch_merge",
    )(ya, yb, yc, wa, wb, wc, h, h, h, gate_bias, gate_bias, gate_bias)


def _proj_ln_kernel(m_ref, w_ref, x_ref, g_ref, b_ref, of_ref, ob_ref):
    y = DEEPNORM_ALPHA * x_ref[...] + _dot_nn(m_ref[...], w_ref[...])
    out = _layer_norm(y, g_ref[...], b_ref[...])
    of_ref[...] = out
    ob_ref[...] = out.astype(BF16)


def _proj_ln(m, w, x, g, b, tm):
    s, d = x.shape
    return pl.pallas_call(
        _proj_ln_kernel,
        grid=(s // tm,),
        in_specs=[pl.BlockSpec((tm, m.shape[1]), lambda i: (i, 0)),
                  pl.BlockSpec(w.shape, lambda i: (0, 0)),
                  pl.BlockSpec((tm, d), lambda i: (i, 0)),
                  pl.BlockSpec((1, d), lambda i: (0, 0)),
                  pl.BlockSpec((1, d), lambda i: (0, 0))],
        out_specs=[pl.BlockSpec((tm, d), lambda i: (i, 0)),
                   pl.BlockSpec((tm, d), lambda i: (i, 0))],
        out_shape=[jax.ShapeDtypeStruct((s, d), F32), jax.ShapeDtypeStruct((s, d), BF16)],
        compiler_params=_params("parallel"),
        name="out_proj_ln",
    )(m, w, x, g, b)


def _ffn_up_kernel(xp_ref, x_ref, wg_ref, wu_ref, cwg_ref, cbg_ref, cwu_ref, cbu_ref, o_ref,
                   xx_scr, ug_scr, uu_scr, *, tm):
    i = pl.program_id(0)
    j = pl.program_id(1)
    halo = BF16_ROWS

    @pl.when(j == 0)
    def _():
        xx_scr[0:halo, :] = jnp.where(i > 0, xp_ref[...], jnp.zeros_like(xp_ref[...]))
        xx_scr[halo:halo + tm, :] = x_ref[...]

    xx = xx_scr[...]
    ug_scr[...] = _dot_nn(xx, wg_ref[...])
    uu_scr[...] = _dot_nn(xx, wu_ref[...])

    def conv(scr, w_ref, bias_ref):
        acc = bias_ref[...]
        for k in range(FFN_CONV):
            o = halo - (FFN_CONV - 1) + k
            acc = acc + w_ref[k:k + 1, :] * scr[o:o + tm, :]
        return acc

    gate = conv(ug_scr, cwg_ref, cbg_ref)
    up = conv(uu_scr, cwu_ref, cbu_ref)
    o_ref[...] = (_silu(gate) * up).astype(o_ref.dtype)


def _ffn_up(xb, w_up, conv_w, conv_b, tm, tj):
    s, d = xb.shape
    nj = D_FF // tj
    rb = tm // BF16_ROWS
    return pl.pallas_call(
        functools.partial(_ffn_up_kernel, tm=tm),
        grid=(s // tm, nj),
        in_specs=[pl.BlockSpec((BF16_ROWS, d), lambda i, j: (jnp.maximum(i * rb - 1, 0), 0)),
                  pl.BlockSpec((tm, d), lambda i, j: (i, 0)),
                  pl.BlockSpec((d, tj), lambda i, j: (0, j)),
                  pl.BlockSpec((d, tj), lambda i, j: (0, nj + j)),
                  pl.BlockSpec((FFN_CONV, tj), lambda i, j: (0, j)),
                  pl.BlockSpec((1, tj), lambda i, j: (0, j)),
                  pl.BlockSpec((FFN_CONV, tj), lambda i, j: (0, nj + j)),
                  pl.BlockSpec((1, tj), lambda i, j: (0, nj + j))],
        out_specs=pl.BlockSpec((tm, tj), lambda i, j: (i, j)),
        out_shape=jax.ShapeDtypeStruct((s, D_FF), BF16),
        scratch_shapes=[pltpu.VMEM((tm + BF16_ROWS, d), BF16),
                        pltpu.VMEM((tm + BF16_ROWS, tj), F32),
                        pltpu.VMEM((tm + BF16_ROWS, tj), F32)],
        compiler_params=_params("parallel", "arbitrary"),
        name="ffn_up_glu",
    )(xb, xb, w_up, w_up, conv_w, conv_b, conv_w, conv_b)


def _ffn_down_kernel(a_ref, w_ref, x_ref, g_ref, b_ref, of_ref, ob_ref, acc_ref):
    k = pl.program_id(1)

    @pl.when(k == 0)
    def _():
        acc_ref[...] = DEEPNORM_ALPHA * x_ref[...]

    acc_ref[...] += _dot_nn(a_ref[...], w_ref[...])

    @pl.when(k == pl.num_programs(1) - 1)
    def _():
        out = _layer_norm(acc_ref[...], g_ref[...], b_ref[...])
        of_ref[...] = out
        ob_ref[...] = out.astype(BF16)


def _ffn_down(a, w, x, g, b, tm, tk):
    s, d = x.shape
    return pl.pallas_call(
        _ffn_down_kernel,
        grid=(s // tm, a.shape[1] // tk),
        in_specs=[pl.BlockSpec((tm, tk), lambda i, k: (i, k)),
                  pl.BlockSpec((tk, d), lambda i, k: (k, 0)),
                  pl.BlockSpec((tm, d), lambda i, k: (i, 0)),
                  pl.BlockSpec((1, d), lambda i, k: (0, 0)),
                  pl.BlockSpec((1, d), lambda i, k: (0, 0))],
        out_specs=[pl.BlockSpec((tm, d), lambda i, k: (i, 0)),
                   pl.BlockSpec((tm, d), lambda i, k: (i, 0))],
        out_shape=[jax.ShapeDtypeStruct((s, d), F32), jax.ShapeDtypeStruct((s, d), BF16)],
        scratch_shapes=[pltpu.VMEM((tm, d), F32)],
        compiler_params=_params("parallel", "arbitrary"),
        name="ffn_down_ln",
    )(a, w, x, g, b)


def _pad_cols(w, width):
    return jnp.pad(w, ((0, 0), (0, width - w.shape[1])))


def _pack_w_in(w_in):
    sizes = (MLA_Q_LORA, MLA_KV_LORA, MLA_ROPE, SSM_D_INNER, SSM_CONV_DIM, SSM_HEADS,
             MOBA_W, MOBA_W, MOBA_W, N_BRANCH * D_MODEL)
    offs = [0]
    for sz in sizes:
        offs.append(offs[-1] + sz)
    p = [w_in[:, offs[i]:offs[i + 1]] for i in range(len(sizes))]
    cq, ckv, kr, z, xbc, dt, qc, kc, vc, g = p
    out = jnp.concatenate([cq, ckv, _pad_cols(kr, LANES), _pad_cols(dt, LANES), z, xbc, qc, kc, vc, g],
                          axis=1)
    assert out.shape[1] == H_COLS
    return out.astype(BF16)


def _pack_mla_weights(w_uq, w_ukv):
    wq = w_uq.reshape(MLA_Q_LORA, MLA_HEADS, MLA_DK)
    rope_part = jnp.pad(wq[:, :, :MLA_ROPE], ((0, 0), (0, 0), (0, LANES - MLA_ROPE)))
    nope_part = wq[:, :, MLA_ROPE:]
    wq_p = jnp.concatenate([rope_part.reshape(MLA_Q_LORA, -1), nope_part.reshape(MLA_Q_LORA, -1)], axis=1)
    wkv = w_ukv.reshape(MLA_KV_LORA, MLA_HEADS, MLA_NOPE + MLA_V)
    wkv_p = jnp.concatenate([wkv[:, :, :MLA_NOPE].reshape(MLA_KV_LORA, -1),
                             wkv[:, :, MLA_NOPE:].reshape(MLA_KV_LORA, -1)], axis=1)
    return wq_p.astype(BF16), wkv_p.astype(BF16)


def _rope_tables(s, rot_dim):
    half = rot_dim // 2
    inv_freq = ROPE_THETA ** (-jnp.arange(half, dtype=F32) / half)
    ang = jnp.arange(s, dtype=jnp.int32).astype(F32)[:, None] * inv_freq[None, :]
    cos, sin = jnp.cos(ang), jnp.sin(ang)
    cos_g = jnp.concatenate([cos, cos], axis=1)
    sin_g = jnp.concatenate([-sin, sin], axis=1)
    if rot_dim == MLA_ROPE:
        reps = LANES // rot_dim
        return jnp.tile(cos_g, (1, reps)), jnp.tile(sin_g, (1, reps))
    rest = LANES - rot_dim
    return (jnp.concatenate([cos_g, jnp.ones((s, rest), F32)], axis=1),
            jnp.concatenate([sin_g, jnp.zeros((s, rest), F32)], axis=1))


def _layer(x, xb, p, tabs):
    s = x.shape[0]
    mla_cos, mla_sin, moba_cos, moba_sin, onehot = tabs
    h = _matmul(xb, p["w_in"], F32, min(s, 1024), 1024, "in_proj")

    q, k, v = _mla_prep(h, p["mla_q_norm"], p["mla_kv_norm"], p["w_uq"], p["w_ukv"],
                        mla_cos, mla_sin, 256)
    ya = _mla_attn(q, k, v, 512)

    yb = _ssd(h, p["ssm_conv_w"], p["ssm_conv_b"], p["ssm_dt_bias"], p["ssm_a_log"],
              p["ssm_d"], p["ssm_norm"])

    mq, mk, mv, km = _moba_prep(h, moba_cos, moba_sin)
    nb = s // MOBA_BLOCK
    km = jnp.pad(km.reshape(nb, MOBA_W), ((0, -nb % BF16_ROWS), (0, 0)))
    yc = _moba_attn(mq, mk, mv, km, onehot, 512)

    merged = _merge(ya, yb, yc, p["w_branch_a"], p["w_branch_b"], p["w_branch_c"], h,
                    p["gate_bias"], min(s, 512), 512)
    x1, x1b = _proj_ln(merged, p["w_out"], x, p["ln1_g"], p["ln1_b"], 256)
    a = _ffn_up(x1b, p["ffn_w_up"], p["ffn_conv_w"], p["ffn_conv_b"], min(s, 1024), 512)
    return _ffn_down(a, p["ffn_w_down"], x1, p["ln2_g"], p["ln2_b"], min(s, 512), 512)


def kernel(x, w_in, mla_q_norm, mla_w_uq, mla_kv_norm, mla_w_ukv, ssm_conv_w, ssm_conv_b, ssm_dt_bias, ssm_a_log, ssm_d, ssm_norm, w_branch_a, w_branch_b, w_branch_c, gate_bias, w_out, ln1_g, ln1_b, ffn_w_up, ffn_conv_w, ffn_conv_b, ffn_w_down, ln2_g, ln2_b):
    b, s, d = x.shape
    assert b == 1 and d == D_MODEL
    assert s % 1024 == 0 and s // MOBA_BLOCK <= LANES
    depth = w_in.shape[0]

    tabs = _rope_tables(s, MLA_ROPE) + _rope_tables(s, MOBA_ROT)
    blk = jnp.arange(s, dtype=jnp.int32)[:, None] // MOBA_BLOCK
    onehot = (blk == jnp.arange(LANES, dtype=jnp.int32)[None, :]).astype(BF16)
    tabs = tabs + (onehot,)

    xf = x.reshape(s, d)
    xb = xf.astype(BF16)
    for l in range(depth):
        wuq, wukv = _pack_mla_weights(mla_w_uq[l], mla_w_ukv[l])
        p = {
            "w_in": _pack_w_in(w_in[l]),
            "mla_q_norm": mla_q_norm[l].reshape(1, -1),
            "mla_kv_norm": mla_kv_norm[l].reshape(1, -1),
            "w_uq": wuq, "w_ukv": wukv,
            "ssm_conv_w": ssm_conv_w[l],
            "ssm_conv_b": ssm_conv_b[l].reshape(1, -1),
            "ssm_dt_bias": _pad_cols(ssm_dt_bias[l].reshape(1, -1), LANES),
            "ssm_a_log": _pad_cols(ssm_a_log[l].reshape(1, -1), LANES),
            "ssm_d": jnp.repeat(ssm_d[l], SSM_HEADDIM).reshape(1, -1),
            "ssm_norm": ssm_norm[l].reshape(1, -1),
            "w_branch_a": w_branch_a[l].astype(BF16),
            "w_branch_b": w_branch_b[l].astype(BF16),
            "w_branch_c": w_branch_c[l].astype(BF16),
            "gate_bias": gate_bias[l].reshape(1, -1),
            "w_out": w_out[l].astype(BF16),
            "ln1_g": ln1_g[l].reshape(1, -1), "ln1_b": ln1_b[l].reshape(1, -1),
            "ffn_w_up": ffn_w_up[l].astype(BF16),
            "ffn_conv_w": ffn_conv_w[l],
            "ffn_conv_b": ffn_conv_b[l].reshape(1, -1),
            "ffn_w_down": ffn_w_down[l].astype(BF16),
            "ln2_g": ln2_g[l].reshape(1, -1), "ln2_b": ln2_b[l].reshape(1, -1),
        }
        xf, xb = _layer(xf, xb, p, tabs)
    return xf.reshape(b, s, d)
```

```python
import functools

import jax
import jax.numpy as jnp
from jax import lax
from jax.experimental import pallas as pl
from jax.experimental.pallas import tpu as pltpu

F32 = jnp.float32
BF16 = jnp.bfloat16

D_MODEL = 2048
DEPTH = 2
ROPE_THETA = 500000.0
LN_EPS = 1e-5
RMS_EPS = 1e-6
NEG_BIG = -1e30
LOG2E = 1.4426950408889634

MLA_HEADS = 8
MLA_Q_LORA = 512
MLA_KV_LORA = 256
MLA_NOPE = 128
MLA_ROPE = 64
MLA_V = 128
MLA_DK = MLA_NOPE + MLA_ROPE

SSM_D_INNER = D_MODEL
SSM_HEADDIM = 64
SSM_HEADS = SSM_D_INNER // SSM_HEADDIM
SSM_GROUPS = 4
SSM_STATE = 128
SSM_CONV = 4
SSM_CHUNK = 256
SSM_CONV_DIM = SSM_D_INNER + 2 * SSM_GROUPS * SSM_STATE
SSM_GROUP_W = SSM_D_INNER // SSM_GROUPS
SSM_GROUP_HEADS = SSM_HEADS // SSM_GROUPS

MOBA_HEADS = 8
MOBA_HEAD_DIM = 128
MOBA_ROT = MOBA_HEAD_DIM // 4
MOBA_BLOCK = 256
MOBA_TOPK = 3
MOBA_W = MOBA_HEADS * MOBA_HEAD_DIM

D_FF = 5632
FFN_CONV = 3
N_BRANCH = 3
DEEPNORM_ALPHA = (2 * DEPTH) ** 0.25

LANES = 128
SUBLANES = 8
BF16_ROWS = 16
VMEM_LIMIT = 56 * 1024 * 1024

IN_SIZES = (MLA_Q_LORA, MLA_KV_LORA, MLA_ROPE, SSM_D_INNER, SSM_CONV_DIM, SSM_HEADS,
            MOBA_W, MOBA_W, MOBA_W, N_BRANCH * D_MODEL)
COL_CQ = 0
COL_CKV = COL_CQ + MLA_Q_LORA
COL_KR = COL_CKV + MLA_KV_LORA
COL_DT = COL_KR + LANES
HA_COLS = COL_DT + LANES
COL_Z = 0
COL_XS = COL_Z + SSM_D_INNER
COL_B = COL_XS + SSM_D_INNER
COL_C = COL_B + SSM_GROUPS * SSM_STATE
HB_COLS = COL_C + SSM_GROUPS * SSM_STATE
COL_MOBA = 0
COL_G = COL_MOBA + 3 * MOBA_W
HC_COLS = COL_G + N_BRANCH * D_MODEL


def _params(*sem):
    return pltpu.CompilerParams(dimension_semantics=sem, vmem_limit_bytes=VMEM_LIMIT)


def _sigmoid(x):
    return 1.0 / (1.0 + jnp.exp(-x))


def _silu(x):
    return x * _sigmoid(x)


def _split3(a):
    a1 = a.astype(BF16)
    r1 = a - a1.astype(F32)
    a2 = r1.astype(BF16)
    a3 = (r1 - a2.astype(F32)).astype(BF16)
    return a1, a2, a3


def _dot_nn(a, b):
    return jnp.dot(a, b, preferred_element_type=F32)


def _dot_nt(a, b):
    return lax.dot_general(a, b, (((1,), (1,)), ((), ())), preferred_element_type=F32)


def _layer_norm(y, g, b):
    mu = jnp.mean(y, axis=-1, keepdims=True)
    d = y - mu
    var = jnp.mean(d * d, axis=-1, keepdims=True)
    return d * lax.rsqrt(var + LN_EPS) * g + b


def _mm_kernel(x_ref, w_ref, o_ref):
    o_ref[...] = _dot_nn(x_ref[...], w_ref[...]).astype(o_ref.dtype)


def _matmul(x, w, out_dtype, tm, tn, name):
    m, k = x.shape
    n = w.shape[1]
    return pl.pallas_call(
        _mm_kernel,
        grid=(n // tn, m // tm),
        in_specs=[pl.BlockSpec((tm, k), lambda j, i: (i, 0)),
                  pl.BlockSpec((k, tn), lambda j, i: (0, j))],
        out_specs=pl.BlockSpec((tm, tn), lambda j, i: (i, j)),
        out_shape=jax.ShapeDtypeStruct((m, n), out_dtype),
        compiler_params=_params("parallel", "parallel"),
        name=name,
    )(x, w)


def _rope_swap(xs, half):
    lane = lax.broadcasted_iota(jnp.int32, xs.shape, 1)
    first = (lane & (2 * half - 1)) < half
    return jnp.where(first, pltpu.roll(xs, LANES - half, 1), pltpu.roll(xs, half, 1))


def _mla_prep_kernel(h_ref, qn_ref, kvn_ref, wuq_ref, wukv_ref, cos_ref, sin_ref,
                     qt_ref, k_ref, vt_ref):
    hm = h_ref[...]
    cq = hm[:, COL_CQ:COL_CQ + MLA_Q_LORA]
    ckv = hm[:, COL_CKV:COL_CKV + MLA_KV_LORA]
    kr = hm[:, COL_KR:COL_KR + LANES]
    nq = cq * lax.rsqrt(jnp.mean(cq * cq, axis=-1, keepdims=True) + RMS_EPS) * qn_ref[...]
    nkv = ckv * lax.rsqrt(jnp.mean(ckv * ckv, axis=-1, keepdims=True) + RMS_EPS) * kvn_ref[...]
    qu = _dot_nn(nq.astype(BF16), wuq_ref[...])
    kvu = _dot_nn(nkv.astype(BF16), wukv_ref[...])
    cos = cos_ref[...]
    sin = sin_ref[...]

    def rope(xs):
        return xs * cos + _rope_swap(xs, MLA_ROPE // 2) * sin

    scale = MLA_DK ** -0.5 * LOG2E
    kpe = rope(kr)[:, :MLA_ROPE].astype(BF16)
    nope0 = MLA_HEADS * LANES
    for h in range(MLA_HEADS):
        lo, hi = h * LANES, (h + 1) * LANES
        qr = rope(qu[:, lo:hi])
        qt_ref[h, 0:MLA_NOPE, :] = (qu[:, nope0 + lo:nope0 + hi] * scale).T.astype(BF16)
        qt_ref[h, MLA_NOPE:MLA_DK, :] = (qr * scale).T[:MLA_ROPE, :].astype(BF16)
        k_ref[h, :, 0:MLA_NOPE] = kvu[:, lo:hi].astype(BF16)
        k_ref[h, :, MLA_NOPE:MLA_DK] = kpe
        vt_ref[h] = kvu[:, nope0 + lo:nope0 + hi].T.astype(BF16)


def _mla_prep(h, q_norm, kv_norm, wuq_p, wukv_p, cos_t, sin_t, tm):
    s = h.shape[0]
    hd = MLA_HEADS
    return pl.pallas_call(
        _mla_prep_kernel,
        grid=(s // tm,),
        in_specs=[pl.BlockSpec((tm, HA_COLS), lambda i: (i, 0)),
                  pl.BlockSpec((1, MLA_Q_LORA), lambda i: (0, 0)),
                  pl.BlockSpec((1, MLA_KV_LORA), lambda i: (0, 0)),
                  pl.BlockSpec(wuq_p.shape, lambda i: (0, 0)),
                  pl.BlockSpec(wukv_p.shape, lambda i: (0, 0)),
                  pl.BlockSpec((tm, LANES), lambda i: (i, 0)),
                  pl.BlockSpec((tm, LANES), lambda i: (i, 0))],
        out_specs=[pl.BlockSpec((hd, MLA_DK, tm), lambda i: (0, 0, i)),
                   pl.BlockSpec((hd, tm, MLA_DK), lambda i: (0, i, 0)),
                   pl.BlockSpec((hd, MLA_V, tm), lambda i: (0, 0, i))],
        out_shape=[jax.ShapeDtypeStruct((hd, MLA_DK, s), BF16),
                   jax.ShapeDtypeStruct((hd, s, MLA_DK), BF16),
                   jax.ShapeDtypeStruct((hd, MLA_V, s), BF16)],
        compiler_params=_params("parallel"),
        name="mla_prep",
    )(h, q_norm, kv_norm, wuq_p, wukv_p, cos_t, sin_t)


SOFTMAX_ROWS = 64


def _fold_rows(x, op):
    out = x[0:SUBLANES]
    for r in range(1, x.shape[0] // SUBLANES):
        out = op(out, x[r * SUBLANES:(r + 1) * SUBLANES])
    return out


def _softmax_tile(s_ref, p_ref, m, l, t, causal):
    rows = SOFTMAX_ROWS

    def chunk(c):
        blk = s_ref[c * rows:(c + 1) * rows, :]
        if causal:
            key = c * rows + lax.broadcasted_iota(jnp.int32, (rows, t), 0)
            qry = lax.broadcasted_iota(jnp.int32, (rows, t), 1)
            blk = jnp.where(key <= qry, blk, -jnp.inf)
        return blk

    mx = None
    for c in range(t // rows):
        part = _fold_rows(chunk(c), jnp.maximum)
        mx = part if mx is None else jnp.maximum(mx, part)
    m_new = jnp.maximum(m, jnp.max(mx, axis=0, keepdims=True))
    alpha = jnp.exp2(m - m_new)
    tot = None
    for c in range(t // rows):
        p = jnp.exp2(chunk(c) - m_new)
        part = _fold_rows(p, jnp.add)
        tot = part if tot is None else tot + part
        p_ref[c * rows:(c + 1) * rows, :] = p.astype(BF16)
    l_new = alpha * l + jnp.sum(tot, axis=0, keepdims=True)
    return alpha, m_new, l_new


def _causal_flash(scores, values, write_out, i, t, scratch):
    s_bufs, p_bufs, acc_scr = scratch[0:2], scratch[2:4], scratch[4]
    s_bufs[0][...] = scores(0)
    p_bufs[1][...] = jnp.zeros((t, t), BF16)
    acc_scr[...] = jnp.zeros(acc_scr.shape, F32)

    def accumulate(a_prev, j_prev, p_ref):
        acc_scr[...] = a_prev * acc_scr[...] + _dot_nn(values(jnp.maximum(j_prev, 0)), p_ref[...])

    def step(j, carry, cur):
        a_prev, m, l = carry
        s_bufs[1 - cur][...] = scores(j + 1)
        accumulate(a_prev, j - 1, p_bufs[1 - cur])
        return _softmax_tile(s_bufs[cur], p_bufs[cur], m, l, t, causal=False)

    def pair(jj, carry):
        return step(2 * jj + 1, step(2 * jj, carry, 0), 1)

    def finish(carry, cur):
        a_prev, m, l = carry
        accumulate(a_prev, i - 1, p_bufs[1 - cur])
        alpha, m, l = _softmax_tile(s_bufs[cur], p_bufs[cur], m, l, t, causal=True)
        accumulate(alpha, i, p_bufs[cur])
        write_out(acc_scr[...] / l)

    init = (jnp.ones((1, t), F32), jnp.full((1, t), -jnp.inf, F32), jnp.zeros((1, t), F32))
    carry = lax.fori_loop(0, i // 2, pair, init)

    @pl.when(i % 2 == 0)
    def _():
        finish(carry, 0)

    @pl.when(i % 2 == 1)
    def _():
        finish(step(i - 1, carry, 0), 1)


def _flash_scratch(t, dv):
    return [pltpu.VMEM((t, t), F32), pltpu.VMEM((t, t), F32),
            pltpu.VMEM((t, t), BF16), pltpu.VMEM((t, t), BF16), pltpu.VMEM((dv, t), F32)]


def _mla_attn_kernel(qt_ref, k_ref, vt_ref, o_ref, *scratch, t):
    i = pl.program_id(1)
    qt = qt_ref[0]

    def scores(j):
        return _dot_nn(k_ref[0, pl.ds(pl.multiple_of(j * t, t), t), :], qt)

    def values(j):
        return vt_ref[0, :, pl.ds(pl.multiple_of(j * t, t), t)]

    def write_out(out):
        o_ref[...] = out.T.astype(o_ref.dtype)

    _causal_flash(scores, values, write_out, i, t, scratch)


def _mla_attn(qt, k, vt, t):
    hd, _, s = qt.shape
    return pl.pallas_call(
        functools.partial(_mla_attn_kernel, t=t),
        grid=(hd, s // t),
        in_specs=[pl.BlockSpec((1, MLA_DK, t), lambda h, i: (h, 0, i)),
                  pl.BlockSpec((1, s, MLA_DK), lambda h, i: (h, 0, 0)),
                  pl.BlockSpec((1, MLA_V, s), lambda h, i: (h, 0, 0))],
        out_specs=pl.BlockSpec((t, MLA_V), lambda h, i: (i, h)),
        out_shape=jax.ShapeDtypeStruct((s, hd * MLA_V), BF16),
        scratch_shapes=_flash_scratch(t, MLA_V),
        compiler_params=_params("parallel", "parallel"),
        name="mla_attn",
    )(qt, k, vt)


def _moba_prep_kernel(h_ref, cos_ref, sin_ref, q_ref, k_ref, vt_ref, km_ref):
    cos = cos_ref[...]
    sin = sin_ref[...]

    def rope(xs):
        return xs * cos + _rope_swap(xs, MOBA_ROT // 2) * sin

    for h in range(MOBA_HEADS):
        lo, hi = h * LANES, (h + 1) * LANES
        q_ref[:, lo:hi] = rope(h_ref[:, lo:hi])
        kr = rope(h_ref[:, MOBA_W + lo:MOBA_W + hi])
        k_ref[:, lo:hi] = kr.astype(BF16)
        km_ref[0, :, lo:hi] = jnp.mean(kr, axis=0, keepdims=True)
        vt_ref[lo:hi, :] = h_ref[:, 2 * MOBA_W + lo:2 * MOBA_W + hi].T.astype(BF16)


def _moba_prep(h, cos_t, sin_t):
    s = h.shape[0]
    t = MOBA_BLOCK
    nb = s // t
    return pl.pallas_call(
        _moba_prep_kernel,
        grid=(nb,),
        in_specs=[pl.BlockSpec((t, 3 * MOBA_W), lambda i: (i, COL_MOBA // (3 * MOBA_W))),
                  pl.BlockSpec((t, LANES), lambda i: (i, 0)),
                  pl.BlockSpec((t, LANES), lambda i: (i, 0))],
        out_specs=[pl.BlockSpec((t, MOBA_W), lambda i: (i, 0)),
                   pl.BlockSpec((t, MOBA_W), lambda i: (i, 0)),
                   pl.BlockSpec((MOBA_W, t), lambda i: (0, i)),
                   pl.BlockSpec((1, 1, MOBA_W), lambda i: (i, 0, 0))],
        out_shape=[jax.ShapeDtypeStruct((s, MOBA_W), F32),
                   jax.ShapeDtypeStruct((s, MOBA_W), BF16),
                   jax.ShapeDtypeStruct((MOBA_W, s), BF16),
                   jax.ShapeDtypeStruct((nb, 1, MOBA_W), F32)],
        compiler_params=_params("parallel"),
        name="moba_prep",
    )(h, cos_t, sin_t)


def _moba_attn_kernel(q_ref, k_ref, vt_ref, km_ref, oh_ref, o_ref, *scratch, t):
    i = pl.program_id(1)
    slots = km_ref.shape[0]
    qt = q_ref[...].T
    q1, q2, q3 = _split3(qt)
    k1, k2, k3 = _split3(km_ref[...])
    gate = (_dot_nn(k1, q1) + _dot_nn(k2, q1) + _dot_nn(k1, q2)
            + _dot_nn(k3, q1) + _dot_nn(k2, q2) + _dot_nn(k1, q3))
    slot = lax.broadcasted_iota(jnp.int32, (slots, t), 0)
    slot_f = slot.astype(F32)
    qpos = i * t + lax.broadcasted_iota(jnp.int32, (1, t), 1)
    own = jnp.right_shift(qpos, MOBA_BLOCK.bit_length() - 1)
    g = jnp.where(slot < own, gate, NEG_BIG)
    picked = slot < 0
    for _ in range(MOBA_TOPK):
        mx = jnp.max(g, axis=0, keepdims=True)
        first = jnp.min(jnp.where(g == mx, slot_f, float(slots)), axis=0, keepdims=True)
        hit = slot_f == first
        picked = picked | hit
        g = jnp.where(hit, -jnp.inf, g)
    visible = (picked & (slot < own)) | (slot == own)
    parts = [qt * (MOBA_HEAD_DIM ** -0.5 * LOG2E), jnp.where(visible, 0.0, NEG_BIG)]
    if slots < LANES:
        parts.append(jnp.zeros((LANES - slots, t), F32))
    qxt = jnp.concatenate(parts, axis=0).astype(BF16)

    def scores(j):
        off = pl.multiple_of(j * t, t)
        kx = jnp.concatenate([k_ref[pl.ds(off, t), :], oh_ref[pl.ds(off, t), :]], axis=1)
        return _dot_nn(kx, qxt)

    def values(j):
        return vt_ref[:, pl.ds(pl.multiple_of(j * t, t), t)]

    def write_out(out):
        o_ref[...] = out.T.astype(o_ref.dtype)

    _causal_flash(scores, values, write_out, i, t, scratch)


def _moba_attn(q, k, vt, kmean, onehot, t):
    s = q.shape[0]
    assert t % MOBA_BLOCK == 0
    return pl.pallas_call(
        functools.partial(_moba_attn_kernel, t=t),
        grid=(MOBA_HEADS, s // t),
        in_specs=[pl.BlockSpec((t, LANES), lambda h, i: (i, h)),
                  pl.BlockSpec((s, LANES), lambda h, i: (0, h)),
                  pl.BlockSpec((LANES, s), lambda h, i: (h, 0)),
                  pl.BlockSpec((kmean.shape[0], LANES), lambda h, i: (0, h)),
                  pl.BlockSpec((s, LANES), lambda h, i: (0, 0))],
        out_specs=pl.BlockSpec((t, LANES), lambda h, i: (i, h)),
        out_shape=jax.ShapeDtypeStruct((s, MOBA_W), BF16),
        scratch_shapes=_flash_scratch(t, MOBA_HEAD_DIM),
        compiler_params=_params("parallel", "parallel"),
        name="moba_attn",
    )(q, k, vt, kmean, onehot)


def _softplus(x):
    return jnp.maximum(x, 0.0) + jnp.log1p(jnp.exp(-jnp.abs(x)))


def _ssd_kernel(xp_ref, x_ref, bp_ref, b_ref, cp_ref, c_ref, dt_ref, z_ref,
                cwx_ref, cbx_ref, cwb_ref, cbb_ref, cwc_ref, cbc_ref,
                dtb_ref, alog_ref, dexp_ref, nw_ref, o_ref,
                xs_scr, bs_scr, cs_scr, st_scr):
    c = pl.program_id(0)
    g = pl.program_id(1)
    L = SSM_CHUNK
    P = SSM_HEADDIM
    halo = SUBLANES

    def conv_silu(prev_ref, cur_ref, scr, w_ref, bias_ref):
        scr[0:halo, :] = jnp.where(c > 0, prev_ref[...], 0.0)
        scr[halo:halo + L, :] = cur_ref[...]
        acc = bias_ref[...]
        for k in range(SSM_CONV):
            o = halo - (SSM_CONV - 1) + k
            acc = acc + w_ref[k:k + 1, :] * scr[o:o + L, :]
        return _silu(acc)

    xs = conv_silu(xp_ref, x_ref, xs_scr, cwx_ref, cbx_ref)
    bm = conv_silu(bp_ref, b_ref, bs_scr, cwb_ref, cbb_ref)
    cm = conv_silu(cp_ref, c_ref, cs_scr, cwc_ref, cbc_ref)

    dtv = _softplus(dt_ref[...] + dtb_ref[...])
    av = dtv * (-jnp.exp(alog_ref[...]))
    row = lax.broadcasted_iota(jnp.int32, (L, L), 0)
    col = lax.broadcasted_iota(jnp.int32, (L, L), 1)
    tril = col <= row
    ones_tril = jnp.where(tril, 1.0, 0.0).astype(BF16)
    a1, a2, a3 = _split3(av)
    acum = _dot_nn(ones_tril, a1) + _dot_nn(ones_tril, a2) + _dot_nn(ones_tril, a3)

    eh = lax.broadcasted_iota(jnp.int32, (LANES, SSM_GROUP_W), 0)
    ej = lax.broadcasted_iota(jnp.int32, (LANES, SSM_GROUP_W), 1)
    expand = jnp.where(eh == g * SSM_GROUP_HEADS + jnp.right_shift(ej, P.bit_length() - 1),
                       1.0, 0.0).astype(BF16)

    def expand3(v):
        v1, v2, v3 = _split3(v)
        return _dot_nn(v1, expand) + _dot_nn(v2, expand) + _dot_nn(v3, expand)

    dt_e = expand3(dtv)
    ac_e = expand3(acum)
    sr = lax.broadcasted_iota(jnp.int32, (BF16_ROWS, LANES), 0)
    sl = lax.broadcasted_iota(jnp.int32, (BF16_ROWS, LANES), 1)
    pick = jnp.where(sl == g * SSM_GROUP_HEADS + sr, 1.0, 0.0).astype(BF16)
    c1, c2, c3 = _split3(acum)
    ac_t = _dot_nt(pick, c1) + _dot_nt(pick, c2) + _dot_nt(pick, c3)

    bb = bm.astype(BF16)
    cb = cm.astype(BF16)
    gmat = _dot_nt(cb, bb)
    xdt = xs * dt_e
    xdt_b = xdt.astype(BF16)
    lane = lax.broadcasted_iota(jnp.int32, (L, LANES), 1)
    lo_half = lane < P
    parts = []
    for pr in range(SSM_GROUP_HEADS // 2):
        xpair = xdt_b[:, pr * LANES:(pr + 1) * LANES]
        acc = None
        for half in range(2):
            r = 2 * pr + half
            seg = ac_e[:, r * P:r * P + 1] - ac_t[r:r + 1, :]
            mh = (gmat * jnp.exp(jnp.where(tril, seg, -jnp.inf))).astype(BF16)
            xh = jnp.where(lo_half if half == 0 else jnp.logical_not(lo_half), xpair,
                           jnp.zeros_like(xpair))
            term = _dot_nn(mh, xh)
            acc = term if acc is None else acc + term
        parts.append(acc)
    y = jnp.concatenate(parts, axis=1)

    last = ac_e[L - 1:L, :]
    xw = (xdt * jnp.exp(last - ac_e)).astype(BF16)
    s_new = _dot_nn(bm.T.astype(BF16), xw)

    @pl.when(c == 0)
    def _():
        st_scr[g] = jnp.zeros((SSM_STATE, SSM_GROUP_W), F32)

    prev = st_scr[g]
    y = y + _dot_nn(cb, prev.astype(BF16)) * jnp.exp(ac_e)
    st_scr[g] = prev * jnp.exp(last) + s_new

    y = y + xs * dexp_ref[...]
    y = y * _silu(z_ref[...])
    y = y * lax.rsqrt(jnp.mean(y * y, axis=-1, keepdims=True) + RMS_EPS) * nw_ref[...]
    o_ref[...] = y.astype(o_ref.dtype)


def _ssd(h_a, h_b, conv_w, conv_b, dt_bias_p, a_log_p, d_exp, norm_w):
    s = h_b.shape[0]
    L = SSM_CHUNK
    nc = s // L
    gw = SSM_GROUP_W
    n = SSM_STATE
    rb = L // SUBLANES

    def prev_map(cblk):
        return lambda c, g: (jnp.maximum(c * rb - 1, 0), cblk + g)

    def cur_map(cblk):
        return lambda c, g: (c, cblk + g)

    xblk, bblk, cblk = COL_XS // gw, COL_B // n, COL_C // n
    wb, wc = SSM_D_INNER // n, (SSM_D_INNER + SSM_GROUPS * n) // n
    in_specs = [
        pl.BlockSpec((SUBLANES, gw), prev_map(xblk)), pl.BlockSpec((L, gw), cur_map(xblk)),
        pl.BlockSpec((SUBLANES, n), prev_map(bblk)), pl.BlockSpec((L, n), cur_map(bblk)),
        pl.BlockSpec((SUBLANES, n), prev_map(cblk)), pl.BlockSpec((L, n), cur_map(cblk)),
        pl.BlockSpec((L, LANES), lambda c, g: (c, COL_DT // LANES)),
        pl.BlockSpec((L, gw), cur_map(COL_Z // gw)),
        pl.BlockSpec((SSM_CONV, gw), lambda c, g: (0, g)), pl.BlockSpec((1, gw), lambda c, g: (0, g)),
        pl.BlockSpec((SSM_CONV, n), lambda c, g: (0, wb + g)), pl.BlockSpec((1, n), lambda c, g: (0, wb + g)),
        pl.BlockSpec((SSM_CONV, n), lambda c, g: (0, wc + g)), pl.BlockSpec((1, n), lambda c, g: (0, wc + g)),
        pl.BlockSpec((1, LANES), lambda c, g: (0, 0)),
        pl.BlockSpec((1, LANES), lambda c, g: (0, 0)),
        pl.BlockSpec((1, gw), lambda c, g: (0, g)),
        pl.BlockSpec((1, gw), lambda c, g: (0, g)),
    ]
    return pl.pallas_call(
        _ssd_kernel,
        grid=(nc, SSM_GROUPS),
        in_specs=in_specs,
        out_specs=pl.BlockSpec((L, gw), lambda c, g: (c, g)),
        out_shape=jax.ShapeDtypeStruct((s, SSM_D_INNER), BF16),
        scratch_shapes=[pltpu.VMEM((L + SUBLANES, gw), F32),
                        pltpu.VMEM((L + SUBLANES, n), F32),
                        pltpu.VMEM((L + SUBLANES, n), F32),
                        pltpu.VMEM((SSM_GROUPS, n, gw), F32)],
        compiler_params=_params("arbitrary", "arbitrary"),
        name="ssd",
    )(h_b, h_b, h_b, h_b, h_b, h_b, h_a, h_b, conv_w, conv_b, conv_w, conv_b, conv_w, conv_b,
      dt_bias_p, a_log_p, d_exp, norm_w)


SUB_ROWS = 256


def _merge_kernel(ya_ref, yb_ref, yc_ref, wa_ref, wb_ref, wc_ref,
                  g0_ref, g1_ref, g2_ref, b0_ref, b1_ref, b2_ref, o_ref):
    def branch(rows, y_ref, w_ref, g_ref, b_ref):
        return _sigmoid(g_ref[rows, :] + b_ref[...]) * _dot_nn(y_ref[rows, :], w_ref[...])

    for r in range(o_ref.shape[0] // SUB_ROWS):
        rows = slice(r * SUB_ROWS, (r + 1) * SUB_ROWS)
        out = (branch(rows, ya_ref, wa_ref, g0_ref, b0_ref)
               + branch(rows, yb_ref, wb_ref, g1_ref, b1_ref)
               + branch(rows, yc_ref, wc_ref, g2_ref, b2_ref))
        o_ref[rows, :] = out.astype(o_ref.dtype)


def _merge(ya, yb, yc, wa, wb, wc, h, gate_bias, tm, tn):
    s = ya.shape[0]
    nt = D_MODEL // tn
    gblk = COL_G // tn

    def gmap(b):
        return lambda i, j: (i, gblk + b * nt + j)

    def bmap(b):
        return lambda i, j: (0, b * nt + j)

    return pl.pallas_call(
        _merge_kernel,
        grid=(s // tm, nt),
        in_specs=[pl.BlockSpec((tm, ya.shape[1]), lambda i, j: (i, 0)),
                  pl.BlockSpec((tm, yb.shape[1]), lambda i, j: (i, 0)),
                  pl.BlockSpec((tm, yc.shape[1]), lambda i, j: (i, 0)),
                  pl.BlockSpec((wa.shape[0], tn), lambda i, j: (0, j)),
                  pl.BlockSpec((wb.shape[0], tn), lambda i, j: (0, j)),
                  pl.BlockSpec((wc.shape[0], tn), lambda i, j: (0, j)),
                  pl.BlockSpec((tm, tn), gmap(0)), pl.BlockSpec((tm, tn), gmap(1)),
                  pl.BlockSpec((tm, tn), gmap(2)),
                  pl.BlockSpec((1, tn), bmap(0)), pl.BlockSpec((1, tn), bmap(1)),
                  pl.BlockSpec((1, tn), bmap(2))],
        out_specs=pl.BlockSpec((tm, tn), lambda i, j: (i, j)),
        out_shape=jax.ShapeDtypeStruct((s, D_MODEL), BF16),
        compiler_params=_params("parallel", "parallel"),
        name="branch_merge",
    )(ya, yb, yc, wa, wb, wc, h, h, h, gate_bias, gate_bias, gate_bias)


def _proj_ln_kernel(m_ref, w_ref, x_ref, g_ref, b_ref, of_ref, ob_ref):
    for r in range(of_ref.shape[0] // SUB_ROWS):
        rows = slice(r * SUB_ROWS, (r + 1) * SUB_ROWS)
        y = DEEPNORM_ALPHA * x_ref[rows, :] + _dot_nn(m_ref[rows, :], w_ref[...])
        out = _layer_norm(y, g_ref[...], b_ref[...])
        of_ref[rows, :] = out
        ob_ref[rows, :] = out.astype(BF16)


def _proj_ln(m, w, x, g, b, tm):
    s, d = x.shape
    return pl.pallas_call(
        _proj_ln_kernel,
        grid=(s // tm,),
        in_specs=[pl.BlockSpec((tm, m.shape[1]), lambda i: (i, 0)),
                  pl.BlockSpec(w.shape, lambda i: (0, 0)),
                  pl.BlockSpec((tm, d), lambda i: (i, 0)),
                  pl.BlockSpec((1, d), lambda i: (0, 0)),
                  pl.BlockSpec((1, d), lambda i: (0, 0))],
        out_specs=[pl.BlockSpec((tm, d), lambda i: (i, 0)),
                   pl.BlockSpec((tm, d), lambda i: (i, 0))],
        out_shape=[jax.ShapeDtypeStruct((s, d), F32), jax.ShapeDtypeStruct((s, d), BF16)],
        compiler_params=_params("parallel"),
        name="out_proj_ln",
    )(m, w, x, g, b)


def _ffn_up_kernel(x_ref, wg_ref, wu_ref, cwg_ref, cbg_ref, cwu_ref, cbu_ref, o_ref,
                   wgb_scr, wub_scr, ug_scr, uu_scr, *, tm):
    i = pl.program_id(1)
    hist = SUBLANES

    @pl.when(i == 0)
    def _():
        wgb_scr[...] = wg_ref[...].astype(BF16)
        wub_scr[...] = wu_ref[...].astype(BF16)
        ug_scr[0:hist, :] = jnp.zeros((hist, ug_scr.shape[1]), F32)
        uu_scr[0:hist, :] = jnp.zeros((hist, uu_scr.shape[1]), F32)

    def conv(scr, w_ref, bias_ref, r0):
        acc = bias_ref[...]
        for k in range(FFN_CONV):
            o = hist + r0 - (FFN_CONV - 1) + k
            acc = acc + w_ref[k:k + 1, :] * scr[o:o + SUB_ROWS, :]
        return acc

    for r in range(tm // SUB_ROWS):
        r0 = r * SUB_ROWS
        xs = x_ref[r0:r0 + SUB_ROWS, :]
        ug_scr[hist + r0:hist + r0 + SUB_ROWS, :] = _dot_nn(xs, wgb_scr[...])
        uu_scr[hist + r0:hist + r0 + SUB_ROWS, :] = _dot_nn(xs, wub_scr[...])
        gate = conv(ug_scr, cwg_ref, cbg_ref, r0)
        up = conv(uu_scr, cwu_ref, cbu_ref, r0)
        o_ref[r0:r0 + SUB_ROWS, :] = (_silu(gate) * up).astype(o_ref.dtype)

    ug_scr[0:hist, :] = ug_scr[tm:tm + hist, :]
    uu_scr[0:hist, :] = uu_scr[tm:tm + hist, :]


def _ffn_up(xb, w_up, conv_w, conv_b, tm, tj):
    s, d = xb.shape
    nj = D_FF // tj
    return pl.pallas_call(
        functools.partial(_ffn_up_kernel, tm=tm),
        grid=(nj, s // tm),
        in_specs=[pl.BlockSpec((tm, d), lambda j, i: (i, 0)),
                  pl.BlockSpec((d, tj), lambda j, i: (0, j)),
                  pl.BlockSpec((d, tj), lambda j, i: (0, nj + j)),
                  pl.BlockSpec((FFN_CONV, tj), lambda j, i: (0, j)),
                  pl.BlockSpec((1, tj), lambda j, i: (0, j)),
                  pl.BlockSpec((FFN_CONV, tj), lambda j, i: (0, nj + j)),
                  pl.BlockSpec((1, tj), lambda j, i: (0, nj + j))],
        out_specs=pl.BlockSpec((tm, tj), lambda j, i: (i, j)),
        out_shape=jax.ShapeDtypeStruct((s, D_FF), BF16),
        scratch_shapes=[pltpu.VMEM((d, tj), BF16), pltpu.VMEM((d, tj), BF16),
                        pltpu.VMEM((tm + SUBLANES, tj), F32),
                        pltpu.VMEM((tm + SUBLANES, tj), F32)],
        compiler_params=_params("arbitrary", "arbitrary"),
        name="ffn_up_glu",
    )(xb, w_up, w_up, conv_w, conv_b, conv_w, conv_b)


def _ffn_down_kernel(a_ref, w_ref, x_ref, g_ref, b_ref, of_ref, ob_ref, acc_ref):
    k = pl.program_id(1)

    @pl.when(k == 0)
    def _():
        acc_ref[...] = DEEPNORM_ALPHA * x_ref[...]

    acc_ref[...] += _dot_nn(a_ref[...], w_ref[...])

    @pl.when(k == pl.num_programs(1) - 1)
    def _():
        out = _layer_norm(acc_ref[...], g_ref[...], b_ref[...])
        of_ref[...] = out
        ob_ref[...] = out.astype(BF16)


def _ffn_down(a, w, x, g, b, tm, tk):
    s, d = x.shape
    return pl.pallas_call(
        _ffn_down_kernel,
        grid=(s // tm, a.shape[1] // tk),
        in_specs=[pl.BlockSpec((tm, tk), lambda i, k: (i, k)),
                  pl.BlockSpec((tk, d), lambda i, k: (k, 0)),
                  pl.BlockSpec((tm, d), lambda i, k: (i, 0)),
                  pl.BlockSpec((1, d), lambda i, k: (0, 0)),
                  pl.BlockSpec((1, d), lambda i, k: (0, 0))],
        out_specs=[pl.BlockSpec((tm, d), lambda i, k: (i, 0)),
                   pl.BlockSpec((tm, d), lambda i, k: (i, 0))],
        out_shape=[jax.ShapeDtypeStruct((s, d), F32), jax.ShapeDtypeStruct((s, d), BF16)],
        scratch_shapes=[pltpu.VMEM((tm, d), F32)],
        compiler_params=_params("parallel", "arbitrary"),
        name="ffn_down_ln",
    )(a, w, x, g, b)


def _pad_cols(w, width):
    return jnp.pad(w, ((0, 0), (0, width - w.shape[1])))


def _split_w_in(w_in):
    offs = [0]
    for sz in IN_SIZES:
        offs.append(offs[-1] + sz)
    mla_end, z0, dt0, dt1 = offs[3], offs[3], offs[5], offs[6]
    w_a = jnp.concatenate([_pad_cols(w_in[:, :mla_end], COL_DT), _pad_cols(w_in[:, dt0:dt1], LANES)],
                          axis=1)
    w_b = w_in[:, z0:dt0]
    w_c = w_in[:, dt1:]
    assert (w_a.shape[1], w_b.shape[1], w_c.shape[1]) == (HA_COLS, HB_COLS, HC_COLS)
    return w_a.astype(BF16), w_b.astype(BF16), w_c.astype(BF16)


def _pack_mla_weights(w_uq, w_ukv):
    wq = w_uq.reshape(MLA_Q_LORA, MLA_HEADS, MLA_DK)
    rope_part = jnp.pad(wq[:, :, :MLA_ROPE], ((0, 0), (0, 0), (0, LANES - MLA_ROPE)))
    nope_part = wq[:, :, MLA_ROPE:]
    wq_p = jnp.concatenate([rope_part.reshape(MLA_Q_LORA, -1), nope_part.reshape(MLA_Q_LORA, -1)], axis=1)
    wkv = w_ukv.reshape(MLA_KV_LORA, MLA_HEADS, MLA_NOPE + MLA_V)
    wkv_p = jnp.concatenate([wkv[:, :, :MLA_NOPE].reshape(MLA_KV_LORA, -1),
                             wkv[:, :, MLA_NOPE:].reshape(MLA_KV_LORA, -1)], axis=1)
    return wq_p.astype(BF16), wkv_p.astype(BF16)


def _rope_tables(s, rot_dim):
    half = rot_dim // 2
    inv_freq = ROPE_THETA ** (-jnp.arange(half, dtype=F32) / half)
    ang = jnp.arange(s, dtype=jnp.int32).astype(F32)[:, None] * inv_freq[None, :]
    cos, sin = jnp.cos(ang), jnp.sin(ang)
    cos_g = jnp.concatenate([cos, cos], axis=1)
    sin_g = jnp.concatenate([-sin, sin], axis=1)
    if rot_dim == MLA_ROPE:
        reps = LANES // rot_dim
        return jnp.tile(cos_g, (1, reps)), jnp.tile(sin_g, (1, reps))
    rest = LANES - rot_dim
    return (jnp.concatenate([cos_g, jnp.ones((s, rest), F32)], axis=1),
            jnp.concatenate([sin_g, jnp.zeros((s, rest), F32)], axis=1))


def _layer(x, xb, p, tabs):
    s = x.shape[0]
    mla_cos, mla_sin, moba_cos, moba_sin, onehot = tabs
    w_a, w_b, w_c = p["w_in"]
    tm = min(s, 1024)
    h_a = _matmul(xb, w_a, F32, tm, HA_COLS, "in_proj_a")
    h_b = _matmul(xb, w_b, F32, tm, 1024, "in_proj_b")
    h_c = _matmul(xb, w_c, F32, tm, 1024, "in_proj_c")

    qt, k, vt = _mla_prep(h_a, p["mla_q_norm"], p["mla_kv_norm"], p["w_uq"], p["w_ukv"],
                          mla_cos, mla_sin, 256)
    ya = _mla_attn(qt, k, vt, 512)

    yb = _ssd(h_a, h_b, p["ssm_conv_w"], p["ssm_conv_b"], p["ssm_dt_bias"], p["ssm_a_log"],
              p["ssm_d"], p["ssm_norm"])

    mq, mk, mvt, km = _moba_prep(h_c, moba_cos, moba_sin)
    nb = s // MOBA_BLOCK
    km = jnp.pad(km.reshape(nb, MOBA_W), ((0, -nb % BF16_ROWS), (0, 0)))
    yc = _moba_attn(mq, mk, mvt, km, onehot, 512)

    merged = _merge(ya, yb, yc, p["w_branch_a"], p["w_branch_b"], p["w_branch_c"], h_c,
                    p["gate_bias"], tm, 512)
    x1, x1b = _proj_ln(merged, p["w_out"], x, p["ln1_g"], p["ln1_b"], 512)
    a = _ffn_up(x1b, p["ffn_w_up"], p["ffn_conv_w"], p["ffn_conv_b"], tm, 512)
    return _ffn_down(a, p["ffn_w_down"], x1, p["ln2_g"], p["ln2_b"], 512, D_FF // 4)


def kernel(x, w_in, mla_q_norm, mla_w_uq, mla_kv_norm, mla_w_ukv, ssm_conv_w, ssm_conv_b, ssm_dt_bias, ssm_a_log, ssm_d, ssm_norm, w_branch_a, w_branch_b, w_branch_c, gate_bias, w_out, ln1_g, ln1_b, ffn_w_up, ffn_conv_w, ffn_conv_b, ffn_w_down, ln2_g, ln2_b):
    b, s, d = x.shape
    assert b == 1 and d == D_MODEL
    assert s % 1024 == 0 and s // MOBA_BLOCK <= LANES
    depth = w_in.shape[0]

    tabs = _rope_tables(s, MLA_ROPE) + _rope_tables(s, MOBA_ROT)
    blk = jnp.arange(s, dtype=jnp.int32)[:, None] // MOBA_BLOCK
    onehot = (blk == jnp.arange(LANES, dtype=jnp.int32)[None, :]).astype(BF16)
    tabs = tabs + (onehot,)

    xf = x.reshape(s, d)
    xb = xf.astype(BF16)
    for l in range(depth):
        wuq, wukv = _pack_mla_weights(mla_w_uq[l], mla_w_ukv[l])
        p = {
            "w_in": _split_w_in(w_in[l]),
            "mla_q_norm": mla_q_norm[l].reshape(1, -1),
            "mla_kv_norm": mla_kv_norm[l].reshape(1, -1),
            "w_uq": wuq, "w_ukv": wukv,
            "ssm_conv_w": ssm_conv_w[l],
            "ssm_conv_b": ssm_conv_b[l].reshape(1, -1),
            "ssm_dt_bias": _pad_cols(ssm_dt_bias[l].reshape(1, -1), LANES),
            "ssm_a_log": _pad_cols(ssm_a_log[l].reshape(1, -1), LANES),
            "ssm_d": jnp.repeat(ssm_d[l], SSM_HEADDIM).reshape(1, -1),
            "ssm_norm": ssm_norm[l].reshape(1, -1),
            "w_branch_a": w_branch_a[l].astype(BF16),
            "w_branch_b": w_branch_b[l].astype(BF16),
            "w_branch_c": w_branch_c[l].astype(BF16),
            "gate_bias": gate_bias[l].reshape(1, -1),
            "w_out": w_out[l].astype(BF16),
            "ln1_g": ln1_g[l].reshape(1, -1), "ln1_b": ln1_b[l].reshape(1, -1),
            "ffn_w_up": ffn_w_up[l],
            "ffn_conv_w": ffn_conv_w[l],
            "ffn_conv_b": ffn_conv_b[l].reshape(1, -1),
            "ffn_w_down": ffn_w_down[l].astype(BF16),
            "ln2_g": ln2_g[l].reshape(1, -1), "ln2_b": ln2_b[l].reshape(1, -1),
        }
        xf, xb = _layer(xf, xb, p, tabs)
    return xf.reshape(b, s, d)
```

```python
import functools

import jax
import jax.numpy as jnp
from jax import lax
from jax.experimental import pallas as pl
from jax.experimental.pallas import tpu as pltpu

F32 = jnp.float32
BF16 = jnp.bfloat16

D_MODEL = 2048
DEPTH = 2
ROPE_THETA = 500000.0
LN_EPS = 1e-5
RMS_EPS = 1e-6
NEG_BIG = -1e30
LOG2E = 1.4426950408889634

MLA_HEADS = 8
MLA_Q_LORA = 512
MLA_KV_LORA = 256
MLA_NOPE = 128
MLA_ROPE = 64
MLA_V = 128
MLA_DK = MLA_NOPE + MLA_ROPE

SSM_D_INNER = D_MODEL
SSM_HEADDIM = 64
SSM_HEADS = SSM_D_INNER // SSM_HEADDIM
SSM_GROUPS = 4
SSM_STATE = 128
SSM_CONV = 4
SSM_CHUNK = 256
SSM_CONV_DIM = SSM_D_INNER + 2 * SSM_GROUPS * SSM_STATE
SSM_GROUP_W = SSM_D_INNER // SSM_GROUPS
SSM_GROUP_HEADS = SSM_HEADS // SSM_GROUPS

MOBA_HEADS = 8
MOBA_HEAD_DIM = 128
MOBA_ROT = MOBA_HEAD_DIM // 4
MOBA_BLOCK = 256
MOBA_TOPK = 3
MOBA_W = MOBA_HEADS * MOBA_HEAD_DIM

D_FF = 5632
FFN_CONV = 3
N_BRANCH = 3
DEEPNORM_ALPHA = (2 * DEPTH) ** 0.25

LANES = 128
SUBLANES = 8
BF16_ROWS = 16
VMEM_LIMIT = 56 * 1024 * 1024

IN_SIZES = (MLA_Q_LORA, MLA_KV_LORA, MLA_ROPE, SSM_D_INNER, SSM_CONV_DIM, SSM_HEADS,
            MOBA_W, MOBA_W, MOBA_W, N_BRANCH * D_MODEL)
COL_CQ = 0
COL_CKV = COL_CQ + MLA_Q_LORA
COL_KR = COL_CKV + MLA_KV_LORA
COL_DT = COL_KR + LANES
HA_COLS = COL_DT + LANES
COL_Z = 0
COL_XS = COL_Z + SSM_D_INNER
COL_B = COL_XS + SSM_D_INNER
COL_C = COL_B + SSM_GROUPS * SSM_STATE
HB_COLS = COL_C + SSM_GROUPS * SSM_STATE
COL_MOBA = 0
COL_G = COL_MOBA + 3 * MOBA_W
HC_COLS = COL_G + N_BRANCH * D_MODEL


def _params(*sem):
    return pltpu.CompilerParams(dimension_semantics=sem, vmem_limit_bytes=VMEM_LIMIT)


def _sigmoid(x):
    return 1.0 / (1.0 + jnp.exp(-x))


def _silu(x):
    return x * _sigmoid(x)


def _split3(a):
    a1 = a.astype(BF16)
    r1 = a - a1.astype(F32)
    a2 = r1.astype(BF16)
    a3 = (r1 - a2.astype(F32)).astype(BF16)
    return a1, a2, a3


def _dot_nn(a, b):
    return jnp.dot(a, b, preferred_element_type=F32)


def _dot_nt(a, b):
    return lax.dot_general(a, b, (((1,), (1,)), ((), ())), preferred_element_type=F32)


def _layer_norm(y, g, b):
    mu = jnp.mean(y, axis=-1, keepdims=True)
    d = y - mu
    var = jnp.mean(d * d, axis=-1, keepdims=True)
    return d * lax.rsqrt(var + LN_EPS) * g + b


def _mm_kernel(x_ref, w_ref, o_ref):
    o_ref[...] = _dot_nn(x_ref[...], w_ref[...]).astype(o_ref.dtype)


def _matmul(x, w, out_dtype, tm, tn, name):
    m, k = x.shape
    n = w.shape[1]
    return pl.pallas_call(
        _mm_kernel,
        grid=(n // tn, m // tm),
        in_specs=[pl.BlockSpec((tm, k), lambda j, i: (i, 0)),
                  pl.BlockSpec((k, tn), lambda j, i: (0, j))],
        out_specs=pl.BlockSpec((tm, tn), lambda j, i: (i, j)),
        out_shape=jax.ShapeDtypeStruct((m, n), out_dtype),
        compiler_params=_params("parallel", "parallel"),
        name=name,
    )(x, w)


def _mm_wslice_kernel(x_ref, wlo_ref, whi_ref, o_ref, w_scr, *, off):
    tn = o_ref.shape[1]

    @pl.when(pl.program_id(1) == 0)
    def _():
        both = jnp.concatenate([wlo_ref[...], whi_ref[...]], axis=1)
        w_scr[...] = both[:, off:off + tn].astype(BF16)

    o_ref[...] = _dot_nn(x_ref[...], w_scr[...]).astype(o_ref.dtype)


def _matmul_wslice(x, w3, layer, c0, n, out_dtype, tm, tn, name):
    m, k = x.shape
    base, off = divmod(c0, tn)
    assert n % tn == 0 and (base + n // tn) * tn < w3.shape[2] + tn
    return pl.pallas_call(
        functools.partial(_mm_wslice_kernel, off=off),
        grid=(n // tn, m // tm),
        in_specs=[pl.BlockSpec((tm, k), lambda j, i: (i, 0)),
                  pl.BlockSpec((None, k, tn), lambda j, i: (layer, 0, base + j)),
                  pl.BlockSpec((None, k, tn), lambda j, i: (layer, 0, base + j + 1))],
        out_specs=pl.BlockSpec((tm, tn), lambda j, i: (i, j)),
        out_shape=jax.ShapeDtypeStruct((m, n), out_dtype),
        scratch_shapes=[pltpu.VMEM((k, tn), BF16)],
        compiler_params=_params("arbitrary", "arbitrary"),
        name=name,
    )(x, w3, w3)


def _rope_swap(xs, half):
    lane = lax.broadcasted_iota(jnp.int32, xs.shape, 1)
    first = (lane & (2 * half - 1)) < half
    return jnp.where(first, pltpu.roll(xs, LANES - half, 1), pltpu.roll(xs, half, 1))


def _mla_prep_kernel(h_ref, qn_ref, kvn_ref, wuq_ref, wukv_ref, cos_ref, sin_ref,
                     qt_ref, k_ref, vt_ref):
    hm = h_ref[...]
    cq = hm[:, COL_CQ:COL_CQ + MLA_Q_LORA]
    ckv = hm[:, COL_CKV:COL_CKV + MLA_KV_LORA]
    kr = hm[:, COL_KR:COL_KR + LANES]
    nq = cq * lax.rsqrt(jnp.mean(cq * cq, axis=-1, keepdims=True) + RMS_EPS) * qn_ref[...]
    nkv = ckv * lax.rsqrt(jnp.mean(ckv * ckv, axis=-1, keepdims=True) + RMS_EPS) * kvn_ref[...]
    qu = _dot_nn(nq.astype(BF16), wuq_ref[...])
    kvu = _dot_nn(nkv.astype(BF16), wukv_ref[...])
    cos = cos_ref[...]
    sin = sin_ref[...]

    def rope(xs):
        return xs * cos + _rope_swap(xs, MLA_ROPE // 2) * sin

    scale = MLA_DK ** -0.5 * LOG2E
    kpe = rope(kr)[:, :MLA_ROPE].astype(BF16)
    nope0 = MLA_HEADS * LANES
    for h in range(MLA_HEADS):
        lo, hi = h * LANES, (h + 1) * LANES
        qr = rope(qu[:, lo:hi])
        qt_ref[h, 0:MLA_NOPE, :] = (qu[:, nope0 + lo:nope0 + hi] * scale).T.astype(BF16)
        qt_ref[h, MLA_NOPE:MLA_DK, :] = (qr * scale).T[:MLA_ROPE, :].astype(BF16)
        k_ref[h, :, 0:MLA_NOPE] = kvu[:, lo:hi].astype(BF16)
        k_ref[h, :, MLA_NOPE:MLA_DK] = kpe
        vt_ref[h] = kvu[:, nope0 + lo:nope0 + hi].T.astype(BF16)


def _mla_prep(h, q_norm, kv_norm, wuq_p, wukv_p, cos_t, sin_t, tm):
    s = h.shape[0]
    hd = MLA_HEADS
    return pl.pallas_call(
        _mla_prep_kernel,
        grid=(s // tm,),
        in_specs=[pl.BlockSpec((tm, HA_COLS), lambda i: (i, 0)),
                  pl.BlockSpec((1, MLA_Q_LORA), lambda i: (0, 0)),
                  pl.BlockSpec((1, MLA_KV_LORA), lambda i: (0, 0)),
                  pl.BlockSpec(wuq_p.shape, lambda i: (0, 0)),
                  pl.BlockSpec(wukv_p.shape, lambda i: (0, 0)),
                  pl.BlockSpec((tm, LANES), lambda i: (i, 0)),
                  pl.BlockSpec((tm, LANES), lambda i: (i, 0))],
        out_specs=[pl.BlockSpec((hd, MLA_DK, tm), lambda i: (0, 0, i)),
                   pl.BlockSpec((hd, tm, MLA_DK), lambda i: (0, i, 0)),
                   pl.BlockSpec((hd, MLA_V, tm), lambda i: (0, 0, i))],
        out_shape=[jax.ShapeDtypeStruct((hd, MLA_DK, s), BF16),
                   jax.ShapeDtypeStruct((hd, s, MLA_DK), BF16),
                   jax.ShapeDtypeStruct((hd, MLA_V, s), BF16)],
        compiler_params=_params("parallel"),
        name="mla_prep",
    )(h, q_norm, kv_norm, wuq_p, wukv_p, cos_t, sin_t)


SOFTMAX_ROWS = 64


def _fold_rows(x, op):
    out = x[0:SUBLANES]
    for r in range(1, x.shape[0] // SUBLANES):
        out = op(out, x[r * SUBLANES:(r + 1) * SUBLANES])
    return out


def _softmax_tile(s_ref, p_ref, m, l, t, causal):
    rows = SOFTMAX_ROWS

    def chunk(c):
        blk = s_ref[c * rows:(c + 1) * rows, :]
        if causal:
            key = c * rows + lax.broadcasted_iota(jnp.int32, (rows, t), 0)
            qry = lax.broadcasted_iota(jnp.int32, (rows, t), 1)
            blk = jnp.where(key <= qry, blk, -jnp.inf)
        return blk

    mx = None
    for c in range(t // rows):
        part = _fold_rows(chunk(c), jnp.maximum)
        mx = part if mx is None else jnp.maximum(mx, part)
    m_new = jnp.maximum(m, jnp.max(mx, axis=0, keepdims=True))
    alpha = jnp.exp2(m - m_new)
    tot = None
    for c in range(t // rows):
        p = jnp.exp2(chunk(c) - m_new)
        part = _fold_rows(p, jnp.add)
        tot = part if tot is None else tot + part
        p_ref[c * rows:(c + 1) * rows, :] = p.astype(BF16)
    l_new = alpha * l + jnp.sum(tot, axis=0, keepdims=True)
    return alpha, m_new, l_new


FLASH_HEADS = 2
FLASH_BUFS = 5


def _causal_flash(streams, i, t):
    def accumulate(st, a_prev, j_prev, p_ref):
        _, values, _, scratch = st
        acc_scr = scratch[4]
        acc_scr[...] = a_prev * acc_scr[...] + _dot_nn(values(jnp.maximum(j_prev, 0)), p_ref[...])

    def step_one(st, j, carry, cur):
        scores, _, _, scratch = st
        s_bufs, p_bufs = scratch[0:2], scratch[2:4]
        a_prev, m, l = carry
        s_bufs[1 - cur][...] = scores(j + 1)
        accumulate(st, a_prev, j - 1, p_bufs[1 - cur])
        return _softmax_tile(s_bufs[cur], p_bufs[cur], m, l, t, causal=False)

    def finish_one(st, carry, cur):
        _, _, write_out, scratch = st
        s_bufs, p_bufs, acc_scr = scratch[0:2], scratch[2:4], scratch[4]
        a_prev, m, l = carry
        accumulate(st, a_prev, i - 1, p_bufs[1 - cur])
        alpha, m, l = _softmax_tile(s_bufs[cur], p_bufs[cur], m, l, t, causal=True)
        accumulate(st, alpha, i, p_bufs[cur])
        write_out(acc_scr[...] / l)

    def step(j, carries, cur):
        return tuple(step_one(st, j, c, cur) for st, c in zip(streams, carries))

    def pair(jj, carries):
        return step(2 * jj + 1, step(2 * jj, carries, 0), 1)

    def finish(carries, cur):
        for st, c in zip(streams, carries):
            finish_one(st, c, cur)

    for scores, _, _, scratch in streams:
        scratch[0][...] = scores(0)
        scratch[3][...] = jnp.zeros((t, t), BF16)
        scratch[4][...] = jnp.zeros(scratch[4].shape, F32)

    init = (jnp.ones((1, t), F32), jnp.full((1, t), -jnp.inf, F32), jnp.zeros((1, t), F32))
    carries = lax.fori_loop(0, i // 2, pair, (init,) * len(streams))

    @pl.when(i % 2 == 0)
    def _():
        finish(carries, 0)

    @pl.when(i % 2 == 1)
    def _():
        finish(step(i - 1, carries, 0), 1)


def _flash_scratch(t, dv):
    per_head = [pltpu.VMEM((t, t), F32), pltpu.VMEM((t, t), F32),
                pltpu.VMEM((t, t), BF16), pltpu.VMEM((t, t), BF16), pltpu.VMEM((dv, t), F32)]
    assert len(per_head) == FLASH_BUFS
    return per_head * FLASH_HEADS


def _mla_attn_kernel(qt_ref, k_ref, vt_ref, o_ref, *scratch, t):
    i = pl.program_id(1)

    def stream(a):
        qt = qt_ref[a]

        def scores(j):
            return _dot_nn(k_ref[a, pl.ds(pl.multiple_of(j * t, t), t), :], qt)

        def values(j):
            return vt_ref[a, :, pl.ds(pl.multiple_of(j * t, t), t)]

        def write_out(out):
            o_ref[:, a * MLA_V:(a + 1) * MLA_V] = out.T.astype(o_ref.dtype)

        return scores, values, write_out, scratch[a * FLASH_BUFS:(a + 1) * FLASH_BUFS]

    _causal_flash([stream(a) for a in range(FLASH_HEADS)], i, t)


def _mla_attn(qt, k, vt, t):
    hd, _, s = qt.shape
    nh = FLASH_HEADS
    return pl.pallas_call(
        functools.partial(_mla_attn_kernel, t=t),
        grid=(hd // nh, s // t),
        in_specs=[pl.BlockSpec((nh, MLA_DK, t), lambda h, i: (h, 0, i)),
                  pl.BlockSpec((nh, s, MLA_DK), lambda h, i: (h, 0, 0)),
                  pl.BlockSpec((nh, MLA_V, s), lambda h, i: (h, 0, 0))],
        out_specs=pl.BlockSpec((t, nh * MLA_V), lambda h, i: (i, h)),
        out_shape=jax.ShapeDtypeStruct((s, hd * MLA_V), BF16),
        scratch_shapes=_flash_scratch(t, MLA_V),
        compiler_params=_params("parallel", "parallel"),
        name="mla_attn",
    )(qt, k, vt)


def _moba_prep_kernel(h_ref, cos_ref, sin_ref, q_ref, k_ref, vt_ref, km_ref):
    cos = cos_ref[...]
    sin = sin_ref[...]

    def rope(xs):
        return xs * cos + _rope_swap(xs, MOBA_ROT // 2) * sin

    for h in range(MOBA_HEADS):
        lo, hi = h * LANES, (h + 1) * LANES
        q_ref[:, lo:hi] = rope(h_ref[:, lo:hi])
        kr = rope(h_ref[:, MOBA_W + lo:MOBA_W + hi])
        k_ref[:, lo:hi] = kr.astype(BF16)
        km_ref[0, :, lo:hi] = jnp.mean(kr, axis=0, keepdims=True)
        vt_ref[lo:hi, :] = h_ref[:, 2 * MOBA_W + lo:2 * MOBA_W + hi].T.astype(BF16)


def _moba_prep(h, cos_t, sin_t):
    s = h.shape[0]
    t = MOBA_BLOCK
    nb = s // t
    return pl.pallas_call(
        _moba_prep_kernel,
        grid=(nb,),
        in_specs=[pl.BlockSpec((t, 3 * MOBA_W), lambda i: (i, COL_MOBA // (3 * MOBA_W))),
                  pl.BlockSpec((t, LANES), lambda i: (i, 0)),
                  pl.BlockSpec((t, LANES), lambda i: (i, 0))],
        out_specs=[pl.BlockSpec((t, MOBA_W), lambda i: (i, 0)),
                   pl.BlockSpec((t, MOBA_W), lambda i: (i, 0)),
                   pl.BlockSpec((MOBA_W, t), lambda i: (0, i)),
                   pl.BlockSpec((1, 1, MOBA_W), lambda i: (i, 0, 0))],
        out_shape=[jax.ShapeDtypeStruct((s, MOBA_W), F32),
                   jax.ShapeDtypeStruct((s, MOBA_W), BF16),
                   jax.ShapeDtypeStruct((MOBA_W, s), BF16),
                   jax.ShapeDtypeStruct((nb, 1, MOBA_W), F32)],
        compiler_params=_params("parallel"),
        name="moba_prep",
    )(h, cos_t, sin_t)


def _moba_attn_kernel(q_ref, k_ref, vt_ref, km_ref, oh_ref, o_ref, *scratch, t):
    i = pl.program_id(1)
    slots = km_ref.shape[0]
    slot = lax.broadcasted_iota(jnp.int32, (slots, t), 0)
    slot_f = slot.astype(F32)
    qpos = i * t + lax.broadcasted_iota(jnp.int32, (1, t), 1)
    own = jnp.right_shift(qpos, MOBA_BLOCK.bit_length() - 1)

    def stream(a):
        cols = slice(a * LANES, (a + 1) * LANES)
        qt = q_ref[:, cols].T
        q1, q2, q3 = _split3(qt)
        k1, k2, k3 = _split3(km_ref[:, cols])
        gate = (_dot_nn(k1, q1) + _dot_nn(k2, q1) + _dot_nn(k1, q2)
                + _dot_nn(k3, q1) + _dot_nn(k2, q2) + _dot_nn(k1, q3))
        g = jnp.where(slot < own, gate, NEG_BIG)
        picked = slot < 0
        for _ in range(MOBA_TOPK):
            mx = jnp.max(g, axis=0, keepdims=True)
            first = jnp.min(jnp.where(g == mx, slot_f, float(slots)), axis=0, keepdims=True)
            hit = slot_f == first
            picked = picked | hit
            g = jnp.where(hit, -jnp.inf, g)
        visible = (picked & (slot < own)) | (slot == own)
        parts = [qt * (MOBA_HEAD_DIM ** -0.5 * LOG2E), jnp.where(visible, 0.0, NEG_BIG)]
        if slots < LANES:
            parts.append(jnp.zeros((LANES - slots, t), F32))
        qxt = jnp.concatenate(parts, axis=0).astype(BF16)

        def scores(j):
            off = pl.multiple_of(j * t, t)
            kx = jnp.concatenate([k_ref[pl.ds(off, t), cols], oh_ref[pl.ds(off, t), :]], axis=1)
            return _dot_nn(kx, qxt)

        def values(j):
            return vt_ref[cols, pl.ds(pl.multiple_of(j * t, t), t)]

        def write_out(out):
            o_ref[:, cols] = out.T.astype(o_ref.dtype)

        return scores, values, write_out, scratch[a * FLASH_BUFS:(a + 1) * FLASH_BUFS]

    _causal_flash([stream(a) for a in range(FLASH_HEADS)], i, t)


def _moba_attn(q, k, vt, kmean, onehot, t):
    s = q.shape[0]
    assert t % MOBA_BLOCK == 0
    w = FLASH_HEADS * LANES
    return pl.pallas_call(
        functools.partial(_moba_attn_kernel, t=t),
        grid=(MOBA_HEADS // FLASH_HEADS, s // t),
        in_specs=[pl.BlockSpec((t, w), lambda h, i: (i, h)),
                  pl.BlockSpec((s, w), lambda h, i: (0, h)),
                  pl.BlockSpec((w, s), lambda h, i: (h, 0)),
                  pl.BlockSpec((kmean.shape[0], w), lambda h, i: (0, h)),
                  pl.BlockSpec((s, LANES), lambda h, i: (0, 0))],
        out_specs=pl.BlockSpec((t, w), lambda h, i: (i, h)),
        out_shape=jax.ShapeDtypeStruct((s, MOBA_W), BF16),
        scratch_shapes=_flash_scratch(t, MOBA_HEAD_DIM),
        compiler_params=_params("parallel", "parallel"),
        name="moba_attn",
    )(q, k, vt, kmean, onehot)


def _softplus(x):
    return jnp.maximum(x, 0.0) + jnp.log1p(jnp.exp(-jnp.abs(x)))


def _ssd_kernel(xp_ref, x_ref, bp_ref, b_ref, cp_ref, c_ref, dt_ref, z_ref,
                cwx_ref, cbx_ref, cwb_ref, cbb_ref, cwc_ref, cbc_ref,
                dtb_ref, alog_ref, dexp_ref, nw_ref, o_ref,
                xs_scr, bs_scr, cs_scr, st_scr):
    c = pl.program_id(0)
    g = pl.program_id(1)
    L = SSM_CHUNK
    P = SSM_HEADDIM
    halo = SUBLANES

    def conv_silu(prev_ref, cur_ref, scr, w_ref, bias_ref):
        scr[0:halo, :] = jnp.where(c > 0, prev_ref[...], 0.0)
        scr[halo:halo + L, :] = cur_ref[...]
        acc = bias_ref[...]
        for k in range(SSM_CONV):
            o = halo - (SSM_CONV - 1) + k
            acc = acc + w_ref[k:k + 1, :] * scr[o:o + L, :]
        return _silu(acc)

    xs = conv_silu(xp_ref, x_ref, xs_scr, cwx_ref, cbx_ref)
    bm = conv_silu(bp_ref, b_ref, bs_scr, cwb_ref, cbb_ref)
    cm = conv_silu(cp_ref, c_ref, cs_scr, cwc_ref, cbc_ref)

    dtv = _softplus(dt_ref[...] + dtb_ref[...])
    av = dtv * (-jnp.exp(alog_ref[...]))
    row = lax.broadcasted_iota(jnp.int32, (L, L), 0)
    col = lax.broadcasted_iota(jnp.int32, (L, L), 1)
    tril = col <= row
    ones_tril = jnp.where(tril, 1.0, 0.0).astype(BF16)
    a1, a2, a3 = _split3(av)
    acum = _dot_nn(ones_tril, a1) + _dot_nn(ones_tril, a2) + _dot_nn(ones_tril, a3)

    eh = lax.broadcasted_iota(jnp.int32, (LANES, SSM_GROUP_W), 0)
    ej = lax.broadcasted_iota(jnp.int32, (LANES, SSM_GROUP_W), 1)
    expand = jnp.where(eh == g * SSM_GROUP_HEADS + jnp.right_shift(ej, P.bit_length() - 1),
                       1.0, 0.0).astype(BF16)

    def expand3(v):
        v1, v2, v3 = _split3(v)
        return _dot_nn(v1, expand) + _dot_nn(v2, expand) + _dot_nn(v3, expand)

    dt_e = expand3(dtv)
    ac_e = expand3(acum)
    sr = lax.broadcasted_iota(jnp.int32, (BF16_ROWS, LANES), 0)
    sl = lax.broadcasted_iota(jnp.int32, (BF16_ROWS, LANES), 1)
    pick = jnp.where(sl == g * SSM_GROUP_HEADS + sr, 1.0, 0.0).astype(BF16)
    c1, c2, c3 = _split3(acum)
    ac_t = _dot_nt(pick, c1) + _dot_nt(pick, c2) + _dot_nt(pick, c3)

    bb = bm.astype(BF16)
    cb = cm.astype(BF16)
    gmat = _dot_nt(cb, bb)
    xdt = xs * dt_e
    xdt_b = xdt.astype(BF16)
    lane = lax.broadcasted_iota(jnp.int32, (L, LANES), 1)
    lo_half = lane < P
    parts = []
    for pr in range(SSM_GROUP_HEADS // 2):
        xpair = xdt_b[:, pr * LANES:(pr + 1) * LANES]
        acc = None
        for half in range(2):
            r = 2 * pr + half
            seg = ac_e[:, r * P:r * P + 1] - ac_t[r:r + 1, :]
            mh = (gmat * jnp.exp(jnp.where(tril, seg, -jnp.inf))).astype(BF16)
            xh = jnp.where(lo_half if half == 0 else jnp.logical_not(lo_half), xpair,
                           jnp.zeros_like(xpair))
            term = _dot_nn(mh, xh)
            acc = term if acc is None else acc + term
        parts.append(acc)
    y = jnp.concatenate(parts, axis=1)

    last = ac_e[L - 1:L, :]
    xw = (xdt * jnp.exp(last - ac_e)).astype(BF16)
    s_new = _dot_nn(bm.T.astype(BF16), xw)

    @pl.when(c == 0)
    def _():
        st_scr[g] = jnp.zeros((SSM_STATE, SSM_GROUP_W), F32)

    prev = st_scr[g]
    y = y + _dot_nn(cb, prev.astype(BF16)) * jnp.exp(ac_e)
    st_scr[g] = prev * jnp.exp(last) + s_new

    y = y + xs * dexp_ref[...]
    y = y * _silu(z_ref[...])
    y = y * lax.rsqrt(jnp.mean(y * y, axis=-1, keepdims=True) + RMS_EPS) * nw_ref[...]
    o_ref[...] = y.astype(o_ref.dtype)


def _ssd(h_a, h_b, conv_w, conv_b, dt_bias_p, a_log_p, d_exp, norm_w):
    s = h_b.shape[0]
    L = SSM_CHUNK
    nc = s // L
    gw = SSM_GROUP_W
    n = SSM_STATE
    rb = L // SUBLANES

    def prev_map(cblk):
        return lambda c, g: (jnp.maximum(c * rb - 1, 0), cblk + g)

    def cur_map(cblk):
        return lambda c, g: (c, cblk + g)

    xblk, bblk, cblk = COL_XS // gw, COL_B // n, COL_C // n
    wb, wc = SSM_D_INNER // n, (SSM_D_INNER + SSM_GROUPS * n) // n
    in_specs = [
        pl.BlockSpec((SUBLANES, gw), prev_map(xblk)), pl.BlockSpec((L, gw), cur_map(xblk)),
        pl.BlockSpec((SUBLANES, n), prev_map(bblk)), pl.BlockSpec((L, n), cur_map(bblk)),
        pl.BlockSpec((SUBLANES, n), prev_map(cblk)), pl.BlockSpec((L, n), cur_map(cblk)),
        pl.BlockSpec((L, LANES), lambda c, g: (c, COL_DT // LANES)),
        pl.BlockSpec((L, gw), cur_map(COL_Z // gw)),
        pl.BlockSpec((SSM_CONV, gw), lambda c, g: (0, g)), pl.BlockSpec((1, gw), lambda c, g: (0, g)),
        pl.BlockSpec((SSM_CONV, n), lambda c, g: (0, wb + g)), pl.BlockSpec((1, n), lambda c, g: (0, wb + g)),
        pl.BlockSpec((SSM_CONV, n), lambda c, g: (0, wc + g)), pl.BlockSpec((1, n), lambda c, g: (0, wc + g)),
        pl.BlockSpec((1, LANES), lambda c, g: (0, 0)),
        pl.BlockSpec((1, LANES), lambda c, g: (0, 0)),
        pl.BlockSpec((1, gw), lambda c, g: (0, g)),
        pl.BlockSpec((1, gw), lambda c, g: (0, g)),
    ]
    return pl.pallas_call(
        _ssd_kernel,
        grid=(nc, SSM_GROUPS),
        in_specs=in_specs,
        out_specs=pl.BlockSpec((L, gw), lambda c, g: (c, g)),
        out_shape=jax.ShapeDtypeStruct((s, SSM_D_INNER), BF16),
        scratch_shapes=[pltpu.VMEM((L + SUBLANES, gw), F32),
                        pltpu.VMEM((L + SUBLANES, n), F32),
                        pltpu.VMEM((L + SUBLANES, n), F32),
                        pltpu.VMEM((SSM_GROUPS, n, gw), F32)],
        compiler_params=_params("arbitrary", "arbitrary"),
        name="ssd",
    )(h_b, h_b, h_b, h_b, h_b, h_b, h_a, h_b, conv_w, conv_b, conv_w, conv_b, conv_w, conv_b,
      dt_bias_p, a_log_p, d_exp, norm_w)


SUB_ROWS = 256


def _merge_kernel(ya_ref, yb_ref, yc_ref, wa_ref, wb_ref, wc_ref,
                  g0_ref, g1_ref, g2_ref, b0_ref, b1_ref, b2_ref, o_ref,
                  wa_scr, wb_scr, wc_scr):
    @pl.when(pl.program_id(1) == 0)
    def _():
        wa_scr[...] = wa_ref[...].astype(BF16)
        wb_scr[...] = wb_ref[...].astype(BF16)
        wc_scr[...] = wc_ref[...].astype(BF16)

    def branch(rows, y_ref, w_scr, g_ref, b_ref):
        return _sigmoid(g_ref[rows, :] + b_ref[...]) * _dot_nn(y_ref[rows, :], w_scr[...])

    for r in range(o_ref.shape[0] // SUB_ROWS):
        rows = slice(r * SUB_ROWS, (r + 1) * SUB_ROWS)
        out = (branch(rows, ya_ref, wa_scr, g0_ref, b0_ref)
               + branch(rows, yb_ref, wb_scr, g1_ref, b1_ref)
               + branch(rows, yc_ref, wc_scr, g2_ref, b2_ref))
        o_ref[rows, :] = out.astype(o_ref.dtype)


def _merge(ya, yb, yc, wa, wb, wc, layer, h, gate_bias, tm, tn):
    s = ya.shape[0]
    nt = D_MODEL // tn
    gblk = COL_G // tn

    def gmap(b):
        return lambda j, i: (i, gblk + b * nt + j)

    def bmap(b):
        return lambda j, i: (0, b * nt + j)

    def wspec(w):
        return pl.BlockSpec((None, w.shape[1], tn), lambda j, i: (layer, 0, j))

    return pl.pallas_call(
        _merge_kernel,
        grid=(nt, s // tm),
        in_specs=[pl.BlockSpec((tm, ya.shape[1]), lambda j, i: (i, 0)),
                  pl.BlockSpec((tm, yb.shape[1]), lambda j, i: (i, 0)),
                  pl.BlockSpec((tm, yc.shape[1]), lambda j, i: (i, 0)),
                  wspec(wa), wspec(wb), wspec(wc),
                  pl.BlockSpec((tm, tn), gmap(0)), pl.BlockSpec((tm, tn), gmap(1)),
                  pl.BlockSpec((tm, tn), gmap(2)),
                  pl.BlockSpec((1, tn), bmap(0)), pl.BlockSpec((1, tn), bmap(1)),
                  pl.BlockSpec((1, tn), bmap(2))],
        out_specs=pl.BlockSpec((tm, tn), lambda j, i: (i, j)),
        out_shape=jax.ShapeDtypeStruct((s, D_MODEL), BF16),
        scratch_shapes=[pltpu.VMEM((w.shape[1], tn), BF16) for w in (wa, wb, wc)],
        compiler_params=_params("arbitrary", "arbitrary"),
        name="branch_merge",
    )(ya, yb, yc, wa, wb, wc, h, h, h, gate_bias, gate_bias, gate_bias)


def _proj_ln_kernel(m_ref, w_ref, x_ref, g_ref, b_ref, of_ref, ob_ref):
    for r in range(of_ref.shape[0] // SUB_ROWS):
        rows = slice(r * SUB_ROWS, (r + 1) * SUB_ROWS)
        y = DEEPNORM_ALPHA * x_ref[rows, :] + _dot_nn(m_ref[rows, :], w_ref[...])
        out = _layer_norm(y, g_ref[...], b_ref[...])
        of_ref[rows, :] = out
        ob_ref[rows, :] = out.astype(BF16)


def _proj_ln(m, w, x, g, b, tm):
    s, d = x.shape
    return pl.pallas_call(
        _proj_ln_kernel,
        grid=(s // tm,),
        in_specs=[pl.BlockSpec((tm, m.shape[1]), lambda i: (i, 0)),
                  pl.BlockSpec(w.shape, lambda i: (0, 0)),
                  pl.BlockSpec((tm, d), lambda i: (i, 0)),
                  pl.BlockSpec((1, d), lambda i: (0, 0)),
                  pl.BlockSpec((1, d), lambda i: (0, 0))],
        out_specs=[pl.BlockSpec((tm, d), lambda i: (i, 0)),
                   pl.BlockSpec((tm, d), lambda i: (i, 0))],
        out_shape=[jax.ShapeDtypeStruct((s, d), F32), jax.ShapeDtypeStruct((s, d), BF16)],
        compiler_params=_params("parallel"),
        name="out_proj_ln",
    )(m, w, x, g, b)


def _ffn_up_kernel(x_ref, wg_ref, wu_ref, cwg_ref, cbg_ref, cwu_ref, cbu_ref, o_ref,
                   wgb_scr, wub_scr, ug_scr, uu_scr, *, tm):
    i = pl.program_id(1)
    hist = SUBLANES

    @pl.when(i == 0)
    def _():
        wgb_scr[...] = wg_ref[...].astype(BF16)
        wub_scr[...] = wu_ref[...].astype(BF16)
        ug_scr[0:hist, :] = jnp.zeros((hist, ug_scr.shape[1]), F32)
        uu_scr[0:hist, :] = jnp.zeros((hist, uu_scr.shape[1]), F32)

    def conv(scr, w_ref, bias_ref, r0):
        acc = bias_ref[...]
        for k in range(FFN_CONV):
            o = hist + r0 - (FFN_CONV - 1) + k
            acc = acc + w_ref[k:k + 1, :] * scr[o:o + SUB_ROWS, :]
        return acc

    for r in range(tm // SUB_ROWS):
        r0 = r * SUB_ROWS
        xs = x_ref[r0:r0 + SUB_ROWS, :]
        ug_scr[hist + r0:hist + r0 + SUB_ROWS, :] = _dot_nn(xs, wgb_scr[...])
        uu_scr[hist + r0:hist + r0 + SUB_ROWS, :] = _dot_nn(xs, wub_scr[...])
        gate = conv(ug_scr, cwg_ref, cbg_ref, r0)
        up = conv(uu_scr, cwu_ref, cbu_ref, r0)
        o_ref[r0:r0 + SUB_ROWS, :] = (_silu(gate) * up).astype(o_ref.dtype)

    ug_scr[0:hist, :] = ug_scr[tm:tm + hist, :]
    uu_scr[0:hist, :] = uu_scr[tm:tm + hist, :]


def _ffn_up(xb, w_up, layer, conv_w, conv_b, tm, tj):
    s, d = xb.shape
    nj = D_FF // tj
    return pl.pallas_call(
        functools.partial(_ffn_up_kernel, tm=tm),
        grid=(nj, s // tm),
        in_specs=[pl.BlockSpec((tm, d), lambda j, i: (i, 0)),
                  pl.BlockSpec((None, d, tj), lambda j, i: (layer, 0, j)),
                  pl.BlockSpec((None, d, tj), lambda j, i: (layer, 0, nj + j)),
                  pl.BlockSpec((FFN_CONV, tj), lambda j, i: (0, j)),
                  pl.BlockSpec((1, tj), lambda j, i: (0, j)),
                  pl.BlockSpec((FFN_CONV, tj), lambda j, i: (0, nj + j)),
                  pl.BlockSpec((1, tj), lambda j, i: (0, nj + j))],
        out_specs=pl.BlockSpec((tm, tj), lambda j, i: (i, j)),
        out_shape=jax.ShapeDtypeStruct((s, D_FF), BF16),
        scratch_shapes=[pltpu.VMEM((d, tj), BF16), pltpu.VMEM((d, tj), BF16),
                        pltpu.VMEM((tm + SUBLANES, tj), F32),
                        pltpu.VMEM((tm + SUBLANES, tj), F32)],
        compiler_params=_params("arbitrary", "arbitrary"),
        name="ffn_up_glu",
    )(xb, w_up, w_up, conv_w, conv_b, conv_w, conv_b)


def _ffn_down_kernel(a_ref, w_ref, x_ref, g_ref, b_ref, of_ref, ob_ref, acc_ref):
    k = pl.program_id(1)

    @pl.when(k == 0)
    def _():
        acc_ref[...] = DEEPNORM_ALPHA * x_ref[...]

    acc_ref[...] += _dot_nn(a_ref[...], w_ref[...])

    @pl.when(k == pl.num_programs(1) - 1)
    def _():
        out = _layer_norm(acc_ref[...], g_ref[...], b_ref[...])
        of_ref[...] = out
        ob_ref[...] = out.astype(BF16)


def _ffn_down(a, w, x, g, b, tm, tk):
    s, d = x.shape
    return pl.pallas_call(
        _ffn_down_kernel,
        grid=(s // tm, a.shape[1] // tk),
        in_specs=[pl.BlockSpec((tm, tk), lambda i, k: (i, k)),
                  pl.BlockSpec((tk, d), lambda i, k: (k, 0)),
                  pl.BlockSpec((tm, d), lambda i, k: (i, 0)),
                  pl.BlockSpec((1, d), lambda i, k: (0, 0)),
                  pl.BlockSpec((1, d), lambda i, k: (0, 0))],
        out_specs=[pl.BlockSpec((tm, d), lambda i, k: (i, 0)),
                   pl.BlockSpec((tm, d), lambda i, k: (i, 0))],
        out_shape=[jax.ShapeDtypeStruct((s, d), F32), jax.ShapeDtypeStruct((s, d), BF16)],
        scratch_shapes=[pltpu.VMEM((tm, d), F32)],
        compiler_params=_params("parallel", "arbitrary"),
        name="ffn_down_ln",
    )(a, w, x, g, b)


def _pad_cols(w, width):
    return jnp.pad(w, ((0, 0), (0, width - w.shape[1])))


IN_OFFS = tuple(sum(IN_SIZES[:n]) for n in range(len(IN_SIZES) + 1))
W_IN_B0 = IN_OFFS[3]
W_IN_DT0 = IN_OFFS[5]
W_IN_C0 = IN_OFFS[6]
assert W_IN_DT0 - W_IN_B0 == HB_COLS and IN_OFFS[-1] - W_IN_C0 == HC_COLS


def _pack_w_a(w_in):
    w_a = jnp.concatenate([_pad_cols(w_in[:, :W_IN_B0], COL_DT),
                           _pad_cols(w_in[:, W_IN_DT0:W_IN_C0], LANES)], axis=1)
    assert w_a.shape[1] == HA_COLS
    return w_a.astype(BF16)


def _pack_mla_weights(w_uq, w_ukv):
    wq = w_uq.reshape(MLA_Q_LORA, MLA_HEADS, MLA_DK)
    rope_part = jnp.pad(wq[:, :, :MLA_ROPE], ((0, 0), (0, 0), (0, LANES - MLA_ROPE)))
    nope_part = wq[:, :, MLA_ROPE:]
    wq_p = jnp.concatenate([rope_part.reshape(MLA_Q_LORA, -1), nope_part.reshape(MLA_Q_LORA, -1)], axis=1)
    wkv = w_ukv.reshape(MLA_KV_LORA, MLA_HEADS, MLA_NOPE + MLA_V)
    wkv_p = jnp.concatenate([wkv[:, :, :MLA_NOPE].reshape(MLA_KV_LORA, -1),
                             wkv[:, :, MLA_NOPE:].reshape(MLA_KV_LORA, -1)], axis=1)
    return wq_p.astype(BF16), wkv_p.astype(BF16)


def _rope_tables(s, rot_dim):
    half = rot_dim // 2
    inv_freq = ROPE_THETA ** (-jnp.arange(half, dtype=F32) / half)
    ang = jnp.arange(s, dtype=jnp.int32).astype(F32)[:, None] * inv_freq[None, :]
    cos, sin = jnp.cos(ang), jnp.sin(ang)
    cos_g = jnp.concatenate([cos, cos], axis=1)
    sin_g = jnp.concatenate([-sin, sin], axis=1)
    if rot_dim == MLA_ROPE:
        reps = LANES // rot_dim
        return jnp.tile(cos_g, (1, reps)), jnp.tile(sin_g, (1, reps))
    rest = LANES - rot_dim
    return (jnp.concatenate([cos_g, jnp.ones((s, rest), F32)], axis=1),
            jnp.concatenate([sin_g, jnp.zeros((s, rest), F32)], axis=1))


def _layer(x, xb, p, tabs):
    s = x.shape[0]
    mla_cos, mla_sin, moba_cos, moba_sin, onehot = tabs
    w_in3, layer = p["w_in"]
    tm = min(s, 1024)
    h_a = _matmul(xb, p["w_a"], F32, tm, HA_COLS, "in_proj_a")
    h_b = _matmul_wslice(xb, w_in3, layer, W_IN_B0, HB_COLS, F32, tm, 512, "in_proj_b")
    h_c = _matmul_wslice(xb, w_in3, layer, W_IN_C0, HC_COLS, F32, tm, 512, "in_proj_c")

    qt, k, vt = _mla_prep(h_a, p["mla_q_norm"], p["mla_kv_norm"], p["w_uq"], p["w_ukv"],
                          mla_cos, mla_sin, 256)
    ya = _mla_attn(qt, k, vt, 512)

    yb = _ssd(h_a, h_b, p["ssm_conv_w"], p["ssm_conv_b"], p["ssm_dt_bias"], p["ssm_a_log"],
              p["ssm_d"], p["ssm_norm"])

    mq, mk, mvt, km = _moba_prep(h_c, moba_cos, moba_sin)
    nb = s // MOBA_BLOCK
    km = jnp.pad(km.reshape(nb, MOBA_W), ((0, -nb % BF16_ROWS), (0, 0)))
    yc = _moba_attn(mq, mk, mvt, km, onehot, 512)

    merged = _merge(ya, yb, yc, p["w_branch_a"], p["w_branch_b"], p["w_branch_c"], layer, h_c,
                    p["gate_bias"], min(s, 512), 512)
    x1, x1b = _proj_ln(merged, p["w_out"], x, p["ln1_g"], p["ln1_b"], 512)
    a = _ffn_up(x1b, p["ffn_w_up"], layer, p["ffn_conv_w"], p["ffn_conv_b"], tm, 512)
    return _ffn_down(a, p["ffn_w_down"], x1, p["ln2_g"], p["ln2_b"], 512, D_FF // 4)


def kernel(x, w_in, mla_q_norm, mla_w_uq, mla_kv_norm, mla_w_ukv, ssm_conv_w, ssm_conv_b, ssm_dt_bias, ssm_a_log, ssm_d, ssm_norm, w_branch_a, w_branch_b, w_branch_c, gate_bias, w_out, ln1_g, ln1_b, ffn_w_up, ffn_conv_w, ffn_conv_b, ffn_w_down, ln2_g, ln2_b):
    b, s, d = x.shape
    assert b == 1 and d == D_MODEL
    assert s % 1024 == 0 and s // MOBA_BLOCK <= LANES
    depth = w_in.shape[0]

    tabs = _rope_tables(s, MLA_ROPE) + _rope_tables(s, MOBA_ROT)
    blk = jnp.arange(s, dtype=jnp.int32)[:, None] // MOBA_BLOCK
    onehot = (blk == jnp.arange(LANES, dtype=jnp.int32)[None, :]).astype(BF16)
    tabs = tabs + (onehot,)

    xf = x.reshape(s, d)
    xb = xf.astype(BF16)
    for l in range(depth):
        wuq, wukv = _pack_mla_weights(mla_w_uq[l], mla_w_ukv[l])
        p = {
            "w_in": (w_in, l),
            "w_a": _pack_w_a(w_in[l]),
            "mla_q_norm": mla_q_norm[l].reshape(1, -1),
            "mla_kv_norm": mla_kv_norm[l].reshape(1, -1),
            "w_uq": wuq, "w_ukv": wukv,
            "ssm_conv_w": ssm_conv_w[l],
            "ssm_conv_b": ssm_conv_b[l].reshape(1, -1),
            "ssm_dt_bias": _pad_cols(ssm_dt_bias[l].reshape(1, -1), LANES),
            "ssm_a_log": _pad_cols(ssm_a_log[l].reshape(1, -1), LANES),
            "ssm_d": jnp.repeat(ssm_d[l], SSM_HEADDIM).reshape(1, -1),
            "ssm_norm": ssm_norm[l].reshape(1, -1),
            "w_branch_a": w_branch_a, "w_branch_b": w_branch_b, "w_branch_c": w_branch_c,
            "gate_bias": gate_bias[l].reshape(1, -1),
            "w_out": w_out[l].astype(BF16),
            "ln1_g": ln1_g[l].reshape(1, -1), "ln1_b": ln1_b[l].reshape(1, -1),
            "ffn_w_up": ffn_w_up,
            "ffn_conv_w": ffn_conv_w[l],
            "ffn_conv_b": ffn_conv_b[l].reshape(1, -1),
            "ffn_w_down": ffn_w_down[l].astype(BF16),
            "ln2_g": ln2_g[l].reshape(1, -1), "ln2_b": ln2_b[l].reshape(1, -1),
        }
        xf, xb = _layer(xf, xb, p, tabs)
    return xf.reshape(b, s, d)
```

```python
import functools

import jax
import jax.numpy as jnp
from jax import lax
from jax.experimental import pallas as pl
from jax.experimental.pallas import tpu as pltpu

F32 = jnp.float32
BF16 = jnp.bfloat16

D_MODEL = 2048
DEPTH = 2
ROPE_THETA = 500000.0
LN_EPS = 1e-5
RMS_EPS = 1e-6
NEG_BIG = -1e30
LOG2E = 1.4426950408889634

MLA_HEADS = 8
MLA_Q_LORA = 512
MLA_KV_LORA = 256
MLA_NOPE = 128
MLA_ROPE = 64
MLA_V = 128
MLA_DK = MLA_NOPE + MLA_ROPE

SSM_D_INNER = D_MODEL
SSM_HEADDIM = 64
SSM_HEADS = SSM_D_INNER // SSM_HEADDIM
SSM_GROUPS = 4
SSM_STATE = 128
SSM_CONV = 4
SSM_CHUNK = 256
SSM_CONV_DIM = SSM_D_INNER + 2 * SSM_GROUPS * SSM_STATE
SSM_GROUP_W = SSM_D_INNER // SSM_GROUPS
SSM_GROUP_HEADS = SSM_HEADS // SSM_GROUPS

MOBA_HEADS = 8
MOBA_HEAD_DIM = 128
MOBA_ROT = MOBA_HEAD_DIM // 4
MOBA_BLOCK = 256
MOBA_TOPK = 3
MOBA_W = MOBA_HEADS * MOBA_HEAD_DIM

D_FF = 5632
FFN_CONV = 3
N_BRANCH = 3
DEEPNORM_ALPHA = (2 * DEPTH) ** 0.25

LANES = 128
SUBLANES = 8
BF16_ROWS = 16
VMEM_LIMIT = 56 * 1024 * 1024

IN_SIZES = (MLA_Q_LORA, MLA_KV_LORA, MLA_ROPE, SSM_D_INNER, SSM_CONV_DIM, SSM_HEADS,
            MOBA_W, MOBA_W, MOBA_W, N_BRANCH * D_MODEL)
COL_CQ = 0
COL_CKV = COL_CQ + MLA_Q_LORA
COL_KR = COL_CKV + MLA_KV_LORA
COL_DT = COL_KR + LANES
HA_COLS = COL_DT + LANES
COL_Z = 0
COL_XS = COL_Z + SSM_D_INNER
COL_B = COL_XS + SSM_D_INNER
COL_C = COL_B + SSM_GROUPS * SSM_STATE
HB_COLS = COL_C + SSM_GROUPS * SSM_STATE
COL_MOBA = 0
COL_G = COL_MOBA + 3 * MOBA_W
HC_COLS = COL_G + N_BRANCH * D_MODEL


def _params(*sem):
    return pltpu.CompilerParams(dimension_semantics=sem, vmem_limit_bytes=VMEM_LIMIT)


def _sigmoid(x):
    return 1.0 / (1.0 + jnp.exp(-x))


def _silu(x):
    return x * _sigmoid(x)


def _split3(a):
    a1 = a.astype(BF16)
    r1 = a - a1.astype(F32)
    a2 = r1.astype(BF16)
    a3 = (r1 - a2.astype(F32)).astype(BF16)
    return a1, a2, a3


def _dot_nn(a, b):
    return jnp.dot(a, b, preferred_element_type=F32)


def _dot_nt(a, b):
    return lax.dot_general(a, b, (((1,), (1,)), ((), ())), preferred_element_type=F32)


def _layer_norm(y, g, b):
    mu = jnp.mean(y, axis=-1, keepdims=True)
    d = y - mu
    var = jnp.mean(d * d, axis=-1, keepdims=True)
    return d * lax.rsqrt(var + LN_EPS) * g + b


def _mm_kernel(x_ref, w_ref, o_ref):
    o_ref[...] = _dot_nn(x_ref[...], w_ref[...]).astype(o_ref.dtype)


def _matmul(x, w, out_dtype, tm, tn, name):
    m, k = x.shape
    n = w.shape[1]
    return pl.pallas_call(
        _mm_kernel,
        grid=(n // tn, m // tm),
        in_specs=[pl.BlockSpec((tm, k), lambda j, i: (i, 0)),
                  pl.BlockSpec((k, tn), lambda j, i: (0, j))],
        out_specs=pl.BlockSpec((tm, tn), lambda j, i: (i, j)),
        out_shape=jax.ShapeDtypeStruct((m, n), out_dtype),
        compiler_params=_params("parallel", "parallel"),
        name=name,
    )(x, w)


def _mm_wt_kernel(x_ref, *refs):
    wt_refs, o_ref, w_scr = refs[:-2], refs[-2], refs[-1]

    @pl.when(pl.program_id(1) == 0)
    def _():
        r = 0
        for wt_ref in wt_refs:
            w_scr[r:r + wt_ref.shape[0], :] = wt_ref[...].astype(BF16)
            r += wt_ref.shape[0]

    o_ref[...] = _dot_nt(x_ref[...], w_scr[...]).astype(o_ref.dtype)


def _matmul_wt(x, wt3, layer, windows, n_tiles, out_dtype, tm, name):
    m, k = x.shape
    tn = sum(rows for _, rows in windows)

    def wspec(r0, rows):
        assert r0 % SUBLANES == 0 and tn % SUBLANES == 0
        return pl.BlockSpec((None, pl.Element(rows), pl.Element(k)),
                            lambda j, i: (layer, pl.multiple_of(r0 + j * tn, SUBLANES), 0))

    return pl.pallas_call(
        _mm_wt_kernel,
        grid=(n_tiles, m // tm),
        in_specs=[pl.BlockSpec((tm, k), lambda j, i: (i, 0))] + [wspec(*w) for w in windows],
        out_specs=pl.BlockSpec((tm, tn), lambda j, i: (i, j)),
        out_shape=jax.ShapeDtypeStruct((m, n_tiles * tn), out_dtype),
        scratch_shapes=[pltpu.VMEM((tn, k), BF16)],
        compiler_params=_params("arbitrary", "arbitrary"),
        name=name,
    )(x, *([wt3] * len(windows)))


def _rope_swap(xs, half):
    lane = lax.broadcasted_iota(jnp.int32, xs.shape, 1)
    first = (lane & (2 * half - 1)) < half
    return jnp.where(first, pltpu.roll(xs, LANES - half, 1), pltpu.roll(xs, half, 1))


def _mla_prep_kernel(h_ref, qn_ref, kvn_ref, wuq_ref, wukv_ref, cos_ref, sin_ref,
                     qt_ref, k_ref, vt_ref):
    hm = h_ref[...]
    cq = hm[:, COL_CQ:COL_CQ + MLA_Q_LORA]
    ckv = hm[:, COL_CKV:COL_CKV + MLA_KV_LORA]
    kr = hm[:, COL_KR:COL_KR + LANES]
    nq = cq * lax.rsqrt(jnp.mean(cq * cq, axis=-1, keepdims=True) + RMS_EPS) * qn_ref[...]
    nkv = ckv * lax.rsqrt(jnp.mean(ckv * ckv, axis=-1, keepdims=True) + RMS_EPS) * kvn_ref[...]
    qu = _dot_nn(nq.astype(BF16), wuq_ref[...])
    kvu = _dot_nn(nkv.astype(BF16), wukv_ref[...])
    cos = cos_ref[...]
    sin = sin_ref[...]

    def rope(xs):
        return xs * cos + _rope_swap(xs, MLA_ROPE // 2) * sin

    scale = MLA_DK ** -0.5 * LOG2E
    kpe = rope(kr)[:, :MLA_ROPE].astype(BF16)
    nope0 = MLA_HEADS * LANES
    for h in range(MLA_HEADS):
        lo, hi = h * LANES, (h + 1) * LANES
        qr = rope(qu[:, lo:hi])
        qt_ref[h, 0:MLA_NOPE, :] = (qu[:, nope0 + lo:nope0 + hi] * scale).T.astype(BF16)
        qt_ref[h, MLA_NOPE:MLA_DK, :] = (qr * scale).T[:MLA_ROPE, :].astype(BF16)
        k_ref[h, :, 0:MLA_NOPE] = kvu[:, lo:hi].astype(BF16)
        k_ref[h, :, MLA_NOPE:MLA_DK] = kpe
        vt_ref[h] = kvu[:, nope0 + lo:nope0 + hi].T.astype(BF16)


def _mla_prep(h, q_norm, kv_norm, wuq_p, wukv_p, cos_t, sin_t, tm):
    s = h.shape[0]
    hd = MLA_HEADS
    return pl.pallas_call(
        _mla_prep_kernel,
        grid=(s // tm,),
        in_specs=[pl.BlockSpec((tm, HA_COLS), lambda i: (i, 0)),
                  pl.BlockSpec((1, MLA_Q_LORA), lambda i: (0, 0)),
                  pl.BlockSpec((1, MLA_KV_LORA), lambda i: (0, 0)),
                  pl.BlockSpec(wuq_p.shape, lambda i: (0, 0)),
                  pl.BlockSpec(wukv_p.shape, lambda i: (0, 0)),
                  pl.BlockSpec((tm, LANES), lambda i: (i, 0)),
                  pl.BlockSpec((tm, LANES), lambda i: (i, 0))],
        out_specs=[pl.BlockSpec((hd, MLA_DK, tm), lambda i: (0, 0, i)),
                   pl.BlockSpec((hd, tm, MLA_DK), lambda i: (0, i, 0)),
                   pl.BlockSpec((hd, MLA_V, tm), lambda i: (0, 0, i))],
        out_shape=[jax.ShapeDtypeStruct((hd, MLA_DK, s), BF16),
                   jax.ShapeDtypeStruct((hd, s, MLA_DK), BF16),
                   jax.ShapeDtypeStruct((hd, MLA_V, s), BF16)],
        compiler_params=_params("parallel"),
        name="mla_prep",
    )(h, q_norm, kv_norm, wuq_p, wukv_p, cos_t, sin_t)


SOFTMAX_ROWS = 64


def _fold_rows(x, op):
    out = x[0:SUBLANES]
    for r in range(1, x.shape[0] // SUBLANES):
        out = op(out, x[r * SUBLANES:(r + 1) * SUBLANES])
    return out


def _softmax_tile(s_ref, p_ref, m, l, t, causal):
    rows = SOFTMAX_ROWS

    def chunk(c):
        blk = s_ref[c * rows:(c + 1) * rows, :]
        if causal:
            key = c * rows + lax.broadcasted_iota(jnp.int32, (rows, t), 0)
            qry = lax.broadcasted_iota(jnp.int32, (rows, t), 1)
            blk = jnp.where(key <= qry, blk, -jnp.inf)
        return blk

    mx = None
    for c in range(t // rows):
        part = _fold_rows(chunk(c), jnp.maximum)
        mx = part if mx is None else jnp.maximum(mx, part)
    m_new = jnp.maximum(m, jnp.max(mx, axis=0, keepdims=True))
    alpha = jnp.exp2(m - m_new)
    tot = None
    for c in range(t // rows):
        p = jnp.exp2(chunk(c) - m_new)
        part = _fold_rows(p, jnp.add)
        tot = part if tot is None else tot + part
        p_ref[c * rows:(c + 1) * rows, :] = p.astype(BF16)
    l_new = alpha * l + jnp.sum(tot, axis=0, keepdims=True)
    return alpha, m_new, l_new


FLASH_HEADS = 4
RESIDENT = pl.Buffered(1)
FLASH_BUFS = 5


def _causal_flash(streams, i, t):
    def accumulate(st, a_prev, j_prev, p_ref):
        _, values, _, scratch = st
        acc_scr = scratch[4]
        acc_scr[...] = a_prev * acc_scr[...] + _dot_nn(values(jnp.maximum(j_prev, 0)), p_ref[...])

    def step_one(st, j, carry, cur):
        scores, _, _, scratch = st
        s_bufs, p_bufs = scratch[0:2], scratch[2:4]
        a_prev, m, l = carry
        s_bufs[1 - cur][...] = scores(j + 1)
        accumulate(st, a_prev, j - 1, p_bufs[1 - cur])
        return _softmax_tile(s_bufs[cur], p_bufs[cur], m, l, t, causal=False)

    def finish_one(st, carry, cur):
        _, _, write_out, scratch = st
        s_bufs, p_bufs, acc_scr = scratch[0:2], scratch[2:4], scratch[4]
        a_prev, m, l = carry
        accumulate(st, a_prev, i - 1, p_bufs[1 - cur])
        alpha, m, l = _softmax_tile(s_bufs[cur], p_bufs[cur], m, l, t, causal=True)
        accumulate(st, alpha, i, p_bufs[cur])
        write_out(acc_scr[...] / l)

    def step(j, carries, cur):
        return tuple(step_one(st, j, c, cur) for st, c in zip(streams, carries))

    def pair(jj, carries):
        return step(2 * jj + 1, step(2 * jj, carries, 0), 1)

    def finish(carries, cur):
        for st, c in zip(streams, carries):
            finish_one(st, c, cur)

    for scores, _, _, scratch in streams:
        scratch[0][...] = scores(0)
        scratch[3][...] = jnp.zeros((t, t), BF16)
        scratch[4][...] = jnp.zeros(scratch[4].shape, F32)

    init = (jnp.ones((1, t), F32), jnp.full((1, t), -jnp.inf, F32), jnp.zeros((1, t), F32))
    carries = lax.fori_loop(0, i // 2, pair, (init,) * len(streams))

    @pl.when(i % 2 == 0)
    def _():
        finish(carries, 0)

    @pl.when(i % 2 == 1)
    def _():
        finish(step(i - 1, carries, 0), 1)


def _flash_scratch(t, dv):
    per_head = [pltpu.VMEM((t, t), F32), pltpu.VMEM((t, t), F32),
                pltpu.VMEM((t, t), BF16), pltpu.VMEM((t, t), BF16), pltpu.VMEM((dv, t), F32)]
    assert len(per_head) == FLASH_BUFS
    return per_head * FLASH_HEADS


def _mla_attn_kernel(qt_ref, k_ref, vt_ref, o_ref, *scratch, t):
    i = pl.program_id(1)

    def stream(a):
        qt = qt_ref[a]

        def scores(j):
            return _dot_nn(k_ref[a, pl.ds(pl.multiple_of(j * t, t), t), :], qt)

        def values(j):
            return vt_ref[a, :, pl.ds(pl.multiple_of(j * t, t), t)]

        def write_out(out):
            o_ref[:, a * MLA_V:(a + 1) * MLA_V] = out.T.astype(o_ref.dtype)

        return scores, values, write_out, scratch[a * FLASH_BUFS:(a + 1) * FLASH_BUFS]

    _causal_flash([stream(a) for a in range(FLASH_HEADS)], i, t)


def _mla_attn(qt, k, vt, t):
    hd, _, s = qt.shape
    nh = FLASH_HEADS
    return pl.pallas_call(
        functools.partial(_mla_attn_kernel, t=t),
        grid=(hd // nh, s // t),
        in_specs=[pl.BlockSpec((nh, MLA_DK, t), lambda h, i: (h, 0, i)),
                  pl.BlockSpec((nh, s, MLA_DK), lambda h, i: (h, 0, 0), pipeline_mode=RESIDENT),
                  pl.BlockSpec((nh, MLA_V, s), lambda h, i: (h, 0, 0), pipeline_mode=RESIDENT)],
        out_specs=pl.BlockSpec((t, nh * MLA_V), lambda h, i: (i, h)),
        out_shape=jax.ShapeDtypeStruct((s, hd * MLA_V), BF16),
        scratch_shapes=_flash_scratch(t, MLA_V),
        compiler_params=_params("parallel", "parallel"),
        name="mla_attn",
    )(qt, k, vt)


def _moba_prep_kernel(h_ref, cos_ref, sin_ref, q_ref, k_ref, vt_ref, km_ref):
    cos = cos_ref[...]
    sin = sin_ref[...]

    def rope(xs):
        return xs * cos + _rope_swap(xs, MOBA_ROT // 2) * sin

    for h in range(MOBA_HEADS):
        lo, hi = h * LANES, (h + 1) * LANES
        q_ref[:, lo:hi] = rope(h_ref[:, lo:hi])
        kr = rope(h_ref[:, MOBA_W + lo:MOBA_W + hi])
        k_ref[:, lo:hi] = kr.astype(BF16)
        km_ref[0, :, lo:hi] = jnp.mean(kr, axis=0, keepdims=True)
        vt_ref[lo:hi, :] = h_ref[:, 2 * MOBA_W + lo:2 * MOBA_W + hi].T.astype(BF16)


def _moba_prep(h, cos_t, sin_t):
    s = h.shape[0]
    t = MOBA_BLOCK
    nb = s // t
    return pl.pallas_call(
        _moba_prep_kernel,
        grid=(nb,),
        in_specs=[pl.BlockSpec((t, 3 * MOBA_W), lambda i: (i, COL_MOBA // (3 * MOBA_W))),
                  pl.BlockSpec((t, LANES), lambda i: (i, 0)),
                  pl.BlockSpec((t, LANES), lambda i: (i, 0))],
        out_specs=[pl.BlockSpec((t, MOBA_W), lambda i: (i, 0)),
                   pl.BlockSpec((t, MOBA_W), lambda i: (i, 0)),
                   pl.BlockSpec((MOBA_W, t), lambda i: (0, i)),
                   pl.BlockSpec((1, 1, MOBA_W), lambda i: (i, 0, 0))],
        out_shape=[jax.ShapeDtypeStruct((s, MOBA_W), F32),
                   jax.ShapeDtypeStruct((s, MOBA_W), BF16),
                   jax.ShapeDtypeStruct((MOBA_W, s), BF16),
                   jax.ShapeDtypeStruct((nb, 1, MOBA_W), F32)],
        compiler_params=_params("parallel"),
        name="moba_prep",
    )(h, cos_t, sin_t)


def _moba_attn_kernel(q_ref, k_ref, vt_ref, km_ref, oh_ref, o_ref, *scratch, t):
    i = pl.program_id(1)
    slots = km_ref.shape[0]
    slot = lax.broadcasted_iota(jnp.int32, (slots, t), 0)
    slot_f = slot.astype(F32)
    qpos = i * t + lax.broadcasted_iota(jnp.int32, (1, t), 1)
    own = jnp.right_shift(qpos, MOBA_BLOCK.bit_length() - 1)

    def stream(a):
        cols = slice(a * LANES, (a + 1) * LANES)
        qt = q_ref[:, cols].T
        q1, q2, q3 = _split3(qt)
        k1, k2, k3 = _split3(km_ref[:, cols])
        gate = (_dot_nn(k1, q1) + _dot_nn(k2, q1) + _dot_nn(k1, q2)
                + _dot_nn(k3, q1) + _dot_nn(k2, q2) + _dot_nn(k1, q3))
        g = jnp.where(slot < own, gate, NEG_BIG)
        picked = slot < 0
        for _ in range(MOBA_TOPK):
            mx = jnp.max(g, axis=0, keepdims=True)
            first = jnp.min(jnp.where(g == mx, slot_f, float(slots)), axis=0, keepdims=True)
            hit = slot_f == first
            picked = picked | hit
            g = jnp.where(hit, -jnp.inf, g)
        visible = (picked & (slot < own)) | (slot == own)
        parts = [qt * (MOBA_HEAD_DIM ** -0.5 * LOG2E), jnp.where(visible, 0.0, NEG_BIG)]
        if slots < LANES:
            parts.append(jnp.zeros((LANES - slots, t), F32))
        qxt = jnp.concatenate(parts, axis=0).astype(BF16)

        def scores(j):
            off = pl.multiple_of(j * t, t)
            kx = jnp.concatenate([k_ref[pl.ds(off, t), cols], oh_ref[pl.ds(off, t), :]], axis=1)
            return _dot_nn(kx, qxt)

        def values(j):
            return vt_ref[cols, pl.ds(pl.multiple_of(j * t, t), t)]

        def write_out(out):
            o_ref[:, cols] = out.T.astype(o_ref.dtype)

        return scores, values, write_out, scratch[a * FLASH_BUFS:(a + 1) * FLASH_BUFS]

    _causal_flash([stream(a) for a in range(FLASH_HEADS)], i, t)


def _moba_attn(q, k, vt, kmean, onehot, t):
    s = q.shape[0]
    assert t % MOBA_BLOCK == 0
    w = FLASH_HEADS * LANES
    return pl.pallas_call(
        functools.partial(_moba_attn_kernel, t=t),
        grid=(MOBA_HEADS // FLASH_HEADS, s // t),
        in_specs=[pl.BlockSpec((t, w), lambda h, i: (i, h)),
                  pl.BlockSpec((s, w), lambda h, i: (0, h), pipeline_mode=RESIDENT),
                  pl.BlockSpec((w, s), lambda h, i: (h, 0), pipeline_mode=RESIDENT),
                  pl.BlockSpec((kmean.shape[0], w), lambda h, i: (0, h)),
                  pl.BlockSpec((s, LANES), lambda h, i: (0, 0), pipeline_mode=RESIDENT)],
        out_specs=pl.BlockSpec((t, w), lambda h, i: (i, h)),
        out_shape=jax.ShapeDtypeStruct((s, MOBA_W), BF16),
        scratch_shapes=_flash_scratch(t, MOBA_HEAD_DIM),
        compiler_params=_params("parallel", "parallel"),
        name="moba_attn",
    )(q, k, vt, kmean, onehot)


def _softplus(x):
    return jnp.maximum(x, 0.0) + jnp.log1p(jnp.exp(-jnp.abs(x)))


def _ssd_kernel(xp_ref, x_ref, bp_ref, b_ref, cp_ref, c_ref, dt_ref, z_ref,
                cwx_ref, cbx_ref, cwb_ref, cbb_ref, cwc_ref, cbc_ref,
                dtb_ref, alog_ref, dexp_ref, nw_ref, o_ref,
                xs_scr, bs_scr, cs_scr, st_scr):
    c = pl.program_id(0)
    g = pl.program_id(1)
    L = SSM_CHUNK
    P = SSM_HEADDIM
    halo = SUBLANES

    def conv_silu(prev_ref, cur_ref, scr, w_ref, bias_ref):
        scr[0:halo, :] = jnp.where(c > 0, prev_ref[...], 0.0)
        scr[halo:halo + L, :] = cur_ref[...]
        acc = bias_ref[...]
        for k in range(SSM_CONV):
            o = halo - (SSM_CONV - 1) + k
            acc = acc + w_ref[k:k + 1, :] * scr[o:o + L, :]
        return _silu(acc)

    xs = conv_silu(xp_ref, x_ref, xs_scr, cwx_ref, cbx_ref)
    bm = conv_silu(bp_ref, b_ref, bs_scr, cwb_ref, cbb_ref)
    cm = conv_silu(cp_ref, c_ref, cs_scr, cwc_ref, cbc_ref)

    dtv = _softplus(dt_ref[...] + dtb_ref[...])
    av = dtv * (-jnp.exp(alog_ref[...]))
    row = lax.broadcasted_iota(jnp.int32, (L, L), 0)
    col = lax.broadcasted_iota(jnp.int32, (L, L), 1)
    tril = col <= row
    ones_tril = jnp.where(tril, 1.0, 0.0).astype(BF16)
    a1, a2, a3 = _split3(av)
    acum = _dot_nn(ones_tril, a1) + _dot_nn(ones_tril, a2) + _dot_nn(ones_tril, a3)

    eh = lax.broadcasted_iota(jnp.int32, (LANES, SSM_GROUP_W), 0)
    ej = lax.broadcasted_iota(jnp.int32, (LANES, SSM_GROUP_W), 1)
    expand = jnp.where(eh == g * SSM_GROUP_HEADS + jnp.right_shift(ej, P.bit_length() - 1),
                       1.0, 0.0).astype(BF16)

    def expand3(v):
        v1, v2, v3 = _split3(v)
        return _dot_nn(v1, expand) + _dot_nn(v2, expand) + _dot_nn(v3, expand)

    dt_e = expand3(dtv)
    ac_e = expand3(acum)
    sr = lax.broadcasted_iota(jnp.int32, (BF16_ROWS, LANES), 0)
    sl = lax.broadcasted_iota(jnp.int32, (BF16_ROWS, LANES), 1)
    pick = jnp.where(sl == g * SSM_GROUP_HEADS + sr, 1.0, 0.0).astype(BF16)
    c1, c2, c3 = _split3(acum)
    ac_t = _dot_nt(pick, c1) + _dot_nt(pick, c2) + _dot_nt(pick, c3)

    bb = bm.astype(BF16)
    cb = cm.astype(BF16)
    gmat = _dot_nt(cb, bb)
    xdt = xs * dt_e
    xdt_b = xdt.astype(BF16)
    lane = lax.broadcasted_iota(jnp.int32, (L, LANES), 1)
    lo_half = lane < P
    parts = []
    for pr in range(SSM_GROUP_HEADS // 2):
        xpair = xdt_b[:, pr * LANES:(pr + 1) * LANES]
        acc = None
        for half in range(2):
            r = 2 * pr + half
            seg = ac_e[:, r * P:r * P + 1] - ac_t[r:r + 1, :]
            mh = (gmat * jnp.exp(jnp.where(tril, seg, -jnp.inf))).astype(BF16)
            xh = jnp.where(lo_half if half == 0 else jnp.logical_not(lo_half), xpair,
                           jnp.zeros_like(xpair))
            term = _dot_nn(mh, xh)
            acc = term if acc is None else acc + term
        parts.append(acc)
    y = jnp.concatenate(parts, axis=1)

    last = ac_e[L - 1:L, :]
    xw = (xdt * jnp.exp(last - ac_e)).astype(BF16)
    s_new = _dot_nn(bm.T.astype(BF16), xw)

    @pl.when(c == 0)
    def _():
        st_scr[g] = jnp.zeros((SSM_STATE, SSM_GROUP_W), F32)

    prev = st_scr[g]
    y = y + _dot_nn(cb, prev.astype(BF16)) * jnp.exp(ac_e)
    st_scr[g] = prev * jnp.exp(last) + s_new

    y = y + xs * dexp_ref[...]
    y = y * _silu(z_ref[...])
    y = y * lax.rsqrt(jnp.mean(y * y, axis=-1, keepdims=True) + RMS_EPS) * nw_ref[...]
    o_ref[...] = y.astype(o_ref.dtype)


def _ssd(h_a, h_b, conv_w, conv_b, dt_bias_p, a_log_p, d_exp, norm_w):
    s = h_b.shape[0]
    L = SSM_CHUNK
    nc = s // L
    gw = SSM_GROUP_W
    n = SSM_STATE
    rb = L // SUBLANES

    def prev_map(cblk):
        return lambda c, g: (jnp.maximum(c * rb - 1, 0), cblk + g)

    def cur_map(cblk):
        return lambda c, g: (c, cblk + g)

    xblk, bblk, cblk = COL_XS // gw, COL_B // n, COL_C // n
    wb, wc = SSM_D_INNER // n, (SSM_D_INNER + SSM_GROUPS * n) // n
    in_specs = [
        pl.BlockSpec((SUBLANES, gw), prev_map(xblk)), pl.BlockSpec((L, gw), cur_map(xblk)),
        pl.BlockSpec((SUBLANES, n), prev_map(bblk)), pl.BlockSpec((L, n), cur_map(bblk)),
        pl.BlockSpec((SUBLANES, n), prev_map(cblk)), pl.BlockSpec((L, n), cur_map(cblk)),
        pl.BlockSpec((L, LANES), lambda c, g: (c, COL_DT // LANES)),
        pl.BlockSpec((L, gw), cur_map(COL_Z // gw)),
        pl.BlockSpec((SSM_CONV, gw), lambda c, g: (0, g)), pl.BlockSpec((1, gw), lambda c, g: (0, g)),
        pl.BlockSpec((SSM_CONV, n), lambda c, g: (0, wb + g)), pl.BlockSpec((1, n), lambda c, g: (0, wb + g)),
        pl.BlockSpec((SSM_CONV, n), lambda c, g: (0, wc + g)), pl.BlockSpec((1, n), lambda c, g: (0, wc + g)),
        pl.BlockSpec((1, LANES), lambda c, g: (0, 0)),
        pl.BlockSpec((1, LANES), lambda c, g: (0, 0)),
        pl.BlockSpec((1, gw), lambda c, g: (0, g)),
        pl.BlockSpec((1, gw), lambda c, g: (0, g)),
    ]
    return pl.pallas_call(
        _ssd_kernel,
        grid=(nc, SSM_GROUPS),
        in_specs=in_specs,
        out_specs=pl.BlockSpec((L, gw), lambda c, g: (c, g)),
        out_shape=jax.ShapeDtypeStruct((s, SSM_D_INNER), BF16),
        scratch_shapes=[pltpu.VMEM((L + SUBLANES, gw), F32),
                        pltpu.VMEM((L + SUBLANES, n), F32),
                        pltpu.VMEM((L + SUBLANES, n), F32),
                        pltpu.VMEM((SSM_GROUPS, n, gw), F32)],
        compiler_params=_params("arbitrary", "arbitrary"),
        name="ssd",
    )(h_b, h_b, h_b, h_b, h_b, h_b, h_a, h_b, conv_w, conv_b, conv_w, conv_b, conv_w, conv_b,
      dt_bias_p, a_log_p, d_exp, norm_w)


SUB_ROWS = 256


def _merge_kernel(ya_ref, yb_ref, yc_ref, wa_ref, wb_ref, wc_ref,
                  g0_ref, g1_ref, g2_ref, b0_ref, b1_ref, b2_ref, o_ref,
                  wa_scr, wb_scr, wc_scr):
    @pl.when(pl.program_id(1) == 0)
    def _():
        wa_scr[...] = wa_ref[...].astype(BF16)
        wb_scr[...] = wb_ref[...].astype(BF16)
        wc_scr[...] = wc_ref[...].astype(BF16)

    def branch(rows, y_ref, w_scr, g_ref, b_ref):
        return _sigmoid(g_ref[rows, :] + b_ref[...]) * _dot_nn(y_ref[rows, :], w_scr[...])

    for r in range(o_ref.shape[0] // SUB_ROWS):
        rows = slice(r * SUB_ROWS, (r + 1) * SUB_ROWS)
        out = (branch(rows, ya_ref, wa_scr, g0_ref, b0_ref)
               + branch(rows, yb_ref, wb_scr, g1_ref, b1_ref)
               + branch(rows, yc_ref, wc_scr, g2_ref, b2_ref))
        o_ref[rows, :] = out.astype(o_ref.dtype)


def _merge(ya, yb, yc, wa, wb, wc, layer, h, gate_bias, tm, tn):
    s = ya.shape[0]
    nt = D_MODEL // tn
    gblk = COL_G // tn

    def gmap(b):
        return lambda j, i: (i, gblk + b * nt + j)

    def bmap(b):
        return lambda j, i: (0, b * nt + j)

    def wspec(w):
        return pl.BlockSpec((None, w.shape[1], tn), lambda j, i: (layer, 0, j))

    return pl.pallas_call(
        _merge_kernel,
        grid=(nt, s // tm),
        in_specs=[pl.BlockSpec((tm, ya.shape[1]), lambda j, i: (i, 0)),
                  pl.BlockSpec((tm, yb.shape[1]), lambda j, i: (i, 0)),
                  pl.BlockSpec((tm, yc.shape[1]), lambda j, i: (i, 0)),
                  wspec(wa), wspec(wb), wspec(wc),
                  pl.BlockSpec((tm, tn), gmap(0)), pl.BlockSpec((tm, tn), gmap(1)),
                  pl.BlockSpec((tm, tn), gmap(2)),
                  pl.BlockSpec((1, tn), bmap(0)), pl.BlockSpec((1, tn), bmap(1)),
                  pl.BlockSpec((1, tn), bmap(2))],
        out_specs=pl.BlockSpec((tm, tn), lambda j, i: (i, j)),
        out_shape=jax.ShapeDtypeStruct((s, D_MODEL), BF16),
        scratch_shapes=[pltpu.VMEM((w.shape[1], tn), BF16) for w in (wa, wb, wc)],
        compiler_params=_params("arbitrary", "arbitrary"),
        name="branch_merge",
    )(ya, yb, yc, wa, wb, wc, h, h, h, gate_bias, gate_bias, gate_bias)


def _proj_ln_kernel(m_ref, w_ref, x_ref, g_ref, b_ref, of_ref, ob_ref):
    for r in range(of_ref.shape[0] // SUB_ROWS):
        rows = slice(r * SUB_ROWS, (r + 1) * SUB_ROWS)
        y = DEEPNORM_ALPHA * x_ref[rows, :] + _dot_nn(m_ref[rows, :], w_ref[...])
        out = _layer_norm(y, g_ref[...], b_ref[...])
        of_ref[rows, :] = out
        ob_ref[rows, :] = out.astype(BF16)


def _proj_ln(m, w3, layer, x, g, b, tm):
    s, d = x.shape
    w = w3
    return pl.pallas_call(
        _proj_ln_kernel,
        grid=(s // tm,),
        in_specs=[pl.BlockSpec((tm, m.shape[1]), lambda i: (i, 0)),
                  pl.BlockSpec((None,) + w.shape[1:], lambda i: (layer, 0, 0)),
                  pl.BlockSpec((tm, d), lambda i: (i, 0)),
                  pl.BlockSpec((1, d), lambda i: (0, 0)),
                  pl.BlockSpec((1, d), lambda i: (0, 0))],
        out_specs=[pl.BlockSpec((tm, d), lambda i: (i, 0)),
                   pl.BlockSpec((tm, d), lambda i: (i, 0))],
        out_shape=[jax.ShapeDtypeStruct((s, d), F32), jax.ShapeDtypeStruct((s, d), BF16)],
        compiler_params=_params("parallel"),
        name="out_proj_ln",
    )(m, w, x, g, b)


def _ffn_up_kernel(x_ref, wg_ref, wu_ref, cwg_ref, cbg_ref, cwu_ref, cbu_ref, o_ref,
                   wgb_scr, wub_scr, ug_scr, uu_scr, *, tm):
    i = pl.program_id(1)
    hist = SUBLANES

    @pl.when(i == 0)
    def _():
        wgb_scr[...] = wg_ref[...].astype(BF16)
        wub_scr[...] = wu_ref[...].astype(BF16)
        ug_scr[0:hist, :] = jnp.zeros((hist, ug_scr.shape[1]), F32)
        uu_scr[0:hist, :] = jnp.zeros((hist, uu_scr.shape[1]), F32)

    def conv(scr, w_ref, bias_ref, r0):
        acc = bias_ref[...]
        for k in range(FFN_CONV):
            o = hist + r0 - (FFN_CONV - 1) + k
            acc = acc + w_ref[k:k + 1, :] * scr[o:o + SUB_ROWS, :]
        return acc

    for r in range(tm // SUB_ROWS):
        r0 = r * SUB_ROWS
        xs = x_ref[r0:r0 + SUB_ROWS, :]
        ug_scr[hist + r0:hist + r0 + SUB_ROWS, :] = _dot_nn(xs, wgb_scr[...])
        uu_scr[hist + r0:hist + r0 + SUB_ROWS, :] = _dot_nn(xs, wub_scr[...])
        gate = conv(ug_scr, cwg_ref, cbg_ref, r0)
        up = conv(uu_scr, cwu_ref, cbu_ref, r0)
        o_ref[r0:r0 + SUB_ROWS, :] = (_silu(gate) * up).astype(o_ref.dtype)

    ug_scr[0:hist, :] = ug_scr[tm:tm + hist, :]
    uu_scr[0:hist, :] = uu_scr[tm:tm + hist, :]


def _ffn_up(xb, w_up, layer, conv_w, conv_b, tm, tj):
    s, d = xb.shape
    nj = D_FF // tj
    return pl.pallas_call(
        functools.partial(_ffn_up_kernel, tm=tm),
        grid=(nj, s // tm),
        in_specs=[pl.BlockSpec((tm, d), lambda j, i: (i, 0)),
                  pl.BlockSpec((None, d, tj), lambda j, i: (layer, 0, j)),
                  pl.BlockSpec((None, d, tj), lambda j, i: (layer, 0, nj + j)),
                  pl.BlockSpec((FFN_CONV, tj), lambda j, i: (0, j)),
                  pl.BlockSpec((1, tj), lambda j, i: (0, j)),
                  pl.BlockSpec((FFN_CONV, tj), lambda j, i: (0, nj + j)),
                  pl.BlockSpec((1, tj), lambda j, i: (0, nj + j))],
        out_specs=pl.BlockSpec((tm, tj), lambda j, i: (i, j)),
        out_shape=jax.ShapeDtypeStruct((s, D_FF), BF16),
        scratch_shapes=[pltpu.VMEM((d, tj), BF16), pltpu.VMEM((d, tj), BF16),
                        pltpu.VMEM((tm + SUBLANES, tj), F32),
                        pltpu.VMEM((tm + SUBLANES, tj), F32)],
        compiler_params=_params("arbitrary", "arbitrary"),
        name="ffn_up_glu",
    )(xb, w_up, w_up, conv_w, conv_b, conv_w, conv_b)


def _ffn_down_kernel(a_ref, w_ref, x_ref, g_ref, b_ref, of_ref, ob_ref, acc_ref):
    k = pl.program_id(1)

    @pl.when(k == 0)
    def _():
        acc_ref[...] = DEEPNORM_ALPHA * x_ref[...]

    acc_ref[...] += _dot_nn(a_ref[...], w_ref[...])

    @pl.when(k == pl.num_programs(1) - 1)
    def _():
        out = _layer_norm(acc_ref[...], g_ref[...], b_ref[...])
        of_ref[...] = out
        ob_ref[...] = out.astype(BF16)


def _ffn_down(a, w3, layer, x, g, b, tm, tk):
    s, d = x.shape
    w = w3
    return pl.pallas_call(
        _ffn_down_kernel,
        grid=(s // tm, a.shape[1] // tk),
        in_specs=[pl.BlockSpec((tm, tk), lambda i, k: (i, k)),
                  pl.BlockSpec((None, tk, d), lambda i, k: (layer, k, 0)),
                  pl.BlockSpec((tm, d), lambda i, k: (i, 0)),
                  pl.BlockSpec((1, d), lambda i, k: (0, 0)),
                  pl.BlockSpec((1, d), lambda i, k: (0, 0))],
        out_specs=[pl.BlockSpec((tm, d), lambda i, k: (i, 0)),
                   pl.BlockSpec((tm, d), lambda i, k: (i, 0))],
        out_shape=[jax.ShapeDtypeStruct((s, d), F32), jax.ShapeDtypeStruct((s, d), BF16)],
        scratch_shapes=[pltpu.VMEM((tm, d), F32)],
        compiler_params=_params("parallel", "arbitrary"),
        name="ffn_down_ln",
    )(a, w, x, g, b)


def _pad_cols(w, width):
    return jnp.pad(w, ((0, 0), (0, width - w.shape[1])))


IN_OFFS = tuple(sum(IN_SIZES[:n]) for n in range(len(IN_SIZES) + 1))
W_IN_B0 = IN_OFFS[3]
W_IN_DT0 = IN_OFFS[5]
W_IN_C0 = IN_OFFS[6]
assert W_IN_DT0 - W_IN_B0 == HB_COLS and IN_OFFS[-1] - W_IN_C0 == HC_COLS


def _pack_mla_weights(w_uq, w_ukv):
    wq = w_uq.reshape(MLA_Q_LORA, MLA_HEADS, MLA_DK)
    rope_part = jnp.pad(wq[:, :, :MLA_ROPE], ((0, 0), (0, 0), (0, LANES - MLA_ROPE)))
    nope_part = wq[:, :, MLA_ROPE:]
    wq_p = jnp.concatenate([rope_part.reshape(MLA_Q_LORA, -1), nope_part.reshape(MLA_Q_LORA, -1)], axis=1)
    wkv = w_ukv.reshape(MLA_KV_LORA, MLA_HEADS, MLA_NOPE + MLA_V)
    wkv_p = jnp.concatenate([wkv[:, :, :MLA_NOPE].reshape(MLA_KV_LORA, -1),
                             wkv[:, :, MLA_NOPE:].reshape(MLA_KV_LORA, -1)], axis=1)
    return wq_p.astype(BF16), wkv_p.astype(BF16)


def _rope_tables(s, rot_dim):
    half = rot_dim // 2
    inv_freq = ROPE_THETA ** (-jnp.arange(half, dtype=F32) / half)
    ang = jnp.arange(s, dtype=jnp.int32).astype(F32)[:, None] * inv_freq[None, :]
    cos, sin = jnp.cos(ang), jnp.sin(ang)
    cos_g = jnp.concatenate([cos, cos], axis=1)
    sin_g = jnp.concatenate([-sin, sin], axis=1)
    if rot_dim == MLA_ROPE:
        reps = LANES // rot_dim
        return jnp.tile(cos_g, (1, reps)), jnp.tile(sin_g, (1, reps))
    rest = LANES - rot_dim
    return (jnp.concatenate([cos_g, jnp.ones((s, rest), F32)], axis=1),
            jnp.concatenate([sin_g, jnp.zeros((s, rest), F32)], axis=1))


def _layer(x, xb, p, tabs):
    s = x.shape[0]
    mla_cos, mla_sin, moba_cos, moba_sin, onehot = tabs
    w_in_t = p["w_in_t"]
    layer = p["layer"]
    tm = min(s, 1024)
    h_a = _matmul_wt(xb, w_in_t, layer, [(0, COL_DT), (W_IN_DT0, LANES)], 1, F32, tm, "in_proj_a")
    h_b = _matmul_wt(xb, w_in_t, layer, [(W_IN_B0, 1024)], HB_COLS // 1024, F32, tm, "in_proj_b")
    h_c = _matmul_wt(xb, w_in_t, layer, [(W_IN_C0, 1024)], HC_COLS // 1024, F32, tm, "in_proj_c")

    qt, k, vt = _mla_prep(h_a, p["mla_q_norm"], p["mla_kv_norm"], p["w_uq"], p["w_ukv"],
                          mla_cos, mla_sin, 256)
    ya = _mla_attn(qt, k, vt, 512)

    yb = _ssd(h_a, h_b, p["ssm_conv_w"], p["ssm_conv_b"], p["ssm_dt_bias"], p["ssm_a_log"],
              p["ssm_d"], p["ssm_norm"])

    mq, mk, mvt, km = _moba_prep(h_c, moba_cos, moba_sin)
    nb = s // MOBA_BLOCK
    km = jnp.pad(km.reshape(nb, MOBA_W), ((0, -nb % BF16_ROWS), (0, 0)))
    yc = _moba_attn(mq, mk, mvt, km, onehot, 512)

    merged = _merge(ya, yb, yc, p["w_branch_a"], p["w_branch_b"], p["w_branch_c"], layer, h_c,
                    p["gate_bias"], min(s, 512), 512)
    x1, x1b = _proj_ln(merged, p["w_out"], layer, x, p["ln1_g"], p["ln1_b"], 512)
    a = _ffn_up(x1b, p["ffn_w_up"], layer, p["ffn_conv_w"], p["ffn_conv_b"], tm, 512)
    return _ffn_down(a, p["ffn_w_down"], layer, x1, p["ln2_g"], p["ln2_b"], 512, D_FF // 4)


def kernel(x, w_in, mla_q_norm, mla_w_uq, mla_kv_norm, mla_w_ukv, ssm_conv_w, ssm_conv_b, ssm_dt_bias, ssm_a_log, ssm_d, ssm_norm, w_branch_a, w_branch_b, w_branch_c, gate_bias, w_out, ln1_g, ln1_b, ffn_w_up, ffn_conv_w, ffn_conv_b, ffn_w_down, ln2_g, ln2_b):
    b, s, d = x.shape
    assert b == 1 and d == D_MODEL
    assert s % 1024 == 0 and s // MOBA_BLOCK <= LANES
    depth = w_in.shape[0]

    tabs = _rope_tables(s, MLA_ROPE) + _rope_tables(s, MOBA_ROT)
    blk = jnp.arange(s, dtype=jnp.int32)[:, None] // MOBA_BLOCK
    onehot = (blk == jnp.arange(LANES, dtype=jnp.int32)[None, :]).astype(BF16)
    tabs = tabs + (onehot,)

    xf = x.reshape(s, d)
    xb = xf.astype(BF16)
    w_in_t = jnp.transpose(w_in, (0, 2, 1))
    w_out_b = w_out.astype(BF16)
    ffn_w_down_b = ffn_w_down.astype(BF16)
    for l in range(depth):
        wuq, wukv = _pack_mla_weights(mla_w_uq[l], mla_w_ukv[l])
        p = {
            "w_in_t": w_in_t,
            "layer": l,
            "mla_q_norm": mla_q_norm[l].reshape(1, -1),
            "mla_kv_norm": mla_kv_norm[l].reshape(1, -1),
            "w_uq": wuq, "w_ukv": wukv,
            "ssm_conv_w": ssm_conv_w[l],
            "ssm_conv_b": ssm_conv_b[l].reshape(1, -1),
            "ssm_dt_bias": _pad_cols(ssm_dt_bias[l].reshape(1, -1), LANES),
            "ssm_a_log": _pad_cols(ssm_a_log[l].reshape(1, -1), LANES),
            "ssm_d": jnp.repeat(ssm_d[l], SSM_HEADDIM).reshape(1, -1),
            "ssm_norm": ssm_norm[l].reshape(1, -1),
            "w_branch_a": w_branch_a, "w_branch_b": w_branch_b, "w_branch_c": w_branch_c,
            "gate_bias": gate_bias[l].reshape(1, -1),
            "w_out": w_out_b,
            "ln1_g": ln1_g[l].reshape(1, -1), "ln1_b": ln1_b[l].reshape(1, -1),
            "ffn_w_up": ffn_w_up,
            "ffn_conv_w": ffn_conv_w[l],
            "ffn_conv_b": ffn_conv_b[l].reshape(1, -1),
            "ffn_w_down": ffn_w_down_b,
            "ln2_g": ln2_g[l].reshape(1, -1), "ln2_b": ln2_b[l].reshape(1, -1),
        }
        xf, xb = _layer(xf, xb, p, tabs)
    return xf.reshape(b, s, d)
```

```python
import functools

import jax
import jax.numpy as jnp
from jax import lax
from jax.experimental import pallas as pl
from jax.experimental.pallas import tpu as pltpu

F32 = jnp.float32
BF16 = jnp.bfloat16

D_MODEL = 2048
DEPTH = 2
ROPE_THETA = 500000.0
LN_EPS = 1e-5
RMS_EPS = 1e-6
NEG_BIG = -1e30
LOG2E = 1.4426950408889634

MLA_HEADS = 8
MLA_Q_LORA = 512
MLA_KV_LORA = 256
MLA_NOPE = 128
MLA_ROPE = 64
MLA_V = 128
MLA_DK = MLA_NOPE + MLA_ROPE

SSM_D_INNER = D_MODEL
SSM_HEADDIM = 64
SSM_HEADS = SSM_D_INNER // SSM_HEADDIM
SSM_GROUPS = 4
SSM_STATE = 128
SSM_CONV = 4
SSM_CHUNK = 256
SSM_CONV_DIM = SSM_D_INNER + 2 * SSM_GROUPS * SSM_STATE
SSM_GROUP_W = SSM_D_INNER // SSM_GROUPS
SSM_GROUP_HEADS = SSM_HEADS // SSM_GROUPS

MOBA_HEADS = 8
MOBA_HEAD_DIM = 128
MOBA_ROT = MOBA_HEAD_DIM // 4
MOBA_BLOCK = 256
MOBA_TOPK = 3
MOBA_W = MOBA_HEADS * MOBA_HEAD_DIM

D_FF = 5632
FFN_CONV = 3
N_BRANCH = 3
DEEPNORM_ALPHA = (2 * DEPTH) ** 0.25

LANES = 128
SUBLANES = 8
BF16_ROWS = 16
VMEM_LIMIT = 56 * 1024 * 1024

IN_SIZES = (MLA_Q_LORA, MLA_KV_LORA, MLA_ROPE, SSM_D_INNER, SSM_CONV_DIM, SSM_HEADS,
            MOBA_W, MOBA_W, MOBA_W, N_BRANCH * D_MODEL)
COL_CQ = 0
COL_CKV = COL_CQ + MLA_Q_LORA
COL_KR = COL_CKV + MLA_KV_LORA
COL_DT = COL_KR + LANES
HA_COLS = COL_DT + LANES
COL_Z = 0
COL_XS = COL_Z + SSM_D_INNER
COL_B = COL_XS + SSM_D_INNER
COL_C = COL_B + SSM_GROUPS * SSM_STATE
HB_COLS = COL_C + SSM_GROUPS * SSM_STATE
COL_MOBA = 0
HC_COLS = COL_MOBA + 3 * MOBA_W
HG_COLS = N_BRANCH * D_MODEL


def _params(*sem):
    return pltpu.CompilerParams(dimension_semantics=sem, vmem_limit_bytes=VMEM_LIMIT)


def _sigmoid(x):
    return 1.0 / (1.0 + jnp.exp(-x))


def _silu(x):
    return x * _sigmoid(x)


def _split3(a):
    a1 = a.astype(BF16)
    r1 = a - a1.astype(F32)
    a2 = r1.astype(BF16)
    a3 = (r1 - a2.astype(F32)).astype(BF16)
    return a1, a2, a3


def _dot_nn(a, b):
    return jnp.dot(a, b, preferred_element_type=F32)


def _dot_nt(a, b):
    return lax.dot_general(a, b, (((1,), (1,)), ((), ())), preferred_element_type=F32)


def _layer_norm(y, g, b):
    mu = jnp.mean(y, axis=-1, keepdims=True)
    d = y - mu
    var = jnp.mean(d * d, axis=-1, keepdims=True)
    return d * lax.rsqrt(var + LN_EPS) * g + b


def _mm_kernel(x_ref, w_ref, o_ref):
    o_ref[...] = _dot_nn(x_ref[...], w_ref[...]).astype(o_ref.dtype)


def _matmul(x, w, out_dtype, tm, tn, name):
    m, k = x.shape
    n = w.shape[1]
    return pl.pallas_call(
        _mm_kernel,
        grid=(n // tn, m // tm),
        in_specs=[pl.BlockSpec((tm, k), lambda j, i: (i, 0)),
                  pl.BlockSpec((k, tn), lambda j, i: (0, j))],
        out_specs=pl.BlockSpec((tm, tn), lambda j, i: (i, j)),
        out_shape=jax.ShapeDtypeStruct((m, n), out_dtype),
        compiler_params=_params("parallel", "parallel"),
        name=name,
    )(x, w)


def _mm_wt_kernel(x_ref, *refs):
    wt_refs, o_ref, w_scr = refs[:-2], refs[-2], refs[-1]

    @pl.when(pl.program_id(1) == 0)
    def _():
        r = 0
        for wt_ref in wt_refs:
            w_scr[r:r + wt_ref.shape[0], :] = wt_ref[...].astype(BF16)
            r += wt_ref.shape[0]

    o_ref[...] = _dot_nt(x_ref[...], w_scr[...]).astype(o_ref.dtype)


def _matmul_wt(x, wt3, layer, windows, n_tiles, out_dtype, tm, name):
    m, k = x.shape
    tn = sum(rows for _, rows in windows)

    def wspec(r0, rows):
        assert r0 % SUBLANES == 0 and tn % SUBLANES == 0
        return pl.BlockSpec((None, pl.Element(rows), pl.Element(k)),
                            lambda j, i: (layer, pl.multiple_of(r0 + j * tn, SUBLANES), 0))

    return pl.pallas_call(
        _mm_wt_kernel,
        grid=(n_tiles, m // tm),
        in_specs=[pl.BlockSpec((tm, k), lambda j, i: (i, 0))] + [wspec(*w) for w in windows],
        out_specs=pl.BlockSpec((tm, tn), lambda j, i: (i, j)),
        out_shape=jax.ShapeDtypeStruct((m, n_tiles * tn), out_dtype),
        scratch_shapes=[pltpu.VMEM((tn, k), BF16)],
        compiler_params=_params("arbitrary", "arbitrary"),
        name=name,
    )(x, *([wt3] * len(windows)))


def _rope_swap(xs, half):
    lane = lax.broadcasted_iota(jnp.int32, xs.shape, 1)
    first = (lane & (2 * half - 1)) < half
    return jnp.where(first, pltpu.roll(xs, LANES - half, 1), pltpu.roll(xs, half, 1))


def _mla_prep_kernel(h_ref, qn_ref, kvn_ref, wuq_ref, wukv_ref, cos_ref, sin_ref,
                     qt_ref, k_ref, vt_ref):
    hm = h_ref[...]
    cq = hm[:, COL_CQ:COL_CQ + MLA_Q_LORA]
    ckv = hm[:, COL_CKV:COL_CKV + MLA_KV_LORA]
    kr = hm[:, COL_KR:COL_KR + LANES]
    nq = cq * lax.rsqrt(jnp.mean(cq * cq, axis=-1, keepdims=True) + RMS_EPS) * qn_ref[...]
    nkv = ckv * lax.rsqrt(jnp.mean(ckv * ckv, axis=-1, keepdims=True) + RMS_EPS) * kvn_ref[...]
    qu = _dot_nn(nq.astype(BF16), wuq_ref[...])
    kvu = _dot_nn(nkv.astype(BF16), wukv_ref[...])
    cos = cos_ref[...]
    sin = sin_ref[...]

    def rope(xs):
        return xs * cos + _rope_swap(xs, MLA_ROPE // 2) * sin

    scale = MLA_DK ** -0.5 * LOG2E
    kpe = rope(kr)[:, :MLA_ROPE].astype(BF16)
    nope0 = MLA_HEADS * LANES
    for h in range(MLA_HEADS):
        lo, hi = h * LANES, (h + 1) * LANES
        qr = rope(qu[:, lo:hi])
        qt_ref[h, 0:MLA_NOPE, :] = (qu[:, nope0 + lo:nope0 + hi] * scale).T.astype(BF16)
        qt_ref[h, MLA_NOPE:MLA_DK, :] = (qr * scale).T[:MLA_ROPE, :].astype(BF16)
        k_ref[h, :, 0:MLA_NOPE] = kvu[:, lo:hi].astype(BF16)
        k_ref[h, :, MLA_NOPE:MLA_DK] = kpe
        vt_ref[h] = kvu[:, nope0 + lo:nope0 + hi].T.astype(BF16)


def _mla_prep(h, q_norm, kv_norm, wuq_p, wukv_p, cos_t, sin_t, tm):
    s = h.shape[0]
    hd = MLA_HEADS
    return pl.pallas_call(
        _mla_prep_kernel,
        grid=(s // tm,),
        in_specs=[pl.BlockSpec((tm, HA_COLS), lambda i: (i, 0)),
                  pl.BlockSpec((1, MLA_Q_LORA), lambda i: (0, 0)),
                  pl.BlockSpec((1, MLA_KV_LORA), lambda i: (0, 0)),
                  pl.BlockSpec(wuq_p.shape, lambda i: (0, 0)),
                  pl.BlockSpec(wukv_p.shape, lambda i: (0, 0)),
                  pl.BlockSpec((tm, LANES), lambda i: (i, 0)),
                  pl.BlockSpec((tm, LANES), lambda i: (i, 0))],
        out_specs=[pl.BlockSpec((hd, MLA_DK, tm), lambda i: (0, 0, i)),
                   pl.BlockSpec((hd, tm, MLA_DK), lambda i: (0, i, 0)),
                   pl.BlockSpec((hd, MLA_V, tm), lambda i: (0, 0, i))],
        out_shape=[jax.ShapeDtypeStruct((hd, MLA_DK, s), BF16),
                   jax.ShapeDtypeStruct((hd, s, MLA_DK), BF16),
                   jax.ShapeDtypeStruct((hd, MLA_V, s), BF16)],
        compiler_params=_params("parallel"),
        name="mla_prep",
    )(h, q_norm, kv_norm, wuq_p, wukv_p, cos_t, sin_t)


SOFTMAX_ROWS = 64


def _fold_rows(x, op):
    out = x[0:SUBLANES]
    for r in range(1, x.shape[0] // SUBLANES):
        out = op(out, x[r * SUBLANES:(r + 1) * SUBLANES])
    return out


def _softmax_tile(s_ref, p_ref, m, l, t, causal):
    rows = SOFTMAX_ROWS

    def chunk(c):
        blk = s_ref[c * rows:(c + 1) * rows, :]
        if causal:
            key = c * rows + lax.broadcasted_iota(jnp.int32, (rows, t), 0)
            qry = lax.broadcasted_iota(jnp.int32, (rows, t), 1)
            blk = jnp.where(key <= qry, blk, -jnp.inf)
        return blk

    mx = None
    for c in range(t // rows):
        part = _fold_rows(chunk(c), jnp.maximum)
        mx = part if mx is None else jnp.maximum(mx, part)
    m_new = jnp.maximum(m, jnp.max(mx, axis=0, keepdims=True))
    alpha = jnp.exp2(m - m_new)
    tot = None
    for c in range(t // rows):
        p = jnp.exp2(chunk(c) - m_new)
        part = _fold_rows(p, jnp.add)
        tot = part if tot is None else tot + part
        p_ref[c * rows:(c + 1) * rows, :] = p.astype(BF16)
    l_new = alpha * l + jnp.sum(tot, axis=0, keepdims=True)
    return alpha, m_new, l_new


FLASH_HEADS = 4
RESIDENT = pl.Buffered(1)
FLASH_BUFS = 5


def _causal_flash(streams, i, t):
    def accumulate(st, a_prev, j_prev, p_ref):
        _, values, _, scratch = st
        acc_scr = scratch[4]
        acc_scr[...] = a_prev * acc_scr[...] + _dot_nn(values(jnp.maximum(j_prev, 0)), p_ref[...])

    def step_one(st, j, carry, cur):
        scores, _, _, scratch = st
        s_bufs, p_bufs = scratch[0:2], scratch[2:4]
        a_prev, m, l = carry
        s_bufs[1 - cur][...] = scores(j + 1)
        accumulate(st, a_prev, j - 1, p_bufs[1 - cur])
        return _softmax_tile(s_bufs[cur], p_bufs[cur], m, l, t, causal=False)

    def finish_one(st, carry, cur):
        _, _, write_out, scratch = st
        s_bufs, p_bufs, acc_scr = scratch[0:2], scratch[2:4], scratch[4]
        a_prev, m, l = carry
        accumulate(st, a_prev, i - 1, p_bufs[1 - cur])
        alpha, m, l = _softmax_tile(s_bufs[cur], p_bufs[cur], m, l, t, causal=True)
        accumulate(st, alpha, i, p_bufs[cur])
        write_out(acc_scr[...] / l)

    def step(j, carries, cur):
        return tuple(step_one(st, j, c, cur) for st, c in zip(streams, carries))

    def pair(jj, carries):
        return step(2 * jj + 1, step(2 * jj, carries, 0), 1)

    def finish(carries, cur):
        for st, c in zip(streams, carries):
            finish_one(st, c, cur)

    for scores, _, _, scratch in streams:
        scratch[0][...] = scores(0)
        scratch[3][...] = jnp.zeros((t, t), BF16)
        scratch[4][...] = jnp.zeros(scratch[4].shape, F32)

    init = (jnp.ones((1, t), F32), jnp.full((1, t), -jnp.inf, F32), jnp.zeros((1, t), F32))
    carries = lax.fori_loop(0, i // 2, pair, (init,) * len(streams))

    @pl.when(i % 2 == 0)
    def _():
        finish(carries, 0)

    @pl.when(i % 2 == 1)
    def _():
        finish(step(i - 1, carries, 0), 1)


def _flash_scratch(t, dv):
    per_head = [pltpu.VMEM((t, t), F32), pltpu.VMEM((t, t), F32),
                pltpu.VMEM((t, t), BF16), pltpu.VMEM((t, t), BF16), pltpu.VMEM((dv, t), F32)]
    assert len(per_head) == FLASH_BUFS
    return per_head * FLASH_HEADS


def _mla_attn_kernel(qt_ref, k_ref, vt_ref, o_ref, *scratch, t):
    i = pl.program_id(1)

    def stream(a):
        qt = qt_ref[a]

        def scores(j):
            return _dot_nn(k_ref[a, pl.ds(pl.multiple_of(j * t, t), t), :], qt)

        def values(j):
            return vt_ref[a, :, pl.ds(pl.multiple_of(j * t, t), t)]

        def write_out(out):
            o_ref[:, a * MLA_V:(a + 1) * MLA_V] = out.T.astype(o_ref.dtype)

        return scores, values, write_out, scratch[a * FLASH_BUFS:(a + 1) * FLASH_BUFS]

    _causal_flash([stream(a) for a in range(FLASH_HEADS)], i, t)


def _mla_attn(qt, k, vt, t):
    hd, _, s = qt.shape
    nh = FLASH_HEADS
    return pl.pallas_call(
        functools.partial(_mla_attn_kernel, t=t),
        grid=(hd // nh, s // t),
        in_specs=[pl.BlockSpec((nh, MLA_DK, t), lambda h, i: (h, 0, i)),
                  pl.BlockSpec((nh, s, MLA_DK), lambda h, i: (h, 0, 0), pipeline_mode=RESIDENT),
                  pl.BlockSpec((nh, MLA_V, s), lambda h, i: (h, 0, 0), pipeline_mode=RESIDENT)],
        out_specs=pl.BlockSpec((t, nh * MLA_V), lambda h, i: (i, h)),
        out_shape=jax.ShapeDtypeStruct((s, hd * MLA_V), BF16),
        scratch_shapes=_flash_scratch(t, MLA_V),
        compiler_params=_params("parallel", "parallel"),
        name="mla_attn",
    )(qt, k, vt)


def _moba_prep_kernel(h_ref, cos_ref, sin_ref, q_ref, k_ref, vt_ref, km_ref):
    cos = cos_ref[...]
    sin = sin_ref[...]

    def rope(xs):
        return xs * cos + _rope_swap(xs, MOBA_ROT // 2) * sin

    for h in range(MOBA_HEADS):
        lo, hi = h * LANES, (h + 1) * LANES
        q_ref[:, lo:hi] = rope(h_ref[:, lo:hi])
        kr = rope(h_ref[:, MOBA_W + lo:MOBA_W + hi])
        k_ref[:, lo:hi] = kr.astype(BF16)
        km_ref[0, :, lo:hi] = jnp.mean(kr, axis=0, keepdims=True)
        vt_ref[lo:hi, :] = h_ref[:, 2 * MOBA_W + lo:2 * MOBA_W + hi].T.astype(BF16)


def _moba_prep(h, cos_t, sin_t):
    s = h.shape[0]
    t = MOBA_BLOCK
    nb = s // t
    return pl.pallas_call(
        _moba_prep_kernel,
        grid=(nb,),
        in_specs=[pl.BlockSpec((t, 3 * MOBA_W), lambda i: (i, COL_MOBA // (3 * MOBA_W))),
                  pl.BlockSpec((t, LANES), lambda i: (i, 0)),
                  pl.BlockSpec((t, LANES), lambda i: (i, 0))],
        out_specs=[pl.BlockSpec((t, MOBA_W), lambda i: (i, 0)),
                   pl.BlockSpec((t, MOBA_W), lambda i: (i, 0)),
                   pl.BlockSpec((MOBA_W, t), lambda i: (0, i)),
                   pl.BlockSpec((1, 1, MOBA_W), lambda i: (i, 0, 0))],
        out_shape=[jax.ShapeDtypeStruct((s, MOBA_W), F32),
                   jax.ShapeDtypeStruct((s, MOBA_W), BF16),
                   jax.ShapeDtypeStruct((MOBA_W, s), BF16),
                   jax.ShapeDtypeStruct((nb, 1, MOBA_W), F32)],
        compiler_params=_params("parallel"),
        name="moba_prep",
    )(h, cos_t, sin_t)


def _moba_attn_kernel(q_ref, k_ref, vt_ref, km_ref, oh_ref, o_ref, *scratch, t):
    i = pl.program_id(1)
    slots = km_ref.shape[0]
    slot = lax.broadcasted_iota(jnp.int32, (slots, t), 0)
    slot_f = slot.astype(F32)
    qpos = i * t + lax.broadcasted_iota(jnp.int32, (1, t), 1)
    own = jnp.right_shift(qpos, MOBA_BLOCK.bit_length() - 1)

    def stream(a):
        cols = slice(a * LANES, (a + 1) * LANES)
        qt = q_ref[:, cols].T
        q1, q2, q3 = _split3(qt)
        k1, k2, k3 = _split3(km_ref[:, cols])
        gate = (_dot_nn(k1, q1) + _dot_nn(k2, q1) + _dot_nn(k1, q2)
                + _dot_nn(k3, q1) + _dot_nn(k2, q2) + _dot_nn(k1, q3))
        g = jnp.where(slot < own, gate, NEG_BIG)
        picked = slot < 0
        for _ in range(MOBA_TOPK):
            mx = jnp.max(g, axis=0, keepdims=True)
            first = jnp.min(jnp.where(g == mx, slot_f, float(slots)), axis=0, keepdims=True)
            hit = slot_f == first
            picked = picked | hit
            g = jnp.where(hit, -jnp.inf, g)
        visible = (picked & (slot < own)) | (slot == own)
        parts = [qt * (MOBA_HEAD_DIM ** -0.5 * LOG2E), jnp.where(visible, 0.0, NEG_BIG)]
        if slots < LANES:
            parts.append(jnp.zeros((LANES - slots, t), F32))
        qxt = jnp.concatenate(parts, axis=0).astype(BF16)

        def scores(j):
            off = pl.multiple_of(j * t, t)
            kx = jnp.concatenate([k_ref[pl.ds(off, t), cols], oh_ref[pl.ds(off, t), :]], axis=1)
            return _dot_nn(kx, qxt)

        def values(j):
            return vt_ref[cols, pl.ds(pl.multiple_of(j * t, t), t)]

        def write_out(out):
            o_ref[:, cols] = out.T.astype(o_ref.dtype)

        return scores, values, write_out, scratch[a * FLASH_BUFS:(a + 1) * FLASH_BUFS]

    _causal_flash([stream(a) for a in range(FLASH_HEADS)], i, t)


def _moba_attn(q, k, vt, kmean, onehot, t):
    s = q.shape[0]
    assert t % MOBA_BLOCK == 0
    w = FLASH_HEADS * LANES
    return pl.pallas_call(
        functools.partial(_moba_attn_kernel, t=t),
        grid=(MOBA_HEADS // FLASH_HEADS, s // t),
        in_specs=[pl.BlockSpec((t, w), lambda h, i: (i, h)),
                  pl.BlockSpec((s, w), lambda h, i: (0, h), pipeline_mode=RESIDENT),
                  pl.BlockSpec((w, s), lambda h, i: (h, 0), pipeline_mode=RESIDENT),
                  pl.BlockSpec((kmean.shape[0], w), lambda h, i: (0, h)),
                  pl.BlockSpec((s, LANES), lambda h, i: (0, 0), pipeline_mode=RESIDENT)],
        out_specs=pl.BlockSpec((t, w), lambda h, i: (i, h)),
        out_shape=jax.ShapeDtypeStruct((s, MOBA_W), BF16),
        scratch_shapes=_flash_scratch(t, MOBA_HEAD_DIM),
        compiler_params=_params("parallel", "parallel"),
        name="moba_attn",
    )(q, k, vt, kmean, onehot)


def _softplus(x):
    return jnp.maximum(x, 0.0) + jnp.log1p(jnp.exp(-jnp.abs(x)))


def _ssd_kernel(xp_ref, x_ref, bp_ref, b_ref, cp_ref, c_ref, dt_ref, z_ref,
                cwx_ref, cbx_ref, cwb_ref, cbb_ref, cwc_ref, cbc_ref,
                dtb_ref, alog_ref, dexp_ref, nw_ref, o_ref,
                xs_scr, bs_scr, cs_scr, st_scr):
    c = pl.program_id(0)
    g = pl.program_id(1)
    L = SSM_CHUNK
    P = SSM_HEADDIM
    halo = SUBLANES

    def conv_silu(prev_ref, cur_ref, scr, w_ref, bias_ref):
        scr[0:halo, :] = jnp.where(c > 0, prev_ref[...], 0.0)
        scr[halo:halo + L, :] = cur_ref[...]
        acc = bias_ref[...]
        for k in range(SSM_CONV):
            o = halo - (SSM_CONV - 1) + k
            acc = acc + w_ref[k:k + 1, :] * scr[o:o + L, :]
        return _silu(acc)

    xs = conv_silu(xp_ref, x_ref, xs_scr, cwx_ref, cbx_ref)
    bm = conv_silu(bp_ref, b_ref, bs_scr, cwb_ref, cbb_ref)
    cm = conv_silu(cp_ref, c_ref, cs_scr, cwc_ref, cbc_ref)

    dtv = _softplus(dt_ref[...] + dtb_ref[...])
    av = dtv * (-jnp.exp(alog_ref[...]))
    row = lax.broadcasted_iota(jnp.int32, (L, L), 0)
    col = lax.broadcasted_iota(jnp.int32, (L, L), 1)
    tril = col <= row
    ones_tril = jnp.where(tril, 1.0, 0.0).astype(BF16)
    a1, a2, a3 = _split3(av)
    acum = _dot_nn(ones_tril, a1) + _dot_nn(ones_tril, a2) + _dot_nn(ones_tril, a3)

    eh = lax.broadcasted_iota(jnp.int32, (LANES, SSM_GROUP_W), 0)
    ej = lax.broadcasted_iota(jnp.int32, (LANES, SSM_GROUP_W), 1)
    expand = jnp.where(eh == g * SSM_GROUP_HEADS + jnp.right_shift(ej, P.bit_length() - 1),
                       1.0, 0.0).astype(BF16)

    def expand3(v):
        v1, v2, v3 = _split3(v)
        return _dot_nn(v1, expand) + _dot_nn(v2, expand) + _dot_nn(v3, expand)

    dt_e = expand3(dtv)
    ac_e = expand3(acum)
    sr = lax.broadcasted_iota(jnp.int32, (BF16_ROWS, LANES), 0)
    sl = lax.broadcasted_iota(jnp.int32, (BF16_ROWS, LANES), 1)
    pick = jnp.where(sl == g * SSM_GROUP_HEADS + sr, 1.0, 0.0).astype(BF16)
    c1, c2, c3 = _split3(acum)
    ac_t = _dot_nt(pick, c1) + _dot_nt(pick, c2) + _dot_nt(pick, c3)

    bb = bm.astype(BF16)
    cb = cm.astype(BF16)
    gmat = _dot_nt(cb, bb)
    xdt = xs * dt_e
    xdt_b = xdt.astype(BF16)
    lane = lax.broadcasted_iota(jnp.int32, (L, LANES), 1)
    lo_half = lane < P
    parts = []
    for pr in range(SSM_GROUP_HEADS // 2):
        xpair = xdt_b[:, pr * LANES:(pr + 1) * LANES]
        acc = None
        for half in range(2):
            r = 2 * pr + half
            seg = ac_e[:, r * P:r * P + 1] - ac_t[r:r + 1, :]
            mh = (gmat * jnp.exp(jnp.where(tril, seg, -jnp.inf))).astype(BF16)
            xh = jnp.where(lo_half if half == 0 else jnp.logical_not(lo_half), xpair,
                           jnp.zeros_like(xpair))
            term = _dot_nn(mh, xh)
            acc = term if acc is None else acc + term
        parts.append(acc)
    y = jnp.concatenate(parts, axis=1)

    last = ac_e[L - 1:L, :]
    xw = (xdt * jnp.exp(last - ac_e)).astype(BF16)
    s_new = _dot_nn(bm.T.astype(BF16), xw)

    @pl.when(c == 0)
    def _():
        st_scr[g] = jnp.zeros((SSM_STATE, SSM_GROUP_W), F32)

    prev = st_scr[g]
    y = y + _dot_nn(cb, prev.astype(BF16)) * jnp.exp(ac_e)
    st_scr[g] = prev * jnp.exp(last) + s_new

    y = y + xs * dexp_ref[...]
    y = y * _silu(z_ref[...])
    y = y * lax.rsqrt(jnp.mean(y * y, axis=-1, keepdims=True) + RMS_EPS) * nw_ref[...]
    o_ref[...] = y.astype(o_ref.dtype)


def _ssd(h_a, h_b, conv_w, conv_b, dt_bias_p, a_log_p, d_exp, norm_w):
    s = h_b.shape[0]
    L = SSM_CHUNK
    nc = s // L
    gw = SSM_GROUP_W
    n = SSM_STATE
    rb = L // SUBLANES

    def prev_map(cblk):
        return lambda c, g: (jnp.maximum(c * rb - 1, 0), cblk + g)

    def cur_map(cblk):
        return lambda c, g: (c, cblk + g)

    xblk, bblk, cblk = COL_XS // gw, COL_B // n, COL_C // n
    wb, wc = SSM_D_INNER // n, (SSM_D_INNER + SSM_GROUPS * n) // n
    in_specs = [
        pl.BlockSpec((SUBLANES, gw), prev_map(xblk)), pl.BlockSpec((L, gw), cur_map(xblk)),
        pl.BlockSpec((SUBLANES, n), prev_map(bblk)), pl.BlockSpec((L, n), cur_map(bblk)),
        pl.BlockSpec((SUBLANES, n), prev_map(cblk)), pl.BlockSpec((L, n), cur_map(cblk)),
        pl.BlockSpec((L, LANES), lambda c, g: (c, COL_DT // LANES)),
        pl.BlockSpec((L, gw), cur_map(COL_Z // gw)),
        pl.BlockSpec((SSM_CONV, gw), lambda c, g: (0, g)), pl.BlockSpec((1, gw), lambda c, g: (0, g)),
        pl.BlockSpec((SSM_CONV, n), lambda c, g: (0, wb + g)), pl.BlockSpec((1, n), lambda c, g: (0, wb + g)),
        pl.BlockSpec((SSM_CONV, n), lambda c, g: (0, wc + g)), pl.BlockSpec((1, n), lambda c, g: (0, wc + g)),
        pl.BlockSpec((1, LANES), lambda c, g: (0, 0)),
        pl.BlockSpec((1, LANES), lambda c, g: (0, 0)),
        pl.BlockSpec((1, gw), lambda c, g: (0, g)),
        pl.BlockSpec((1, gw), lambda c, g: (0, g)),
    ]
    return pl.pallas_call(
        _ssd_kernel,
        grid=(nc, SSM_GROUPS),
        in_specs=in_specs,
        out_specs=pl.BlockSpec((L, gw), lambda c, g: (c, g)),
        out_shape=jax.ShapeDtypeStruct((s, SSM_D_INNER), BF16),
        scratch_shapes=[pltpu.VMEM((L + SUBLANES, gw), F32),
                        pltpu.VMEM((L + SUBLANES, n), F32),
                        pltpu.VMEM((L + SUBLANES, n), F32),
                        pltpu.VMEM((SSM_GROUPS, n, gw), F32)],
        compiler_params=_params("arbitrary", "arbitrary"),
        name="ssd",
    )(h_b, h_b, h_b, h_b, h_b, h_b, h_a, h_b, conv_w, conv_b, conv_w, conv_b, conv_w, conv_b,
      dt_bias_p, a_log_p, d_exp, norm_w)


SUB_ROWS = 128


def _merge_kernel(ya_ref, yb_ref, yc_ref, wa_ref, wb_ref, wc_ref,
                  g0_ref, g1_ref, g2_ref, b0_ref, b1_ref, b2_ref, o_ref,
                  wa_scr, wb_scr, wc_scr):
    @pl.when(pl.program_id(1) == 0)
    def _():
        wa_scr[...] = wa_ref[...].astype(BF16)
        wb_scr[...] = wb_ref[...].astype(BF16)
        wc_scr[...] = wc_ref[...].astype(BF16)

    def branch(rows, y_ref, w_scr, g_ref, b_ref):
        gate = _sigmoid(g_ref[rows, :].astype(F32) + b_ref[...])
        return gate * _dot_nn(y_ref[rows, :], w_scr[...])

    for r in range(o_ref.shape[0] // SUB_ROWS):
        rows = slice(r * SUB_ROWS, (r + 1) * SUB_ROWS)
        out = (branch(rows, ya_ref, wa_scr, g0_ref, b0_ref)
               + branch(rows, yb_ref, wb_scr, g1_ref, b1_ref)
               + branch(rows, yc_ref, wc_scr, g2_ref, b2_ref))
        o_ref[rows, :] = out.astype(o_ref.dtype)


def _merge(ya, yb, yc, wa, wb, wc, layer, h, gate_bias, tm, tn):
    s = ya.shape[0]
    nt = D_MODEL // tn
    gblk = 0

    def gmap(b):
        return lambda j, i: (i, gblk + b * nt + j)

    def bmap(b):
        return lambda j, i: (0, b * nt + j)

    def wspec(w):
        return pl.BlockSpec((None, w.shape[1], tn), lambda j, i: (layer, 0, j))

    return pl.pallas_call(
        _merge_kernel,
        grid=(nt, s // tm),
        in_specs=[pl.BlockSpec((tm, ya.shape[1]), lambda j, i: (i, 0)),
                  pl.BlockSpec((tm, yb.shape[1]), lambda j, i: (i, 0)),
                  pl.BlockSpec((tm, yc.shape[1]), lambda j, i: (i, 0)),
                  wspec(wa), wspec(wb), wspec(wc),
                  pl.BlockSpec((tm, tn), gmap(0)), pl.BlockSpec((tm, tn), gmap(1)),
                  pl.BlockSpec((tm, tn), gmap(2)),
                  pl.BlockSpec((1, tn), bmap(0)), pl.BlockSpec((1, tn), bmap(1)),
                  pl.BlockSpec((1, tn), bmap(2))],
        out_specs=pl.BlockSpec((tm, tn), lambda j, i: (i, j)),
        out_shape=jax.ShapeDtypeStruct((s, D_MODEL), BF16),
        scratch_shapes=[pltpu.VMEM((w.shape[1], tn), BF16) for w in (wa, wb, wc)],
        compiler_params=_params("arbitrary", "arbitrary"),
        name="branch_merge",
    )(ya, yb, yc, wa, wb, wc, h, h, h, gate_bias, gate_bias, gate_bias)


def _proj_ln_kernel(m_ref, w_ref, x_ref, g_ref, b_ref, of_ref, ob_ref):
    for r in range(of_ref.shape[0] // SUB_ROWS):
        rows = slice(r * SUB_ROWS, (r + 1) * SUB_ROWS)
        y = DEEPNORM_ALPHA * x_ref[rows, :] + _dot_nn(m_ref[rows, :], w_ref[...])
        out = _layer_norm(y, g_ref[...], b_ref[...])
        of_ref[rows, :] = out
        ob_ref[rows, :] = out.astype(BF16)


def _proj_ln(m, w3, layer, x, g, b, tm):
    s, d = x.shape
    w = w3
    return pl.pallas_call(
        _proj_ln_kernel,
        grid=(s // tm,),
        in_specs=[pl.BlockSpec((tm, m.shape[1]), lambda i: (i, 0)),
                  pl.BlockSpec((None,) + w.shape[1:], lambda i: (layer, 0, 0)),
                  pl.BlockSpec((tm, d), lambda i: (i, 0)),
                  pl.BlockSpec((1, d), lambda i: (0, 0)),
                  pl.BlockSpec((1, d), lambda i: (0, 0))],
        out_specs=[pl.BlockSpec((tm, d), lambda i: (i, 0)),
                   pl.BlockSpec((tm, d), lambda i: (i, 0))],
        out_shape=[jax.ShapeDtypeStruct((s, d), F32), jax.ShapeDtypeStruct((s, d), BF16)],
        compiler_params=_params("parallel"),
        name="out_proj_ln",
    )(m, w, x, g, b)


def _ffn_up_kernel(x_ref, wg_ref, wu_ref, cwg_ref, cbg_ref, cwu_ref, cbu_ref, o_ref,
                   wgb_scr, wub_scr, ug_scr, uu_scr, *, tm):
    i = pl.program_id(1)
    hist = SUBLANES

    @pl.when(i == 0)
    def _():
        wgb_scr[...] = wg_ref[...].astype(BF16)
        wub_scr[...] = wu_ref[...].astype(BF16)
        ug_scr[0:hist, :] = jnp.zeros((hist, ug_scr.shape[1]), F32)
        uu_scr[0:hist, :] = jnp.zeros((hist, uu_scr.shape[1]), F32)

    def conv(scr, w_ref, bias_ref, r0):
        acc = bias_ref[...]
        for k in range(FFN_CONV):
            o = hist + r0 - (FFN_CONV - 1) + k
            acc = acc + w_ref[k:k + 1, :] * scr[o:o + SUB_ROWS, :]
        return acc

    for r0 in range(0, tm, SUB_ROWS):
        xs = x_ref[r0:r0 + SUB_ROWS, :]
        ug_scr[hist + r0:hist + r0 + SUB_ROWS, :] = _dot_nn(xs, wgb_scr[...])
        uu_scr[hist + r0:hist + r0 + SUB_ROWS, :] = _dot_nn(xs, wub_scr[...])
        gate = conv(ug_scr, cwg_ref, cbg_ref, r0)
        up = conv(uu_scr, cwu_ref, cbu_ref, r0)
        o_ref[r0:r0 + SUB_ROWS, :] = (_silu(gate) * up).astype(o_ref.dtype)

    ug_scr[0:hist, :] = ug_scr[tm:tm + hist, :]
    uu_scr[0:hist, :] = uu_scr[tm:tm + hist, :]


def _ffn_up(xb, w_up, layer, conv_w, conv_b, tm, tj):
    s, d = xb.shape
    nj = D_FF // tj
    return pl.pallas_call(
        functools.partial(_ffn_up_kernel, tm=tm),
        grid=(nj, s // tm),
        in_specs=[pl.BlockSpec((tm, d), lambda j, i: (i, 0)),
                  pl.BlockSpec((None, d, tj), lambda j, i: (layer, 0, j)),
                  pl.BlockSpec((None, d, tj), lambda j, i: (layer, 0, nj + j)),
                  pl.BlockSpec((FFN_CONV, tj), lambda j, i: (0, j)),
                  pl.BlockSpec((1, tj), lambda j, i: (0, j)),
                  pl.BlockSpec((FFN_CONV, tj), lambda j, i: (0, nj + j)),
                  pl.BlockSpec((1, tj), lambda j, i: (0, nj + j))],
        out_specs=pl.BlockSpec((tm, tj), lambda j, i: (i, j)),
        out_shape=jax.ShapeDtypeStruct((s, D_FF), BF16),
        scratch_shapes=[pltpu.VMEM((d, tj), BF16), pltpu.VMEM((d, tj), BF16),
                        pltpu.VMEM((tm + SUBLANES, tj), F32),
                        pltpu.VMEM((tm + SUBLANES, tj), F32)],
        compiler_params=_params("arbitrary", "arbitrary"),
        name="ffn_up_glu",
    )(xb, w_up, w_up, conv_w, conv_b, conv_w, conv_b)


def _ffn_down_kernel(a_ref, w_ref, x_ref, g_ref, b_ref, of_ref, ob_ref, acc_ref):
    k = pl.program_id(1)

    @pl.when(k == 0)
    def _():
        acc_ref[...] = DEEPNORM_ALPHA * x_ref[...]

    acc_ref[...] += _dot_nn(a_ref[...], w_ref[...])

    @pl.when(k == pl.num_programs(1) - 1)
    def _():
        out = _layer_norm(acc_ref[...], g_ref[...], b_ref[...])
        of_ref[...] = out
        ob_ref[...] = out.astype(BF16)


def _ffn_down(a, w3, layer, x, g, b, tm, tk):
    s, d = x.shape
    w = w3
    return pl.pallas_call(
        _ffn_down_kernel,
        grid=(s // tm, a.shape[1] // tk),
        in_specs=[pl.BlockSpec((tm, tk), lambda i, k: (i, k)),
                  pl.BlockSpec((None, tk, d), lambda i, k: (layer, k, 0)),
                  pl.BlockSpec((tm, d), lambda i, k: (i, 0)),
                  pl.BlockSpec((1, d), lambda i, k: (0, 0)),
                  pl.BlockSpec((1, d), lambda i, k: (0, 0))],
        out_specs=[pl.BlockSpec((tm, d), lambda i, k: (i, 0)),
                   pl.BlockSpec((tm, d), lambda i, k: (i, 0))],
        out_shape=[jax.ShapeDtypeStruct((s, d), F32), jax.ShapeDtypeStruct((s, d), BF16)],
        scratch_shapes=[pltpu.VMEM((tm, d), F32)],
        compiler_params=_params("parallel", "arbitrary"),
        name="ffn_down_ln",
    )(a, w, x, g, b)


def _pad_cols(w, width):
    return jnp.pad(w, ((0, 0), (0, width - w.shape[1])))


IN_OFFS = tuple(sum(IN_SIZES[:n]) for n in range(len(IN_SIZES) + 1))
W_IN_B0 = IN_OFFS[3]
W_IN_DT0 = IN_OFFS[5]
W_IN_C0 = IN_OFFS[6]
W_IN_G0 = IN_OFFS[9]
assert (W_IN_DT0 - W_IN_B0, W_IN_G0 - W_IN_C0, IN_OFFS[-1] - W_IN_G0) == (HB_COLS, HC_COLS, HG_COLS)


def _pack_mla_weights(w_uq, w_ukv):
    wq = w_uq.reshape(MLA_Q_LORA, MLA_HEADS, MLA_DK)
    rope_part = jnp.pad(wq[:, :, :MLA_ROPE], ((0, 0), (0, 0), (0, LANES - MLA_ROPE)))
    nope_part = wq[:, :, MLA_ROPE:]
    wq_p = jnp.concatenate([rope_part.reshape(MLA_Q_LORA, -1), nope_part.reshape(MLA_Q_LORA, -1)], axis=1)
    wkv = w_ukv.reshape(MLA_KV_LORA, MLA_HEADS, MLA_NOPE + MLA_V)
    wkv_p = jnp.concatenate([wkv[:, :, :MLA_NOPE].reshape(MLA_KV_LORA, -1),
                             wkv[:, :, MLA_NOPE:].reshape(MLA_KV_LORA, -1)], axis=1)
    return wq_p.astype(BF16), wkv_p.astype(BF16)


def _rope_tables(s, rot_dim):
    half = rot_dim // 2
    inv_freq = ROPE_THETA ** (-jnp.arange(half, dtype=F32) / half)
    ang = jnp.arange(s, dtype=jnp.int32).astype(F32)[:, None] * inv_freq[None, :]
    cos, sin = jnp.cos(ang), jnp.sin(ang)
    cos_g = jnp.concatenate([cos, cos], axis=1)
    sin_g = jnp.concatenate([-sin, sin], axis=1)
    if rot_dim == MLA_ROPE:
        reps = LANES // rot_dim
        return jnp.tile(cos_g, (1, reps)), jnp.tile(sin_g, (1, reps))
    rest = LANES - rot_dim
    return (jnp.concatenate([cos_g, jnp.ones((s, rest), F32)], axis=1),
            jnp.concatenate([sin_g, jnp.zeros((s, rest), F32)], axis=1))


def _layer(x, xb, p, tabs):
    s = x.shape[0]
    mla_cos, mla_sin, moba_cos, moba_sin, onehot = tabs
    w_in_t = p["w_in_t"]
    layer = p["layer"]
    tm = min(s, 1024)
    h_a = _matmul_wt(xb, w_in_t, layer, [(0, COL_DT), (W_IN_DT0, LANES)], 1, F32, tm, "in_proj_a")
    h_b = _matmul_wt(xb, w_in_t, layer, [(W_IN_B0, 1024)], HB_COLS // 1024, F32, tm, "in_proj_b")
    h_c = _matmul_wt(xb, w_in_t, layer, [(W_IN_C0, 1024)], HC_COLS // 1024, F32, tm, "in_proj_c")
    h_g = _matmul_wt(xb, w_in_t, layer, [(W_IN_G0, 1024)], HG_COLS // 1024, BF16, tm, "in_proj_g")

    qt, k, vt = _mla_prep(h_a, p["mla_q_norm"], p["mla_kv_norm"], p["w_uq"], p["w_ukv"],
                          mla_cos, mla_sin, 256)
    ya = _mla_attn(qt, k, vt, 512)

    yb = _ssd(h_a, h_b, p["ssm_conv_w"], p["ssm_conv_b"], p["ssm_dt_bias"], p["ssm_a_log"],
              p["ssm_d"], p["ssm_norm"])

    mq, mk, mvt, km = _moba_prep(h_c, moba_cos, moba_sin)
    nb = s // MOBA_BLOCK
    km = jnp.pad(km.reshape(nb, MOBA_W), ((0, -nb % BF16_ROWS), (0, 0)))
    yc = _moba_attn(mq, mk, mvt, km, onehot, 512)

    merged = _merge(ya, yb, yc, p["w_branch_a"], p["w_branch_b"], p["w_branch_c"], layer, h_g,
                    p["gate_bias"], tm, 512)
    x1, x1b = _proj_ln(merged, p["w_out"], layer, x, p["ln1_g"], p["ln1_b"], 512)
    a = _ffn_up(x1b, p["ffn_w_up"], layer, p["ffn_conv_w"], p["ffn_conv_b"], tm, 512)
    return _ffn_down(a, p["ffn_w_down"], layer, x1, p["ln2_g"], p["ln2_b"], 512, D_FF // 4)


def kernel(x, w_in, mla_q_norm, mla_w_uq, mla_kv_norm, mla_w_ukv, ssm_conv_w, ssm_conv_b, ssm_dt_bias, ssm_a_log, ssm_d, ssm_norm, w_branch_a, w_branch_b, w_branch_c, gate_bias, w_out, ln1_g, ln1_b, ffn_w_up, ffn_conv_w, ffn_conv_b, ffn_w_down, ln2_g, ln2_b):
    b, s, d = x.shape
    assert b == 1 and d == D_MODEL
    assert s % 1024 == 0 and s // MOBA_BLOCK <= LANES
    depth = w_in.shape[0]

    tabs = _rope_tables(s, MLA_ROPE) + _rope_tables(s, MOBA_ROT)
    blk = jnp.arange(s, dtype=jnp.int32)[:, None] // MOBA_BLOCK
    onehot = (blk == jnp.arange(LANES, dtype=jnp.int32)[None, :]).astype(BF16)
    tabs = tabs + (onehot,)

    xf = x.reshape(s, d)
    xb = xf.astype(BF16)
    w_in_t = jnp.transpose(w_in, (0, 2, 1))
    w_out_b = w_out.astype(BF16)
    ffn_w_down_b = ffn_w_down.astype(BF16)
    for l in range(depth):
        wuq, wukv = _pack_mla_weights(mla_w_uq[l], mla_w_ukv[l])
        p = {
            "w_in_t": w_in_t,
            "layer": l,
            "mla_q_norm": mla_q_norm[l].reshape(1, -1),
            "mla_kv_norm": mla_kv_norm[l].reshape(1, -1),
            "w_uq": wuq, "w_ukv": wukv,
            "ssm_conv_w": ssm_conv_w[l],
            "ssm_conv_b": ssm_conv_b[l].reshape(1, -1),
            "ssm_dt_bias": _pad_cols(ssm_dt_bias[l].reshape(1, -1), LANES),
            "ssm_a_log": _pad_cols(ssm_a_log[l].reshape(1, -1), LANES),
            "ssm_d": jnp.repeat(ssm_d[l], SSM_HEADDIM).reshape(1, -1),
            "ssm_norm": ssm_norm[l].reshape(1, -1),
            "w_branch_a": w_branch_a, "w_branch_b": w_branch_b, "w_branch_c": w_branch_c,
            "gate_bias": gate_bias[l].reshape(1, -1),
            "w_out": w_out_b,
            "ln1_g": ln1_g[l].reshape(1, -1), "ln1_b": ln1_b[l].reshape(1, -1),
            "ffn_w_up": ffn_w_up,
            "ffn_conv_w": ffn_conv_w[l],
            "ffn_conv_b": ffn_conv_b[l].reshape(1, -1),
            "ffn_w_down": ffn_w_down_b,
            "ln2_g": ln2_g[l].reshape(1, -1), "ln2_b": ln2_b[l].reshape(1, -1),
        }
        xf, xb = _layer(xf, xb, p, tabs)
    return xf.reshape(b, s, d)
```

```python
import functools

import jax
import jax.numpy as jnp
from jax import lax
from jax.experimental import pallas as pl
from jax.experimental.pallas import tpu as pltpu

F32 = jnp.float32
BF16 = jnp.bfloat16

D_MODEL = 2048
DEPTH = 2
ROPE_THETA = 500000.0
LN_EPS = 1e-5
RMS_EPS = 1e-6
NEG_BIG = -1e30
LOG2E = 1.4426950408889634

MLA_HEADS = 8
MLA_Q_LORA = 512
MLA_KV_LORA = 256
MLA_NOPE = 128
MLA_ROPE = 64
MLA_V = 128
MLA_DK = MLA_NOPE + MLA_ROPE

SSM_D_INNER = D_MODEL
SSM_HEADDIM = 64
SSM_HEADS = SSM_D_INNER // SSM_HEADDIM
SSM_GROUPS = 4
SSM_STATE = 128
SSM_CONV = 4
SSM_CHUNK = 256
SSM_CONV_DIM = SSM_D_INNER + 2 * SSM_GROUPS * SSM_STATE
SSM_GROUP_W = SSM_D_INNER // SSM_GROUPS
SSM_GROUP_HEADS = SSM_HEADS // SSM_GROUPS

MOBA_HEADS = 8
MOBA_HEAD_DIM = 128
MOBA_ROT = MOBA_HEAD_DIM // 4
MOBA_BLOCK = 256
MOBA_TOPK = 3
MOBA_W = MOBA_HEADS * MOBA_HEAD_DIM

D_FF = 5632
FFN_CONV = 3
N_BRANCH = 3
DEEPNORM_ALPHA = (2 * DEPTH) ** 0.25

LANES = 128
SUBLANES = 8
BF16_ROWS = 16
VMEM_LIMIT = 56 * 1024 * 1024

IN_SIZES = (MLA_Q_LORA, MLA_KV_LORA, MLA_ROPE, SSM_D_INNER, SSM_CONV_DIM, SSM_HEADS,
            MOBA_W, MOBA_W, MOBA_W, N_BRANCH * D_MODEL)
COL_CQ = 0
COL_CKV = COL_CQ + MLA_Q_LORA
COL_KR = COL_CKV + MLA_KV_LORA
COL_DT = COL_KR + LANES
HA_COLS = COL_DT + LANES
COL_Z = 0
COL_XS = COL_Z + SSM_D_INNER
COL_B = COL_XS + SSM_D_INNER
COL_C = COL_B + SSM_GROUPS * SSM_STATE
HB_COLS = COL_C + SSM_GROUPS * SSM_STATE
COL_MOBA = 0
HC_COLS = COL_MOBA + 3 * MOBA_W
HG_COLS = N_BRANCH * D_MODEL


def _params(*sem):
    return pltpu.CompilerParams(dimension_semantics=sem, vmem_limit_bytes=VMEM_LIMIT)


def _sigmoid(x):
    return 1.0 / (1.0 + jnp.exp(-x))


def _silu(x):
    return x * _sigmoid(x)


def _split3(a):
    a1 = a.astype(BF16)
    r1 = a - a1.astype(F32)
    a2 = r1.astype(BF16)
    a3 = (r1 - a2.astype(F32)).astype(BF16)
    return a1, a2, a3


def _dot_nn(a, b):
    return jnp.dot(a, b, preferred_element_type=F32)


def _dot_nt(a, b):
    return lax.dot_general(a, b, (((1,), (1,)), ((), ())), preferred_element_type=F32)


def _layer_norm(y, g, b):
    mu = jnp.mean(y, axis=-1, keepdims=True)
    d = y - mu
    var = jnp.mean(d * d, axis=-1, keepdims=True)
    return d * lax.rsqrt(var + LN_EPS) * g + b


def _mm_kernel(x_ref, w_ref, o_ref):
    o_ref[...] = _dot_nn(x_ref[...], w_ref[...]).astype(o_ref.dtype)


def _matmul(x, w, out_dtype, tm, tn, name):
    m, k = x.shape
    n = w.shape[1]
    return pl.pallas_call(
        _mm_kernel,
        grid=(n // tn, m // tm),
        in_specs=[pl.BlockSpec((tm, k), lambda j, i: (i, 0)),
                  pl.BlockSpec((k, tn), lambda j, i: (0, j))],
        out_specs=pl.BlockSpec((tm, tn), lambda j, i: (i, j)),
        out_shape=jax.ShapeDtypeStruct((m, n), out_dtype),
        compiler_params=_params("parallel", "parallel"),
        name=name,
    )(x, w)


def _mm_wt_kernel(x_ref, *refs):
    wt_refs, o_ref, w_scr = refs[:-2], refs[-2], refs[-1]

    @pl.when(pl.program_id(1) == 0)
    def _():
        r = 0
        for wt_ref in wt_refs:
            w_scr[r:r + wt_ref.shape[0], :] = wt_ref[...].astype(BF16)
            r += wt_ref.shape[0]

    o_ref[...] = _dot_nt(x_ref[...], w_scr[...]).astype(o_ref.dtype)


def _matmul_wt(x, wt3, layer, windows, n_tiles, out_dtype, tm, name):
    m, k = x.shape
    tn = sum(rows for _, rows in windows)

    def wspec(r0, rows):
        assert r0 % SUBLANES == 0 and tn % SUBLANES == 0
        return pl.BlockSpec((None, pl.Element(rows), pl.Element(k)),
                            lambda j, i: (layer, pl.multiple_of(r0 + j * tn, SUBLANES), 0))

    return pl.pallas_call(
        _mm_wt_kernel,
        grid=(n_tiles, m // tm),
        in_specs=[pl.BlockSpec((tm, k), lambda j, i: (i, 0))] + [wspec(*w) for w in windows],
        out_specs=pl.BlockSpec((tm, tn), lambda j, i: (i, j)),
        out_shape=jax.ShapeDtypeStruct((m, n_tiles * tn), out_dtype),
        scratch_shapes=[pltpu.VMEM((tn, k), BF16)],
        compiler_params=_params("arbitrary", "arbitrary"),
        name=name,
    )(x, *([wt3] * len(windows)))


def _rope_swap(xs, half):
    lane = lax.broadcasted_iota(jnp.int32, xs.shape, 1)
    first = (lane & (2 * half - 1)) < half
    return jnp.where(first, pltpu.roll(xs, LANES - half, 1), pltpu.roll(xs, half, 1))


def _mla_prep_kernel(h_ref, qn_ref, kvn_ref, wuq_ref, wukv_ref, cos_ref, sin_ref,
                     qt_ref, k_ref, vt_ref):
    hm = h_ref[...]
    cq = hm[:, COL_CQ:COL_CQ + MLA_Q_LORA]
    ckv = hm[:, COL_CKV:COL_CKV + MLA_KV_LORA]
    kr = hm[:, COL_KR:COL_KR + LANES]
    nq = cq * lax.rsqrt(jnp.mean(cq * cq, axis=-1, keepdims=True) + RMS_EPS) * qn_ref[...]
    nkv = ckv * lax.rsqrt(jnp.mean(ckv * ckv, axis=-1, keepdims=True) + RMS_EPS) * kvn_ref[...]
    qu = _dot_nn(nq.astype(BF16), wuq_ref[...])
    kvu = _dot_nn(nkv.astype(BF16), wukv_ref[...])
    cos = cos_ref[...]
    sin = sin_ref[...]

    def rope(xs):
        return xs * cos + _rope_swap(xs, MLA_ROPE // 2) * sin

    scale = MLA_DK ** -0.5 * LOG2E
    kpe = rope(kr)[:, :MLA_ROPE].astype(BF16)
    nope0 = MLA_HEADS * LANES
    for h in range(MLA_HEADS):
        lo, hi = h * LANES, (h + 1) * LANES
        qr = rope(qu[:, lo:hi])
        qt_ref[h, 0:MLA_NOPE, :] = (qu[:, nope0 + lo:nope0 + hi] * scale).T.astype(BF16)
        qt_ref[h, MLA_NOPE:MLA_DK, :] = (qr * scale).T[:MLA_ROPE, :].astype(BF16)
        k_ref[h, :, 0:MLA_NOPE] = kvu[:, lo:hi].astype(BF16)
        k_ref[h, :, MLA_NOPE:MLA_DK] = kpe
        vt_ref[h] = kvu[:, nope0 + lo:nope0 + hi].T.astype(BF16)


def _mla_prep(h, q_norm, kv_norm, wuq_p, wukv_p, cos_t, sin_t, tm):
    s = h.shape[0]
    hd = MLA_HEADS
    return pl.pallas_call(
        _mla_prep_kernel,
        grid=(s // tm,),
        in_specs=[pl.BlockSpec((tm, HA_COLS), lambda i: (i, 0)),
                  pl.BlockSpec((1, MLA_Q_LORA), lambda i: (0, 0)),
                  pl.BlockSpec((1, MLA_KV_LORA), lambda i: (0, 0)),
                  pl.BlockSpec(wuq_p.shape, lambda i: (0, 0)),
                  pl.BlockSpec(wukv_p.shape, lambda i: (0, 0)),
                  pl.BlockSpec((tm, LANES), lambda i: (i, 0)),
                  pl.BlockSpec((tm, LANES), lambda i: (i, 0))],
        out_specs=[pl.BlockSpec((hd, MLA_DK, tm), lambda i: (0, 0, i)),
                   pl.BlockSpec((hd, tm, MLA_DK), lambda i: (0, i, 0)),
                   pl.BlockSpec((hd, MLA_V, tm), lambda i: (0, 0, i))],
        out_shape=[jax.ShapeDtypeStruct((hd, MLA_DK, s), BF16),
                   jax.ShapeDtypeStruct((hd, s, MLA_DK), BF16),
                   jax.ShapeDtypeStruct((hd, MLA_V, s), BF16)],
        compiler_params=_params("parallel"),
        name="mla_prep",
    )(h, q_norm, kv_norm, wuq_p, wukv_p, cos_t, sin_t)


SOFTMAX_ROWS = 16


def _fold_rows(x, op):
    out = x[0:SUBLANES]
    for r in range(1, x.shape[0] // SUBLANES):
        out = op(out, x[r * SUBLANES:(r + 1) * SUBLANES])
    return out


def _softmax_tile(s_ref, p_ref, m, l, t, causal):
    rows = SOFTMAX_ROWS

    def chunk(c):
        blk = s_ref[c * rows:(c + 1) * rows, :]
        if causal:
            key = c * rows + lax.broadcasted_iota(jnp.int32, (rows, t), 0)
            qry = lax.broadcasted_iota(jnp.int32, (rows, t), 1)
            blk = jnp.where(key <= qry, blk, -jnp.inf)
        return blk

    mx = None
    for c in range(t // rows):
        part = _fold_rows(chunk(c), jnp.maximum)
        mx = part if mx is None else jnp.maximum(mx, part)
    m_new = jnp.maximum(m, jnp.max(mx, axis=0, keepdims=True))
    alpha = jnp.exp2(m - m_new)
    tot = None
    for c in range(t // rows):
        p = jnp.exp2(chunk(c) - m_new)
        part = _fold_rows(p, jnp.add)
        tot = part if tot is None else tot + part
        p_ref[c * rows:(c + 1) * rows, :] = p.astype(BF16)
    l_new = alpha * l + jnp.sum(tot, axis=0, keepdims=True)
    return alpha, m_new, l_new


FLASH_HEADS = 4
RESIDENT = pl.Buffered(1)
FLASH_BUFS = 5


def _causal_flash(streams, i, t):
    def accumulate(st, a_prev, j_prev, p_ref):
        _, values, _, scratch = st
        acc_scr = scratch[4]
        acc_scr[...] = a_prev * acc_scr[...] + _dot_nn(values(jnp.maximum(j_prev, 0)), p_ref[...])

    def step_one(st, j, carry, cur):
        scores, _, _, scratch = st
        s_bufs, p_bufs = scratch[0:2], scratch[2:4]
        a_prev, m, l = carry
        s_bufs[1 - cur][...] = scores(j + 1)
        accumulate(st, a_prev, j - 1, p_bufs[1 - cur])
        return _softmax_tile(s_bufs[cur], p_bufs[cur], m, l, t, causal=False)

    def finish_one(st, carry, cur):
        _, _, write_out, scratch = st
        s_bufs, p_bufs, acc_scr = scratch[0:2], scratch[2:4], scratch[4]
        a_prev, m, l = carry
        accumulate(st, a_prev, i - 1, p_bufs[1 - cur])
        alpha, m, l = _softmax_tile(s_bufs[cur], p_bufs[cur], m, l, t, causal=True)
        accumulate(st, alpha, i, p_bufs[cur])
        write_out(acc_scr[...] / l)

    def step(j, carries, cur):
        return tuple(step_one(st, j, c, cur) for st, c in zip(streams, carries))

    def pair(jj, carries):
        return step(2 * jj + 1, step(2 * jj, carries, 0), 1)

    def finish(carries, cur):
        for st, c in zip(streams, carries):
            finish_one(st, c, cur)

    for scores, _, _, scratch in streams:
        scratch[0][...] = scores(0)
        scratch[3][...] = jnp.zeros((t, t), BF16)
        scratch[4][...] = jnp.zeros(scratch[4].shape, F32)

    init = (jnp.ones((1, t), F32), jnp.full((1, t), -jnp.inf, F32), jnp.zeros((1, t), F32))
    carries = lax.fori_loop(0, i // 2, pair, (init,) * len(streams))

    @pl.when(i % 2 == 0)
    def _():
        finish(carries, 0)

    @pl.when(i % 2 == 1)
    def _():
        finish(step(i - 1, carries, 0), 1)


def _flash_scratch(t, dv):
    per_head = [pltpu.VMEM((t, t), F32), pltpu.VMEM((t, t), F32),
                pltpu.VMEM((t, t), BF16), pltpu.VMEM((t, t), BF16), pltpu.VMEM((dv, t), F32)]
    assert len(per_head) == FLASH_BUFS
    return per_head * FLASH_HEADS


def _mla_attn_kernel(qt_ref, k_ref, vt_ref, o_ref, *scratch, t):
    i = pl.program_id(1)

    def stream(a):
        qt = qt_ref[a]

        def scores(j):
            return _dot_nn(k_ref[a, pl.ds(pl.multiple_of(j * t, t), t), :], qt)

        def values(j):
            return vt_ref[a, :, pl.ds(pl.multiple_of(j * t, t), t)]

        def write_out(out):
            o_ref[:, a * MLA_V:(a + 1) * MLA_V] = out.T.astype(o_ref.dtype)

        return scores, values, write_out, scratch[a * FLASH_BUFS:(a + 1) * FLASH_BUFS]

    _causal_flash([stream(a) for a in range(FLASH_HEADS)], i, t)


def _mla_attn(qt, k, vt, t):
    hd, _, s = qt.shape
    nh = FLASH_HEADS
    return pl.pallas_call(
        functools.partial(_mla_attn_kernel, t=t),
        grid=(hd // nh, s // t),
        in_specs=[pl.BlockSpec((nh, MLA_DK, t), lambda h, i: (h, 0, i)),
                  pl.BlockSpec((nh, s, MLA_DK), lambda h, i: (h, 0, 0), pipeline_mode=RESIDENT),
                  pl.BlockSpec((nh, MLA_V, s), lambda h, i: (h, 0, 0), pipeline_mode=RESIDENT)],
        out_specs=pl.BlockSpec((t, nh * MLA_V), lambda h, i: (i, h)),
        out_shape=jax.ShapeDtypeStruct((s, hd * MLA_V), BF16),
        scratch_shapes=_flash_scratch(t, MLA_V),
        compiler_params=_params("parallel", "parallel"),
        name="mla_attn",
    )(qt, k, vt)


def _moba_prep_kernel(h_ref, cos_ref, sin_ref, q_ref, k_ref, vt_ref, km_ref):
    cos = cos_ref[...]
    sin = sin_ref[...]

    def rope(xs):
        return xs * cos + _rope_swap(xs, MOBA_ROT // 2) * sin

    for h in range(MOBA_HEADS):
        lo, hi = h * LANES, (h + 1) * LANES
        q_ref[:, lo:hi] = rope(h_ref[:, lo:hi])
        kr = rope(h_ref[:, MOBA_W + lo:MOBA_W + hi])
        k_ref[:, lo:hi] = kr.astype(BF16)
        km_ref[0, :, lo:hi] = jnp.mean(kr, axis=0, keepdims=True)
        vt_ref[lo:hi, :] = h_ref[:, 2 * MOBA_W + lo:2 * MOBA_W + hi].T.astype(BF16)


def _moba_prep(h, cos_t, sin_t):
    s = h.shape[0]
    t = MOBA_BLOCK
    nb = s // t
    return pl.pallas_call(
        _moba_prep_kernel,
        grid=(nb,),
        in_specs=[pl.BlockSpec((t, 3 * MOBA_W), lambda i: (i, COL_MOBA // (3 * MOBA_W))),
                  pl.BlockSpec((t, LANES), lambda i: (i, 0)),
                  pl.BlockSpec((t, LANES), lambda i: (i, 0))],
        out_specs=[pl.BlockSpec((t, MOBA_W), lambda i: (i, 0)),
                   pl.BlockSpec((t, MOBA_W), lambda i: (i, 0)),
                   pl.BlockSpec((MOBA_W, t), lambda i: (0, i)),
                   pl.BlockSpec((1, 1, MOBA_W), lambda i: (i, 0, 0))],
        out_shape=[jax.ShapeDtypeStruct((s, MOBA_W), F32),
                   jax.ShapeDtypeStruct((s, MOBA_W), BF16),
                   jax.ShapeDtypeStruct((MOBA_W, s), BF16),
                   jax.ShapeDtypeStruct((nb, 1, MOBA_W), F32)],
        compiler_params=_params("parallel"),
        name="moba_prep",
    )(h, cos_t, sin_t)


def _moba_attn_kernel(q_ref, k_ref, vt_ref, km_ref, oh_ref, o_ref, *scratch, t):
    i = pl.program_id(1)
    slots = km_ref.shape[0]
    slot = lax.broadcasted_iota(jnp.int32, (slots, t), 0)
    slot_f = slot.astype(F32)
    qpos = i * t + lax.broadcasted_iota(jnp.int32, (1, t), 1)
    own = jnp.right_shift(qpos, MOBA_BLOCK.bit_length() - 1)

    def stream(a):
        cols = slice(a * LANES, (a + 1) * LANES)
        qt = q_ref[:, cols].T
        q1, q2, q3 = _split3(qt)
        k1, k2, k3 = _split3(km_ref[:, cols])
        gate = (_dot_nn(k1, q1) + _dot_nn(k2, q1) + _dot_nn(k1, q2)
                + _dot_nn(k3, q1) + _dot_nn(k2, q2) + _dot_nn(k1, q3))
        g = jnp.where(slot < own, gate, NEG_BIG)
        picked = slot < 0
        for _ in range(MOBA_TOPK):
            mx = jnp.max(g, axis=0, keepdims=True)
            first = jnp.min(jnp.where(g == mx, slot_f, float(slots)), axis=0, keepdims=True)
            hit = slot_f == first
            picked = picked | hit
            g = jnp.where(hit, -jnp.inf, g)
        visible = (picked & (slot < own)) | (slot == own)
        parts = [qt * (MOBA_HEAD_DIM ** -0.5 * LOG2E), jnp.where(visible, 0.0, NEG_BIG)]
        if slots < LANES:
            parts.append(jnp.zeros((LANES - slots, t), F32))
        qxt = jnp.concatenate(parts, axis=0).astype(BF16)

        def scores(j):
            off = pl.multiple_of(j * t, t)
            kx = jnp.concatenate([k_ref[pl.ds(off, t), cols], oh_ref[pl.ds(off, t), :]], axis=1)
            return _dot_nn(kx, qxt)

        def values(j):
            return vt_ref[cols, pl.ds(pl.multiple_of(j * t, t), t)]

        def write_out(out):
            o_ref[:, cols] = out.T.astype(o_ref.dtype)

        return scores, values, write_out, scratch[a * FLASH_BUFS:(a + 1) * FLASH_BUFS]

    _causal_flash([stream(a) for a in range(FLASH_HEADS)], i, t)


def _moba_attn(q, k, vt, kmean, onehot, t):
    s = q.shape[0]
    assert t % MOBA_BLOCK == 0
    w = FLASH_HEADS * LANES
    return pl.pallas_call(
        functools.partial(_moba_attn_kernel, t=t),
        grid=(MOBA_HEADS // FLASH_HEADS, s // t),
        in_specs=[pl.BlockSpec((t, w), lambda h, i: (i, h)),
                  pl.BlockSpec((s, w), lambda h, i: (0, h), pipeline_mode=RESIDENT),
                  pl.BlockSpec((w, s), lambda h, i: (h, 0), pipeline_mode=RESIDENT),
                  pl.BlockSpec((kmean.shape[0], w), lambda h, i: (0, h)),
                  pl.BlockSpec((s, LANES), lambda h, i: (0, 0), pipeline_mode=RESIDENT)],
        out_specs=pl.BlockSpec((t, w), lambda h, i: (i, h)),
        out_shape=jax.ShapeDtypeStruct((s, MOBA_W), BF16),
        scratch_shapes=_flash_scratch(t, MOBA_HEAD_DIM),
        compiler_params=_params("parallel", "parallel"),
        name="moba_attn",
    )(q, k, vt, kmean, onehot)


def _softplus(x):
    return jnp.maximum(x, 0.0) + jnp.log1p(jnp.exp(-jnp.abs(x)))


def _ssd_kernel(xp_ref, x_ref, bp_ref, b_ref, cp_ref, c_ref, dt_ref, z_ref,
                cwx_ref, cbx_ref, cwb_ref, cbb_ref, cwc_ref, cbc_ref,
                dtb_ref, alog_ref, dexp_ref, nw_ref, o_ref,
                xs_scr, bs_scr, cs_scr, st_scr):
    c = pl.program_id(0)
    g = pl.program_id(1)
    L = SSM_CHUNK
    P = SSM_HEADDIM
    halo = SUBLANES

    def conv_silu(prev_ref, cur_ref, scr, w_ref, bias_ref):
        scr[0:halo, :] = jnp.where(c > 0, prev_ref[...], 0.0)
        scr[halo:halo + L, :] = cur_ref[...]
        acc = bias_ref[...]
        for k in range(SSM_CONV):
            o = halo - (SSM_CONV - 1) + k
            acc = acc + w_ref[k:k + 1, :] * scr[o:o + L, :]
        return _silu(acc)

    xs = conv_silu(xp_ref, x_ref, xs_scr, cwx_ref, cbx_ref)
    bm = conv_silu(bp_ref, b_ref, bs_scr, cwb_ref, cbb_ref)
    cm = conv_silu(cp_ref, c_ref, cs_scr, cwc_ref, cbc_ref)

    dtv = _softplus(dt_ref[...] + dtb_ref[...])
    av = dtv * (-jnp.exp(alog_ref[...]))
    row = lax.broadcasted_iota(jnp.int32, (L, L), 0)
    col = lax.broadcasted_iota(jnp.int32, (L, L), 1)
    tril = col <= row
    ones_tril = jnp.where(tril, 1.0, 0.0).astype(BF16)
    a1, a2, a3 = _split3(av)
    acum = _dot_nn(ones_tril, a1) + _dot_nn(ones_tril, a2) + _dot_nn(ones_tril, a3)

    eh = lax.broadcasted_iota(jnp.int32, (LANES, SSM_GROUP_W), 0)
    ej = lax.broadcasted_iota(jnp.int32, (LANES, SSM_GROUP_W), 1)
    expand = jnp.where(eh == g * SSM_GROUP_HEADS + jnp.right_shift(ej, P.bit_length() - 1),
                       1.0, 0.0).astype(BF16)

    def expand3(v):
        v1, v2, v3 = _split3(v)
        return _dot_nn(v1, expand) + _dot_nn(v2, expand) + _dot_nn(v3, expand)

    dt_e = expand3(dtv)
    ac_e = expand3(acum)
    sr = lax.broadcasted_iota(jnp.int32, (BF16_ROWS, LANES), 0)
    sl = lax.broadcasted_iota(jnp.int32, (BF16_ROWS, LANES), 1)
    pick = jnp.where(sl == g * SSM_GROUP_HEADS + sr, 1.0, 0.0).astype(BF16)
    c1, c2, c3 = _split3(acum)
    ac_t = _dot_nt(pick, c1) + _dot_nt(pick, c2) + _dot_nt(pick, c3)

    bb = bm.astype(BF16)
    cb = cm.astype(BF16)
    gmat = _dot_nt(cb, bb)
    xdt = xs * dt_e
    xdt_b = xdt.astype(BF16)
    lane = lax.broadcasted_iota(jnp.int32, (L, LANES), 1)
    lo_half = lane < P
    parts = []
    for pr in range(SSM_GROUP_HEADS // 2):
        xpair = xdt_b[:, pr * LANES:(pr + 1) * LANES]
        acc = None
        for half in range(2):
            r = 2 * pr + half
            seg = ac_e[:, r * P:r * P + 1] - ac_t[r:r + 1, :]
            mh = (gmat * jnp.exp(jnp.where(tril, seg, -jnp.inf))).astype(BF16)
            xh = jnp.where(lo_half if half == 0 else jnp.logical_not(lo_half), xpair,
                           jnp.zeros_like(xpair))
            term = _dot_nn(mh, xh)
            acc = term if acc is None else acc + term
        parts.append(acc)
    y = jnp.concatenate(parts, axis=1)

    last = ac_e[L - 1:L, :]
    xw = (xdt * jnp.exp(last - ac_e)).astype(BF16)
    s_new = _dot_nn(bm.T.astype(BF16), xw)

    @pl.when(c == 0)
    def _():
        st_scr[g] = jnp.zeros((SSM_STATE, SSM_GROUP_W), F32)

    prev = st_scr[g]
    y = y + _dot_nn(cb, prev.astype(BF16)) * jnp.exp(ac_e)
    st_scr[g] = prev * jnp.exp(last) + s_new

    y = y + xs * dexp_ref[...]
    y = y * _silu(z_ref[...])
    y = y * lax.rsqrt(jnp.mean(y * y, axis=-1, keepdims=True) + RMS_EPS) * nw_ref[...]
    o_ref[...] = y.astype(o_ref.dtype)


def _ssd(h_a, h_b, conv_w, conv_b, dt_bias_p, a_log_p, d_exp, norm_w):
    s = h_b.shape[0]
    L = SSM_CHUNK
    nc = s // L
    gw = SSM_GROUP_W
    n = SSM_STATE
    rb = L // SUBLANES

    def prev_map(cblk):
        return lambda c, g: (jnp.maximum(c * rb - 1, 0), cblk + g)

    def cur_map(cblk):
        return lambda c, g: (c, cblk + g)

    xblk, bblk, cblk = COL_XS // gw, COL_B // n, COL_C // n
    wb, wc = SSM_D_INNER // n, (SSM_D_INNER + SSM_GROUPS * n) // n
    in_specs = [
        pl.BlockSpec((SUBLANES, gw), prev_map(xblk)), pl.BlockSpec((L, gw), cur_map(xblk)),
        pl.BlockSpec((SUBLANES, n), prev_map(bblk)), pl.BlockSpec((L, n), cur_map(bblk)),
        pl.BlockSpec((SUBLANES, n), prev_map(cblk)), pl.BlockSpec((L, n), cur_map(cblk)),
        pl.BlockSpec((L, LANES), lambda c, g: (c, COL_DT // LANES)),
        pl.BlockSpec((L, gw), cur_map(COL_Z // gw)),
        pl.BlockSpec((SSM_CONV, gw), lambda c, g: (0, g)), pl.BlockSpec((1, gw), lambda c, g: (0, g)),
        pl.BlockSpec((SSM_CONV, n), lambda c, g: (0, wb + g)), pl.BlockSpec((1, n), lambda c, g: (0, wb + g)),
        pl.BlockSpec((SSM_CONV, n), lambda c, g: (0, wc + g)), pl.BlockSpec((1, n), lambda c, g: (0, wc + g)),
        pl.BlockSpec((1, LANES), lambda c, g: (0, 0)),
        pl.BlockSpec((1, LANES), lambda c, g: (0, 0)),
        pl.BlockSpec((1, gw), lambda c, g: (0, g)),
        pl.BlockSpec((1, gw), lambda c, g: (0, g)),
    ]
    return pl.pallas_call(
        _ssd_kernel,
        grid=(nc, SSM_GROUPS),
        in_specs=in_specs,
        out_specs=pl.BlockSpec((L, gw), lambda c, g: (c, g)),
        out_shape=jax.ShapeDtypeStruct((s, SSM_D_INNER), BF16),
        scratch_shapes=[pltpu.VMEM((L + SUBLANES, gw), F32),
                        pltpu.VMEM((L + SUBLANES, n), F32),
                        pltpu.VMEM((L + SUBLANES, n), F32),
                        pltpu.VMEM((SSM_GROUPS, n, gw), F32)],
        compiler_params=_params("arbitrary", "arbitrary"),
        name="ssd",
    )(h_b, h_b, h_b, h_b, h_b, h_b, h_a, h_b, conv_w, conv_b, conv_w, conv_b, conv_w, conv_b,
      dt_bias_p, a_log_p, d_exp, norm_w)


SUB_ROWS = 256


def _merge_kernel(ya_ref, yb_ref, yc_ref, wa_ref, wb_ref, wc_ref,
                  g0_ref, g1_ref, g2_ref, b0_ref, b1_ref, b2_ref, o_ref,
                  wa_scr, wb_scr, wc_scr):
    @pl.when(pl.program_id(1) == 0)
    def _():
        wa_scr[...] = wa_ref[...].astype(BF16)
        wb_scr[...] = wb_ref[...].astype(BF16)
        wc_scr[...] = wc_ref[...].astype(BF16)

    def branch(rows, y_ref, w_scr, g_ref, b_ref):
        gate = _sigmoid(g_ref[rows, :].astype(F32) + b_ref[...])
        return gate * _dot_nn(y_ref[rows, :], w_scr[...])

    for r in range(o_ref.shape[0] // SUB_ROWS):
        rows = slice(r * SUB_ROWS, (r + 1) * SUB_ROWS)
        out = (branch(rows, ya_ref, wa_scr, g0_ref, b0_ref)
               + branch(rows, yb_ref, wb_scr, g1_ref, b1_ref)
               + branch(rows, yc_ref, wc_scr, g2_ref, b2_ref))
        o_ref[rows, :] = out.astype(o_ref.dtype)


def _merge(ya, yb, yc, wa, wb, wc, layer, h, gate_bias, tm, tn):
    s = ya.shape[0]
    nt = D_MODEL // tn
    gblk = 0

    def gmap(b):
        return lambda j, i: (i, gblk + b * nt + j)

    def bmap(b):
        return lambda j, i: (0, b * nt + j)

    def wspec(w):
        return pl.BlockSpec((None, w.shape[1], tn), lambda j, i: (layer, 0, j))

    return pl.pallas_call(
        _merge_kernel,
        grid=(nt, s // tm),
        in_specs=[pl.BlockSpec((tm, ya.shape[1]), lambda j, i: (i, 0)),
                  pl.BlockSpec((tm, yb.shape[1]), lambda j, i: (i, 0)),
                  pl.BlockSpec((tm, yc.shape[1]), lambda j, i: (i, 0)),
                  wspec(wa), wspec(wb), wspec(wc),
                  pl.BlockSpec((tm, tn), gmap(0)), pl.BlockSpec((tm, tn), gmap(1)),
                  pl.BlockSpec((tm, tn), gmap(2)),
                  pl.BlockSpec((1, tn), bmap(0)), pl.BlockSpec((1, tn), bmap(1)),
                  pl.BlockSpec((1, tn), bmap(2))],
        out_specs=pl.BlockSpec((tm, tn), lambda j, i: (i, j)),
        out_shape=jax.ShapeDtypeStruct((s, D_MODEL), BF16),
        scratch_shapes=[pltpu.VMEM((w.shape[1], tn), BF16) for w in (wa, wb, wc)],
        compiler_params=_params("arbitrary", "arbitrary"),
        name="branch_merge",
    )(ya, yb, yc, wa, wb, wc, h, h, h, gate_bias, gate_bias, gate_bias)


def _proj_ln_kernel(m_ref, w_ref, x_ref, g_ref, b_ref, of_ref, ob_ref):
    for r in range(of_ref.shape[0] // SUB_ROWS):
        rows = slice(r * SUB_ROWS, (r + 1) * SUB_ROWS)
        y = DEEPNORM_ALPHA * x_ref[rows, :] + _dot_nn(m_ref[rows, :], w_ref[...])
        out = _layer_norm(y, g_ref[...], b_ref[...])
        of_ref[rows, :] = out
        ob_ref[rows, :] = out.astype(BF16)


def _proj_ln(m, w3, layer, x, g, b, tm):
    s, d = x.shape
    w = w3
    return pl.pallas_call(
        _proj_ln_kernel,
        grid=(s // tm,),
        in_specs=[pl.BlockSpec((tm, m.shape[1]), lambda i: (i, 0)),
                  pl.BlockSpec((None,) + w.shape[1:], lambda i: (layer, 0, 0)),
                  pl.BlockSpec((tm, d), lambda i: (i, 0)),
                  pl.BlockSpec((1, d), lambda i: (0, 0)),
                  pl.BlockSpec((1, d), lambda i: (0, 0))],
        out_specs=[pl.BlockSpec((tm, d), lambda i: (i, 0)),
                   pl.BlockSpec((tm, d), lambda i: (i, 0))],
        out_shape=[jax.ShapeDtypeStruct((s, d), F32), jax.ShapeDtypeStruct((s, d), BF16)],
        compiler_params=_params("parallel"),
        name="out_proj_ln",
    )(m, w, x, g, b)


def _ffn_up_kernel(x_ref, wg_ref, wu_ref, cwg_ref, cbg_ref, cwu_ref, cbu_ref, o_ref,
                   wgb_scr, wub_scr, ug_scr, uu_scr, *, tm):
    i = pl.program_id(1)
    hist = SUBLANES

    @pl.when(i == 0)
    def _():
        wgb_scr[...] = wg_ref[...].astype(BF16)
        wub_scr[...] = wu_ref[...].astype(BF16)
        ug_scr[0:hist, :] = jnp.zeros((hist, ug_scr.shape[1]), F32)
        uu_scr[0:hist, :] = jnp.zeros((hist, uu_scr.shape[1]), F32)

    def conv(scr, w_ref, bias_ref, r0):
        acc = bias_ref[...]
        for k in range(FFN_CONV):
            o = hist + r0 - (FFN_CONV - 1) + k
            acc = acc + w_ref[k:k + 1, :] * scr[o:o + SUB_ROWS, :]
        return acc

    for r0 in range(0, tm, SUB_ROWS):
        xs = x_ref[r0:r0 + SUB_ROWS, :]
        ug_scr[hist + r0:hist + r0 + SUB_ROWS, :] = _dot_nn(xs, wgb_scr[...])
        uu_scr[hist + r0:hist + r0 + SUB_ROWS, :] = _dot_nn(xs, wub_scr[...])
        gate = conv(ug_scr, cwg_ref, cbg_ref, r0)
        up = conv(uu_scr, cwu_ref, cbu_ref, r0)
        o_ref[r0:r0 + SUB_ROWS, :] = (_silu(gate) * up).astype(o_ref.dtype)

    ug_scr[0:hist, :] = ug_scr[tm:tm + hist, :]
    uu_scr[0:hist, :] = uu_scr[tm:tm + hist, :]


def _ffn_up(xb, w_up, layer, conv_w, conv_b, tm, tj):
    s, d = xb.shape
    nj = D_FF // tj
    return pl.pallas_call(
        functools.partial(_ffn_up_kernel, tm=tm),
        grid=(nj, s // tm),
        in_specs=[pl.BlockSpec((tm, d), lambda j, i: (i, 0)),
                  pl.BlockSpec((None, d, tj), lambda j, i: (layer, 0, j)),
                  pl.BlockSpec((None, d, tj), lambda j, i: (layer, 0, nj + j)),
                  pl.BlockSpec((FFN_CONV, tj), lambda j, i: (0, j)),
                  pl.BlockSpec((1, tj), lambda j, i: (0, j)),
                  pl.BlockSpec((FFN_CONV, tj), lambda j, i: (0, nj + j)),
                  pl.BlockSpec((1, tj), lambda j, i: (0, nj + j))],
        out_specs=pl.BlockSpec((tm, tj), lambda j, i: (i, j)),
        out_shape=jax.ShapeDtypeStruct((s, D_FF), BF16),
        scratch_shapes=[pltpu.VMEM((d, tj), BF16), pltpu.VMEM((d, tj), BF16),
                        pltpu.VMEM((tm + SUBLANES, tj), F32),
                        pltpu.VMEM((tm + SUBLANES, tj), F32)],
        compiler_params=_params("arbitrary", "arbitrary"),
        name="ffn_up_glu",
    )(xb, w_up, w_up, conv_w, conv_b, conv_w, conv_b)


def _ffn_down_kernel(a_ref, w_ref, x_ref, g_ref, b_ref, of_ref, ob_ref, acc_ref):
    k = pl.program_id(1)

    @pl.when(k == 0)
    def _():
        acc_ref[...] = DEEPNORM_ALPHA * x_ref[...]

    acc_ref[...] += _dot_nn(a_ref[...], w_ref[...])

    @pl.when(k == pl.num_programs(1) - 1)
    def _():
        out = _layer_norm(acc_ref[...], g_ref[...], b_ref[...])
        of_ref[...] = out
        ob_ref[...] = out.astype(BF16)


def _ffn_down(a, w3, layer, x, g, b, tm, tk):
    s, d = x.shape
    w = w3
    return pl.pallas_call(
        _ffn_down_kernel,
        grid=(s // tm, a.shape[1] // tk),
        in_specs=[pl.BlockSpec((tm, tk), lambda i, k: (i, k)),
                  pl.BlockSpec((None, tk, d), lambda i, k: (layer, k, 0)),
                  pl.BlockSpec((tm, d), lambda i, k: (i, 0)),
                  pl.BlockSpec((1, d), lambda i, k: (0, 0)),
                  pl.BlockSpec((1, d), lambda i, k: (0, 0))],
        out_specs=[pl.BlockSpec((tm, d), lambda i, k: (i, 0)),
                   pl.BlockSpec((tm, d), lambda i, k: (i, 0))],
        out_shape=[jax.ShapeDtypeStruct((s, d), F32), jax.ShapeDtypeStruct((s, d), BF16)],
        scratch_shapes=[pltpu.VMEM((tm, d), F32)],
        compiler_params=_params("parallel", "arbitrary"),
        name="ffn_down_ln",
    )(a, w, x, g, b)


def _pad_cols(w, width):
    return jnp.pad(w, ((0, 0), (0, width - w.shape[1])))


IN_OFFS = tuple(sum(IN_SIZES[:n]) for n in range(len(IN_SIZES) + 1))
W_IN_B0 = IN_OFFS[3]
W_IN_DT0 = IN_OFFS[5]
W_IN_C0 = IN_OFFS[6]
W_IN_G0 = IN_OFFS[9]
assert (W_IN_DT0 - W_IN_B0, W_IN_G0 - W_IN_C0, IN_OFFS[-1] - W_IN_G0) == (HB_COLS, HC_COLS, HG_COLS)


def _pack_mla_weights(w_uq, w_ukv):
    wq = w_uq.reshape(MLA_Q_LORA, MLA_HEADS, MLA_DK)
    rope_part = jnp.pad(wq[:, :, :MLA_ROPE], ((0, 0), (0, 0), (0, LANES - MLA_ROPE)))
    nope_part = wq[:, :, MLA_ROPE:]
    wq_p = jnp.concatenate([rope_part.reshape(MLA_Q_LORA, -1), nope_part.reshape(MLA_Q_LORA, -1)], axis=1)
    wkv = w_ukv.reshape(MLA_KV_LORA, MLA_HEADS, MLA_NOPE + MLA_V)
    wkv_p = jnp.concatenate([wkv[:, :, :MLA_NOPE].reshape(MLA_KV_LORA, -1),
                             wkv[:, :, MLA_NOPE:].reshape(MLA_KV_LORA, -1)], axis=1)
    return wq_p.astype(BF16), wkv_p.astype(BF16)


def _rope_tables(s, rot_dim):
    half = rot_dim // 2
    inv_freq = ROPE_THETA ** (-jnp.arange(half, dtype=F32) / half)
    ang = jnp.arange(s, dtype=jnp.int32).astype(F32)[:, None] * inv_freq[None, :]
    cos, sin = jnp.cos(ang), jnp.sin(ang)
    cos_g = jnp.concatenate([cos, cos], axis=1)
    sin_g = jnp.concatenate([-sin, sin], axis=1)
    if rot_dim == MLA_ROPE:
        reps = LANES // rot_dim
        return jnp.tile(cos_g, (1, reps)), jnp.tile(sin_g, (1, reps))
    rest = LANES - rot_dim
    return (jnp.concatenate([cos_g, jnp.ones((s, rest), F32)], axis=1),
            jnp.concatenate([sin_g, jnp.zeros((s, rest), F32)], axis=1))


def _layer(x, xb, p, tabs):
    s = x.shape[0]
    mla_cos, mla_sin, moba_cos, moba_sin, onehot = tabs
    w_in_t = p["w_in_t"]
    layer = p["layer"]
    tm = min(s, 1024)
    h_a = _matmul_wt(xb, w_in_t, layer, [(0, COL_DT), (W_IN_DT0, LANES)], 1, F32, tm, "in_proj_a")
    h_b = _matmul_wt(xb, w_in_t, layer, [(W_IN_B0, 1024)], HB_COLS // 1024, F32, tm, "in_proj_b")
    h_c = _matmul_wt(xb, w_in_t, layer, [(W_IN_C0, 1024)], HC_COLS // 1024, F32, tm, "in_proj_c")
    h_g = _matmul_wt(xb, w_in_t, layer, [(W_IN_G0, 1024)], HG_COLS // 1024, BF16, tm, "in_proj_g")

    qt, k, vt = _mla_prep(h_a, p["mla_q_norm"], p["mla_kv_norm"], p["w_uq"], p["w_ukv"],
                          mla_cos, mla_sin, 256)
    ya = _mla_attn(qt, k, vt, 512)

    yb = _ssd(h_a, h_b, p["ssm_conv_w"], p["ssm_conv_b"], p["ssm_dt_bias"], p["ssm_a_log"],
              p["ssm_d"], p["ssm_norm"])

    mq, mk, mvt, km = _moba_prep(h_c, moba_cos, moba_sin)
    nb = s // MOBA_BLOCK
    km = jnp.pad(km.reshape(nb, MOBA_W), ((0, -nb % BF16_ROWS), (0, 0)))
    yc = _moba_attn(mq, mk, mvt, km, onehot, 512)

    merged = _merge(ya, yb, yc, p["w_branch_a"], p["w_branch_b"], p["w_branch_c"], layer, h_g,
                    p["gate_bias"], tm, 512)
    x1, x1b = _proj_ln(merged, p["w_out"], layer, x, p["ln1_g"], p["ln1_b"], 512)
    a = _ffn_up(x1b, p["ffn_w_up"], layer, p["ffn_conv_w"], p["ffn_conv_b"], tm, 512)
    return _ffn_down(a, p["ffn_w_down"], layer, x1, p["ln2_g"], p["ln2_b"], 512, D_FF // 4)


def kernel(x, w_in, mla_q_norm, mla_w_uq, mla_kv_norm, mla_w_ukv, ssm_conv_w, ssm_conv_b, ssm_dt_bias, ssm_a_log, ssm_d, ssm_norm, w_branch_a, w_branch_b, w_branch_c, gate_bias, w_out, ln1_g, ln1_b, ffn_w_up, ffn_conv_w, ffn_conv_b, ffn_w_down, ln2_g, ln2_b):
    b, s, d = x.shape
    assert b == 1 and d == D_MODEL
    assert s % 1024 == 0 and s // MOBA_BLOCK <= LANES
    depth = w_in.shape[0]

    tabs = _rope_tables(s, MLA_ROPE) + _rope_tables(s, MOBA_ROT)
    blk = jnp.arange(s, dtype=jnp.int32)[:, None] // MOBA_BLOCK
    onehot = (blk == jnp.arange(LANES, dtype=jnp.int32)[None, :]).astype(BF16)
    tabs = tabs + (onehot,)

    xf = x.reshape(s, d)
    xb = xf.astype(BF16)
    w_in_t = jnp.transpose(w_in, (0, 2, 1))
    w_out_b = w_out.astype(BF16)
    ffn_w_down_b = ffn_w_down.astype(BF16)
    for l in range(depth):
        wuq, wukv = _pack_mla_weights(mla_w_uq[l], mla_w_ukv[l])
        p = {
            "w_in_t": w_in_t,
            "layer": l,
            "mla_q_norm": mla_q_norm[l].reshape(1, -1),
            "mla_kv_norm": mla_kv_norm[l].reshape(1, -1),
            "w_uq": wuq, "w_ukv": wukv,
            "ssm_conv_w": ssm_conv_w[l],
            "ssm_conv_b": ssm_conv_b[l].reshape(1, -1),
            "ssm_dt_bias": _pad_cols(ssm_dt_bias[l].reshape(1, -1), LANES),
            "ssm_a_log": _pad_cols(ssm_a_log[l].reshape(1, -1), LANES),
            "ssm_d": jnp.repeat(ssm_d[l], SSM_HEADDIM).reshape(1, -1),
            "ssm_norm": ssm_norm[l].reshape(1, -1),
            "w_branch_a": w_branch_a, "w_branch_b": w_branch_b, "w_branch_c": w_branch_c,
            "gate_bias": gate_bias[l].reshape(1, -1),
            "w_out": w_out_b,
            "ln1_g": ln1_g[l].reshape(1, -1), "ln1_b": ln1_b[l].reshape(1, -1),
            "ffn_w_up": ffn_w_up,
            "ffn_conv_w": ffn_conv_w[l],
            "ffn_conv_b": ffn_conv_b[l].reshape(1, -1),
            "ffn_w_down": ffn_w_down_b,
            "ln2_g": ln2_g[l].reshape(1, -1), "ln2_b": ln2_b[l].reshape(1, -1),
        }
        xf, xb = _layer(xf, xb, p, tabs)
    return xf.reshape(b, s, d)
```

```python
import functools

import jax
import jax.numpy as jnp
from jax import lax
from jax.experimental import pallas as pl
from jax.experimental.pallas import tpu as pltpu

F32 = jnp.float32
BF16 = jnp.bfloat16

D_MODEL = 2048
DEPTH = 2
ROPE_THETA = 500000.0
LN_EPS = 1e-5
RMS_EPS = 1e-6
NEG_BIG = -1e30
LOG2E = 1.4426950408889634

MLA_HEADS = 8
MLA_Q_LORA = 512
MLA_KV_LORA = 256
MLA_NOPE = 128
MLA_ROPE = 64
MLA_V = 128
MLA_DK = MLA_NOPE + MLA_ROPE

SSM_D_INNER = D_MODEL
SSM_HEADDIM = 64
SSM_HEADS = SSM_D_INNER // SSM_HEADDIM
SSM_GROUPS = 4
SSM_STATE = 128
SSM_CONV = 4
SSM_CHUNK = 256
SSM_CONV_DIM = SSM_D_INNER + 2 * SSM_GROUPS * SSM_STATE
SSM_GROUP_W = SSM_D_INNER // SSM_GROUPS
SSM_GROUP_HEADS = SSM_HEADS // SSM_GROUPS

MOBA_HEADS = 8
MOBA_HEAD_DIM = 128
MOBA_ROT = MOBA_HEAD_DIM // 4
MOBA_BLOCK = 256
MOBA_TOPK = 3
MOBA_W = MOBA_HEADS * MOBA_HEAD_DIM

D_FF = 5632
FFN_CONV = 3
N_BRANCH = 3
DEEPNORM_ALPHA = (2 * DEPTH) ** 0.25

LANES = 128
SUBLANES = 8
BF16_ROWS = 16
VMEM_LIMIT = 56 * 1024 * 1024

IN_SIZES = (MLA_Q_LORA, MLA_KV_LORA, MLA_ROPE, SSM_D_INNER, SSM_CONV_DIM, SSM_HEADS,
            MOBA_W, MOBA_W, MOBA_W, N_BRANCH * D_MODEL)
COL_CQ = 0
COL_CKV = COL_CQ + MLA_Q_LORA
COL_KR = COL_CKV + MLA_KV_LORA
COL_DT = COL_KR + LANES
HA_COLS = COL_DT + LANES
COL_Z = 0
COL_XS = COL_Z + SSM_D_INNER
COL_B = COL_XS + SSM_D_INNER
COL_C = COL_B + SSM_GROUPS * SSM_STATE
HB_COLS = COL_C + SSM_GROUPS * SSM_STATE
COL_MOBA = 0
HC_COLS = COL_MOBA + 3 * MOBA_W
HG_COLS = N_BRANCH * D_MODEL


def _params(*sem):
    return pltpu.CompilerParams(dimension_semantics=sem, vmem_limit_bytes=VMEM_LIMIT)


def _sigmoid(x):
    return 1.0 / (1.0 + jnp.exp(-x))


def _silu(x):
    return x * _sigmoid(x)


def _split3(a):
    a1 = a.astype(BF16)
    r1 = a - a1.astype(F32)
    a2 = r1.astype(BF16)
    a3 = (r1 - a2.astype(F32)).astype(BF16)
    return a1, a2, a3


def _dot_nn(a, b):
    return jnp.dot(a, b, preferred_element_type=F32)


def _dot_nt(a, b):
    return lax.dot_general(a, b, (((1,), (1,)), ((), ())), preferred_element_type=F32)


def _layer_norm(y, g, b):
    mu = jnp.mean(y, axis=-1, keepdims=True)
    d = y - mu
    var = jnp.mean(d * d, axis=-1, keepdims=True)
    return d * lax.rsqrt(var + LN_EPS) * g + b


def _mm_kernel(x_ref, w_ref, o_ref):
    o_ref[...] = _dot_nn(x_ref[...], w_ref[...]).astype(o_ref.dtype)


def _matmul(x, w, out_dtype, tm, tn, name):
    m, k = x.shape
    n = w.shape[1]
    return pl.pallas_call(
        _mm_kernel,
        grid=(n // tn, m // tm),
        in_specs=[pl.BlockSpec((tm, k), lambda j, i: (i, 0)),
                  pl.BlockSpec((k, tn), lambda j, i: (0, j))],
        out_specs=pl.BlockSpec((tm, tn), lambda j, i: (i, j)),
        out_shape=jax.ShapeDtypeStruct((m, n), out_dtype),
        compiler_params=_params("parallel", "parallel"),
        name=name,
    )(x, w)


def _mm_wt_kernel(x_ref, *refs):
    wt_refs, o_ref, w_scr = refs[:-2], refs[-2], refs[-1]

    @pl.when(pl.program_id(1) == 0)
    def _():
        r = 0
        for wt_ref in wt_refs:
            w_scr[r:r + wt_ref.shape[0], :] = wt_ref[...].astype(BF16)
            r += wt_ref.shape[0]

    o_ref[...] = _dot_nt(x_ref[...], w_scr[...]).astype(o_ref.dtype)


def _matmul_wt(x, wt3, layer, windows, n_tiles, out_dtype, tm, name):
    m, k = x.shape
    tn = sum(rows for _, rows in windows)

    def wspec(r0, rows):
        assert r0 % SUBLANES == 0 and tn % SUBLANES == 0
        return pl.BlockSpec((None, pl.Element(rows), pl.Element(k)),
                            lambda j, i: (layer, pl.multiple_of(r0 + j * tn, SUBLANES), 0))

    return pl.pallas_call(
        _mm_wt_kernel,
        grid=(n_tiles, m // tm),
        in_specs=[pl.BlockSpec((tm, k), lambda j, i: (i, 0))] + [wspec(*w) for w in windows],
        out_specs=pl.BlockSpec((tm, tn), lambda j, i: (i, j)),
        out_shape=jax.ShapeDtypeStruct((m, n_tiles * tn), out_dtype),
        scratch_shapes=[pltpu.VMEM((tn, k), BF16)],
        compiler_params=_params("arbitrary", "arbitrary"),
        name=name,
    )(x, *([wt3] * len(windows)))


def _rope_swap(xs, half):
    lane = lax.broadcasted_iota(jnp.int32, xs.shape, 1)
    first = (lane & (2 * half - 1)) < half
    return jnp.where(first, pltpu.roll(xs, LANES - half, 1), pltpu.roll(xs, half, 1))


def _mla_prep_kernel(h_ref, qn_ref, kvn_ref, wuq_ref, wukv_ref, cos_ref, sin_ref,
                     qt_ref, k_ref, vt_ref):
    hm = h_ref[...]
    cq = hm[:, COL_CQ:COL_CQ + MLA_Q_LORA]
    ckv = hm[:, COL_CKV:COL_CKV + MLA_KV_LORA]
    kr = hm[:, COL_KR:COL_KR + LANES]
    nq = cq * lax.rsqrt(jnp.mean(cq * cq, axis=-1, keepdims=True) + RMS_EPS) * qn_ref[...]
    nkv = ckv * lax.rsqrt(jnp.mean(ckv * ckv, axis=-1, keepdims=True) + RMS_EPS) * kvn_ref[...]
    qu = _dot_nn(nq.astype(BF16), wuq_ref[...])
    kvu = _dot_nn(nkv.astype(BF16), wukv_ref[...])
    cos = cos_ref[...]
    sin = sin_ref[...]

    def rope(xs):
        return xs * cos + _rope_swap(xs, MLA_ROPE // 2) * sin

    scale = MLA_DK ** -0.5 * LOG2E
    kpe = rope(kr)[:, :MLA_ROPE].astype(BF16)
    nope0 = MLA_HEADS * LANES
    for h in range(MLA_HEADS):
        lo, hi = h * LANES, (h + 1) * LANES
        qr = rope(qu[:, lo:hi])
        qt_ref[h, 0:MLA_NOPE, :] = (qu[:, nope0 + lo:nope0 + hi] * scale).T.astype(BF16)
        qt_ref[h, MLA_NOPE:MLA_DK, :] = (qr * scale).T[:MLA_ROPE, :].astype(BF16)
        k_ref[h, :, 0:MLA_NOPE] = kvu[:, lo:hi].astype(BF16)
        k_ref[h, :, MLA_NOPE:MLA_DK] = kpe
        vt_ref[h] = kvu[:, nope0 + lo:nope0 + hi].T.astype(BF16)


def _mla_prep(h, q_norm, kv_norm, wuq_p, wukv_p, cos_t, sin_t, tm):
    s = h.shape[0]
    hd = MLA_HEADS
    return pl.pallas_call(
        _mla_prep_kernel,
        grid=(s // tm,),
        in_specs=[pl.BlockSpec((tm, HA_COLS), lambda i: (i, 0)),
                  pl.BlockSpec((1, MLA_Q_LORA), lambda i: (0, 0)),
                  pl.BlockSpec((1, MLA_KV_LORA), lambda i: (0, 0)),
                  pl.BlockSpec(wuq_p.shape, lambda i: (0, 0)),
                  pl.BlockSpec(wukv_p.shape, lambda i: (0, 0)),
                  pl.BlockSpec((tm, LANES), lambda i: (i, 0)),
                  pl.BlockSpec((tm, LANES), lambda i: (i, 0))],
        out_specs=[pl.BlockSpec((hd, MLA_DK, tm), lambda i: (0, 0, i)),
                   pl.BlockSpec((hd, tm, MLA_DK), lambda i: (0, i, 0)),
                   pl.BlockSpec((hd, MLA_V, tm), lambda i: (0, 0, i))],
        out_shape=[jax.ShapeDtypeStruct((hd, MLA_DK, s), BF16),
                   jax.ShapeDtypeStruct((hd, s, MLA_DK), BF16),
                   jax.ShapeDtypeStruct((hd, MLA_V, s), BF16)],
        compiler_params=_params("parallel"),
        name="mla_prep",
    )(h, q_norm, kv_norm, wuq_p, wukv_p, cos_t, sin_t)


SOFTMAX_ROWS = 16


def _fold_rows(x, op):
    out = x[0:SUBLANES]
    for r in range(1, x.shape[0] // SUBLANES):
        out = op(out, x[r * SUBLANES:(r + 1) * SUBLANES])
    return out


def _softmax_tile(s_ref, p_ref, m, l, t, causal):
    rows = SOFTMAX_ROWS

    def chunk(c):
        blk = s_ref[c * rows:(c + 1) * rows, :]
        if causal:
            key = c * rows + lax.broadcasted_iota(jnp.int32, (rows, t), 0)
            qry = lax.broadcasted_iota(jnp.int32, (rows, t), 1)
            blk = jnp.where(key <= qry, blk, -jnp.inf)
        return blk

    mx = None
    for c in range(t // rows):
        part = _fold_rows(chunk(c), jnp.maximum)
        mx = part if mx is None else jnp.maximum(mx, part)
    m_new = jnp.maximum(m, jnp.max(mx, axis=0, keepdims=True))
    alpha = jnp.exp2(m - m_new)
    tot = None
    for c in range(t // rows):
        p = jnp.exp2(chunk(c) - m_new)
        part = _fold_rows(p, jnp.add)
        tot = part if tot is None else tot + part
        p_ref[c * rows:(c + 1) * rows, :] = p.astype(BF16)
    l_new = alpha * l + jnp.sum(tot, axis=0, keepdims=True)
    return alpha, m_new, l_new


FLASH_HEADS = 4
RESIDENT = pl.Buffered(1)
FLASH_BUFS = 5


def _causal_flash(streams, i, t):
    def accumulate(st, a_prev, j_prev, p_ref):
        _, values, _, scratch = st
        acc_scr = scratch[4]
        acc_scr[...] = a_prev * acc_scr[...] + _dot_nn(values(jnp.maximum(j_prev, 0)), p_ref[...])

    def step_one(st, j, carry, cur):
        scores, _, _, scratch = st
        s_bufs, p_bufs = scratch[0:2], scratch[2:4]
        a_prev, m, l = carry
        s_bufs[1 - cur][...] = scores(j + 1)
        accumulate(st, a_prev, j - 1, p_bufs[1 - cur])
        return _softmax_tile(s_bufs[cur], p_bufs[cur], m, l, t, causal=False)

    def finish_one(st, carry, cur):
        _, _, write_out, scratch = st
        s_bufs, p_bufs, acc_scr = scratch[0:2], scratch[2:4], scratch[4]
        a_prev, m, l = carry
        accumulate(st, a_prev, i - 1, p_bufs[1 - cur])
        alpha, m, l = _softmax_tile(s_bufs[cur], p_bufs[cur], m, l, t, causal=True)
        accumulate(st, alpha, i, p_bufs[cur])
        write_out(acc_scr[...] / l)

    def step(j, carries, cur):
        return tuple(step_one(st, j, c, cur) for st, c in zip(streams, carries))

    def pair(jj, carries):
        return step(2 * jj + 1, step(2 * jj, carries, 0), 1)

    def finish(carries, cur):
        for st, c in zip(streams, carries):
            finish_one(st, c, cur)

    for scores, _, _, scratch in streams:
        scratch[0][...] = scores(0)
        scratch[3][...] = jnp.zeros((t, t), BF16)
        scratch[4][...] = jnp.zeros(scratch[4].shape, F32)

    init = (jnp.ones((1, t), F32), jnp.full((1, t), -jnp.inf, F32), jnp.zeros((1, t), F32))
    carries = lax.fori_loop(0, i // 2, pair, (init,) * len(streams))

    @pl.when(i % 2 == 0)
    def _():
        finish(carries, 0)

    @pl.when(i % 2 == 1)
    def _():
        finish(step(i - 1, carries, 0), 1)


def _flash_scratch(t, dv):
    per_head = [pltpu.VMEM((t, t), F32), pltpu.VMEM((t, t), F32),
                pltpu.VMEM((t, t), BF16), pltpu.VMEM((t, t), BF16), pltpu.VMEM((dv, t), F32)]
    assert len(per_head) == FLASH_BUFS
    return per_head * FLASH_HEADS


def _mla_attn_kernel(qt_ref, k_ref, vt_ref, o_ref, *scratch, t):
    i = pl.program_id(1)

    def stream(a):
        qt = qt_ref[a]

        def scores(j):
            return _dot_nn(k_ref[a, pl.ds(pl.multiple_of(j * t, t), t), :], qt)

        def values(j):
            return vt_ref[a, :, pl.ds(pl.multiple_of(j * t, t), t)]

        def write_out(out):
            o_ref[:, a * MLA_V:(a + 1) * MLA_V] = out.T.astype(o_ref.dtype)

        return scores, values, write_out, scratch[a * FLASH_BUFS:(a + 1) * FLASH_BUFS]

    _causal_flash([stream(a) for a in range(FLASH_HEADS)], i, t)


def _mla_attn(qt, k, vt, t):
    hd, _, s = qt.shape
    nh = FLASH_HEADS
    return pl.pallas_call(
        functools.partial(_mla_attn_kernel, t=t),
        grid=(hd // nh, s // t),
        in_specs=[pl.BlockSpec((nh, MLA_DK, t), lambda h, i: (h, 0, i)),
                  pl.BlockSpec((nh, s, MLA_DK), lambda h, i: (h, 0, 0), pipeline_mode=RESIDENT),
                  pl.BlockSpec((nh, MLA_V, s), lambda h, i: (h, 0, 0), pipeline_mode=RESIDENT)],
        out_specs=pl.BlockSpec((t, nh * MLA_V), lambda h, i: (i, h)),
        out_shape=jax.ShapeDtypeStruct((s, hd * MLA_V), BF16),
        scratch_shapes=_flash_scratch(t, MLA_V),
        compiler_params=_params("parallel", "parallel"),
        name="mla_attn",
    )(qt, k, vt)


def _moba_prep_kernel(h_ref, cos_ref, sin_ref, q_ref, k_ref, vt_ref, km_ref):
    cos = cos_ref[...]
    sin = sin_ref[...]

    def rope(xs):
        return xs * cos + _rope_swap(xs, MOBA_ROT // 2) * sin

    for h in range(MOBA_HEADS):
        lo, hi = h * LANES, (h + 1) * LANES
        q_ref[:, lo:hi] = rope(h_ref[:, lo:hi])
        kr = rope(h_ref[:, MOBA_W + lo:MOBA_W + hi])
        k_ref[:, lo:hi] = kr.astype(BF16)
        km_ref[0, :, lo:hi] = jnp.mean(kr, axis=0, keepdims=True)
        vt_ref[lo:hi, :] = h_ref[:, 2 * MOBA_W + lo:2 * MOBA_W + hi].T.astype(BF16)


def _moba_prep(h, cos_t, sin_t):
    s = h.shape[0]
    t = MOBA_BLOCK
    nb = s // t
    return pl.pallas_call(
        _moba_prep_kernel,
        grid=(nb,),
        in_specs=[pl.BlockSpec((t, 3 * MOBA_W), lambda i: (i, COL_MOBA // (3 * MOBA_W))),
                  pl.BlockSpec((t, LANES), lambda i: (i, 0)),
                  pl.BlockSpec((t, LANES), lambda i: (i, 0))],
        out_specs=[pl.BlockSpec((t, MOBA_W), lambda i: (i, 0)),
                   pl.BlockSpec((t, MOBA_W), lambda i: (i, 0)),
                   pl.BlockSpec((MOBA_W, t), lambda i: (0, i)),
                   pl.BlockSpec((1, 1, MOBA_W), lambda i: (i, 0, 0))],
        out_shape=[jax.ShapeDtypeStruct((s, MOBA_W), F32),
                   jax.ShapeDtypeStruct((s, MOBA_W), BF16),
                   jax.ShapeDtypeStruct((MOBA_W, s), BF16),
                   jax.ShapeDtypeStruct((nb, 1, MOBA_W), F32)],
        compiler_params=_params("parallel"),
        name="moba_prep",
    )(h, cos_t, sin_t)


def _moba_attn_kernel(q_ref, k_ref, vt_ref, km_ref, oh_ref, o_ref, *scratch, t):
    i = pl.program_id(1)
    slots = km_ref.shape[0]
    slot = lax.broadcasted_iota(jnp.int32, (slots, t), 0)
    slot_f = slot.astype(F32)
    qpos = i * t + lax.broadcasted_iota(jnp.int32, (1, t), 1)
    own = jnp.right_shift(qpos, MOBA_BLOCK.bit_length() - 1)

    def stream(a):
        cols = slice(a * LANES, (a + 1) * LANES)
        qt = q_ref[:, cols].T
        q1, q2, q3 = _split3(qt)
        k1, k2, k3 = _split3(km_ref[:, cols])
        gate = (_dot_nn(k1, q1) + _dot_nn(k2, q1) + _dot_nn(k1, q2)
                + _dot_nn(k3, q1) + _dot_nn(k2, q2) + _dot_nn(k1, q3))
        g = jnp.where(slot < own, gate, NEG_BIG)
        picked = slot < 0
        for _ in range(MOBA_TOPK):
            mx = jnp.max(g, axis=0, keepdims=True)
            first = jnp.min(jnp.where(g == mx, slot_f, float(slots)), axis=0, keepdims=True)
            hit = slot_f == first
            picked = picked | hit
            g = jnp.where(hit, -jnp.inf, g)
        visible = (picked & (slot < own)) | (slot == own)
        parts = [qt * (MOBA_HEAD_DIM ** -0.5 * LOG2E), jnp.where(visible, 0.0, NEG_BIG)]
        if slots < LANES:
            parts.append(jnp.zeros((LANES - slots, t), F32))
        qxt = jnp.concatenate(parts, axis=0).astype(BF16)

        def scores(j):
            off = pl.multiple_of(j * t, t)
            kx = jnp.concatenate([k_ref[pl.ds(off, t), cols], oh_ref[pl.ds(off, t), :]], axis=1)
            return _dot_nn(kx, qxt)

        def values(j):
            return vt_ref[cols, pl.ds(pl.multiple_of(j * t, t), t)]

        def write_out(out):
            o_ref[:, cols] = out.T.astype(o_ref.dtype)

        return scores, values, write_out, scratch[a * FLASH_BUFS:(a + 1) * FLASH_BUFS]

    _causal_flash([stream(a) for a in range(FLASH_HEADS)], i, t)


def _moba_attn(q, k, vt, kmean, onehot, t):
    s = q.shape[0]
    assert t % MOBA_BLOCK == 0
    w = FLASH_HEADS * LANES
    return pl.pallas_call(
        functools.partial(_moba_attn_kernel, t=t),
        grid=(MOBA_HEADS // FLASH_HEADS, s // t),
        in_specs=[pl.BlockSpec((t, w), lambda h, i: (i, h)),
                  pl.BlockSpec((s, w), lambda h, i: (0, h), pipeline_mode=RESIDENT),
                  pl.BlockSpec((w, s), lambda h, i: (h, 0), pipeline_mode=RESIDENT),
                  pl.BlockSpec((kmean.shape[0], w), lambda h, i: (0, h)),
                  pl.BlockSpec((s, LANES), lambda h, i: (0, 0), pipeline_mode=RESIDENT)],
        out_specs=pl.BlockSpec((t, w), lambda h, i: (i, h)),
        out_shape=jax.ShapeDtypeStruct((s, MOBA_W), BF16),
        scratch_shapes=_flash_scratch(t, MOBA_HEAD_DIM),
        compiler_params=_params("parallel", "parallel"),
        name="moba_attn",
    )(q, k, vt, kmean, onehot)


def _softplus(x):
    return jnp.maximum(x, 0.0) + jnp.log1p(jnp.exp(-jnp.abs(x)))


def _ssd_kernel(xp_ref, x_ref, bp_ref, b_ref, cp_ref, c_ref, dt_ref, z_ref,
                cwx_ref, cbx_ref, cwb_ref, cbb_ref, cwc_ref, cbc_ref,
                dtb_ref, alog_ref, dexp_ref, nw_ref, o_ref,
                xs_scr, bs_scr, cs_scr, st_scr):
    c = pl.program_id(0)
    g = pl.program_id(1)
    L = SSM_CHUNK
    P = SSM_HEADDIM
    halo = SUBLANES

    def conv_silu(prev_ref, cur_ref, scr, w_ref, bias_ref):
        scr[0:halo, :] = jnp.where(c > 0, prev_ref[...], 0.0)
        scr[halo:halo + L, :] = cur_ref[...]
        acc = bias_ref[...]
        for k in range(SSM_CONV):
            o = halo - (SSM_CONV - 1) + k
            acc = acc + w_ref[k:k + 1, :] * scr[o:o + L, :]
        return _silu(acc)

    xs = conv_silu(xp_ref, x_ref, xs_scr, cwx_ref, cbx_ref)
    bm = conv_silu(bp_ref, b_ref, bs_scr, cwb_ref, cbb_ref)
    cm = conv_silu(cp_ref, c_ref, cs_scr, cwc_ref, cbc_ref)

    dtv = _softplus(dt_ref[...] + dtb_ref[...])
    av = dtv * (-jnp.exp(alog_ref[...]))
    row = lax.broadcasted_iota(jnp.int32, (L, L), 0)
    col = lax.broadcasted_iota(jnp.int32, (L, L), 1)
    tril = col <= row
    ones_tril = jnp.where(tril, 1.0, 0.0).astype(BF16)
    a1, a2, a3 = _split3(av)
    acum = _dot_nn(ones_tril, a1) + _dot_nn(ones_tril, a2) + _dot_nn(ones_tril, a3)

    eh = lax.broadcasted_iota(jnp.int32, (LANES, SSM_GROUP_W), 0)
    ej = lax.broadcasted_iota(jnp.int32, (LANES, SSM_GROUP_W), 1)
    expand = jnp.where(eh == g * SSM_GROUP_HEADS + jnp.right_shift(ej, P.bit_length() - 1),
                       1.0, 0.0).astype(BF16)

    def expand3(v):
        v1, v2, v3 = _split3(v)
        return _dot_nn(v1, expand) + _dot_nn(v2, expand) + _dot_nn(v3, expand)

    dt_e = expand3(dtv)
    ac_e = expand3(acum)
    sr = lax.broadcasted_iota(jnp.int32, (BF16_ROWS, LANES), 0)
    sl = lax.broadcasted_iota(jnp.int32, (BF16_ROWS, LANES), 1)
    pick = jnp.where(sl == g * SSM_GROUP_HEADS + sr, 1.0, 0.0).astype(BF16)
    c1, c2, c3 = _split3(acum)
    ac_t = _dot_nt(pick, c1) + _dot_nt(pick, c2) + _dot_nt(pick, c3)

    bb = bm.astype(BF16)
    cb = cm.astype(BF16)
    gmat = _dot_nt(cb, bb)
    xdt = xs * dt_e
    xdt_b = xdt.astype(BF16)
    lane = lax.broadcasted_iota(jnp.int32, (L, LANES), 1)
    lo_half = lane < P
    parts = []
    for pr in range(SSM_GROUP_HEADS // 2):
        xpair = xdt_b[:, pr * LANES:(pr + 1) * LANES]
        acc = None
        for half in range(2):
            r = 2 * pr + half
            seg = ac_e[:, r * P:r * P + 1] - ac_t[r:r + 1, :]
            mh = (gmat * jnp.exp(jnp.where(tril, seg, -jnp.inf))).astype(BF16)
            xh = jnp.where(lo_half if half == 0 else jnp.logical_not(lo_half), xpair,
                           jnp.zeros_like(xpair))
            term = _dot_nn(mh, xh)
            acc = term if acc is None else acc + term
        parts.append(acc)
    y = jnp.concatenate(parts, axis=1)

    last = ac_e[L - 1:L, :]
    xw = (xdt * jnp.exp(last - ac_e)).astype(BF16)
    s_new = _dot_nn(bm.T.astype(BF16), xw)

    @pl.when(c == 0)
    def _():
        st_scr[g] = jnp.zeros((SSM_STATE, SSM_GROUP_W), F32)

    prev = st_scr[g]
    y = y + _dot_nn(cb, prev.astype(BF16)) * jnp.exp(ac_e)
    st_scr[g] = prev * jnp.exp(last) + s_new

    y = y + xs * dexp_ref[...]
    y = y * _silu(z_ref[...])
    y = y * lax.rsqrt(jnp.mean(y * y, axis=-1, keepdims=True) + RMS_EPS) * nw_ref[...]
    o_ref[...] = y.astype(o_ref.dtype)


def _ssd(h_a, h_b, conv_w, conv_b, dt_bias_p, a_log_p, d_exp, norm_w):
    s = h_b.shape[0]
    L = SSM_CHUNK
    nc = s // L
    gw = SSM_GROUP_W
    n = SSM_STATE
    rb = L // SUBLANES

    def prev_map(cblk):
        return lambda c, g: (jnp.maximum(c * rb - 1, 0), cblk + g)

    def cur_map(cblk):
        return lambda c, g: (c, cblk + g)

    xblk, bblk, cblk = COL_XS // gw, COL_B // n, COL_C // n
    wb, wc = SSM_D_INNER // n, (SSM_D_INNER + SSM_GROUPS * n) // n
    in_specs = [
        pl.BlockSpec((SUBLANES, gw), prev_map(xblk)), pl.BlockSpec((L, gw), cur_map(xblk)),
        pl.BlockSpec((SUBLANES, n), prev_map(bblk)), pl.BlockSpec((L, n), cur_map(bblk)),
        pl.BlockSpec((SUBLANES, n), prev_map(cblk)), pl.BlockSpec((L, n), cur_map(cblk)),
        pl.BlockSpec((L, LANES), lambda c, g: (c, COL_DT // LANES)),
        pl.BlockSpec((L, gw), cur_map(COL_Z // gw)),
        pl.BlockSpec((SSM_CONV, gw), lambda c, g: (0, g)), pl.BlockSpec((1, gw), lambda c, g: (0, g)),
        pl.BlockSpec((SSM_CONV, n), lambda c, g: (0, wb + g)), pl.BlockSpec((1, n), lambda c, g: (0, wb + g)),
        pl.BlockSpec((SSM_CONV, n), lambda c, g: (0, wc + g)), pl.BlockSpec((1, n), lambda c, g: (0, wc + g)),
        pl.BlockSpec((1, LANES), lambda c, g: (0, 0)),
        pl.BlockSpec((1, LANES), lambda c, g: (0, 0)),
        pl.BlockSpec((1, gw), lambda c, g: (0, g)),
        pl.BlockSpec((1, gw), lambda c, g: (0, g)),
    ]
    return pl.pallas_call(
        _ssd_kernel,
        grid=(nc, SSM_GROUPS),
        in_specs=in_specs,
        out_specs=pl.BlockSpec((L, gw), lambda c, g: (c, g)),
        out_shape=jax.ShapeDtypeStruct((s, SSM_D_INNER), BF16),
        scratch_shapes=[pltpu.VMEM((L + SUBLANES, gw), F32),
                        pltpu.VMEM((L + SUBLANES, n), F32),
                        pltpu.VMEM((L + SUBLANES, n), F32),
                        pltpu.VMEM((SSM_GROUPS, n, gw), F32)],
        compiler_params=_params("arbitrary", "arbitrary"),
        name="ssd",
    )(h_b, h_b, h_b, h_b, h_b, h_b, h_a, h_b, conv_w, conv_b, conv_w, conv_b, conv_w, conv_b,
      dt_bias_p, a_log_p, d_exp, norm_w)


SUB_ROWS = 256


def _merge_kernel(ya_ref, yb_ref, yc_ref, wa_ref, wb_ref, wc_ref,
                  g0_ref, g1_ref, g2_ref, b0_ref, b1_ref, b2_ref, o_ref,
                  wa_scr, wb_scr, wc_scr):
    @pl.when(pl.program_id(1) == 0)
    def _():
        wa_scr[...] = wa_ref[...].astype(BF16)
        wb_scr[...] = wb_ref[...].astype(BF16)
        wc_scr[...] = wc_ref[...].astype(BF16)

    def branch(rows, y_ref, w_scr, g_ref, b_ref):
        gate = _sigmoid(g_ref[rows, :].astype(F32) + b_ref[...])
        return gate * _dot_nn(y_ref[rows, :], w_scr[...])

    for r in range(o_ref.shape[0] // SUB_ROWS):
        rows = slice(r * SUB_ROWS, (r + 1) * SUB_ROWS)
        out = (branch(rows, ya_ref, wa_scr, g0_ref, b0_ref)
               + branch(rows, yb_ref, wb_scr, g1_ref, b1_ref)
               + branch(rows, yc_ref, wc_scr, g2_ref, b2_ref))
        o_ref[rows, :] = out.astype(o_ref.dtype)


def _merge(ya, yb, yc, wa, wb, wc, layer, h, gate_bias, tm, tn):
    s = ya.shape[0]
    nt = D_MODEL // tn
    gblk = 0

    def gmap(b):
        return lambda j, i: (i, gblk + b * nt + j)

    def bmap(b):
        return lambda j, i: (0, b * nt + j)

    def wspec(w):
        return pl.BlockSpec((None, w.shape[1], tn), lambda j, i: (layer, 0, j))

    return pl.pallas_call(
        _merge_kernel,
        grid=(nt, s // tm),
        in_specs=[pl.BlockSpec((tm, ya.shape[1]), lambda j, i: (i, 0)),
                  pl.BlockSpec((tm, yb.shape[1]), lambda j, i: (i, 0)),
                  pl.BlockSpec((tm, yc.shape[1]), lambda j, i: (i, 0)),
                  wspec(wa), wspec(wb), wspec(wc),
                  pl.BlockSpec((tm, tn), gmap(0)), pl.BlockSpec((tm, tn), gmap(1)),
                  pl.BlockSpec((tm, tn), gmap(2)),
                  pl.BlockSpec((1, tn), bmap(0)), pl.BlockSpec((1, tn), bmap(1)),
                  pl.BlockSpec((1, tn), bmap(2))],
        out_specs=pl.BlockSpec((tm, tn), lambda j, i: (i, j)),
        out_shape=jax.ShapeDtypeStruct((s, D_MODEL), BF16),
        scratch_shapes=[pltpu.VMEM((w.shape[1], tn), BF16) for w in (wa, wb, wc)],
        compiler_params=_params("arbitrary", "arbitrary"),
        name="branch_merge",
    )(ya, yb, yc, wa, wb, wc, h, h, h, gate_bias, gate_bias, gate_bias)


def _proj_ln_kernel(m_ref, w_ref, x_ref, g_ref, b_ref, of_ref, ob_ref, *, sub):
    for r0 in range(0, of_ref.shape[0], sub):
        rows = slice(r0, r0 + sub)
        y = DEEPNORM_ALPHA * x_ref[rows, :] + _dot_nn(m_ref[rows, :], w_ref[...])
        out = _layer_norm(y, g_ref[...], b_ref[...])
        of_ref[rows, :] = out
        ob_ref[rows, :] = out.astype(BF16)


def _proj_ln(m, w3, layer, x, g, b, tm, sub, name):
    s, d = x.shape
    w = w3
    return pl.pallas_call(
        functools.partial(_proj_ln_kernel, sub=sub),
        grid=(s // tm,),
        in_specs=[pl.BlockSpec((tm, m.shape[1]), lambda i: (i, 0)),
                  pl.BlockSpec((None,) + w.shape[1:], lambda i: (layer, 0, 0),
                               pipeline_mode=RESIDENT),
                  pl.BlockSpec((tm, d), lambda i: (i, 0)),
                  pl.BlockSpec((1, d), lambda i: (0, 0)),
                  pl.BlockSpec((1, d), lambda i: (0, 0))],
        out_specs=[pl.BlockSpec((tm, d), lambda i: (i, 0)),
                   pl.BlockSpec((tm, d), lambda i: (i, 0))],
        out_shape=[jax.ShapeDtypeStruct((s, d), F32), jax.ShapeDtypeStruct((s, d), BF16)],
        compiler_params=_params("parallel"),
        name=name,
    )(m, w, x, g, b)


def _ffn_up_kernel(x_ref, wg_ref, wu_ref, cwg_ref, cbg_ref, cwu_ref, cbu_ref, o_ref,
                   wgb_scr, wub_scr, ug_scr, uu_scr, *, tm):
    i = pl.program_id(1)
    hist = SUBLANES

    @pl.when(i == 0)
    def _():
        wgb_scr[...] = wg_ref[...].astype(BF16)
        wub_scr[...] = wu_ref[...].astype(BF16)
        ug_scr[0:hist, :] = jnp.zeros((hist, ug_scr.shape[1]), F32)
        uu_scr[0:hist, :] = jnp.zeros((hist, uu_scr.shape[1]), F32)

    def conv(scr, w_ref, bias_ref, r0):
        acc = bias_ref[...]
        for k in range(FFN_CONV):
            o = hist + r0 - (FFN_CONV - 1) + k
            acc = acc + w_ref[k:k + 1, :] * scr[o:o + SUB_ROWS, :]
        return acc

    for r0 in range(0, tm, SUB_ROWS):
        xs = x_ref[r0:r0 + SUB_ROWS, :]
        ug_scr[hist + r0:hist + r0 + SUB_ROWS, :] = _dot_nn(xs, wgb_scr[...])
        uu_scr[hist + r0:hist + r0 + SUB_ROWS, :] = _dot_nn(xs, wub_scr[...])
        gate = conv(ug_scr, cwg_ref, cbg_ref, r0)
        up = conv(uu_scr, cwu_ref, cbu_ref, r0)
        o_ref[r0:r0 + SUB_ROWS, :] = (_silu(gate) * up).astype(o_ref.dtype)

    ug_scr[0:hist, :] = ug_scr[tm:tm + hist, :]
    uu_scr[0:hist, :] = uu_scr[tm:tm + hist, :]


def _ffn_up(xb, w_up, layer, conv_w, conv_b, tm, tj):
    s, d = xb.shape
    nj = D_FF // tj
    return pl.pallas_call(
        functools.partial(_ffn_up_kernel, tm=tm),
        grid=(nj, s // tm),
        in_specs=[pl.BlockSpec((tm, d), lambda j, i: (i, 0)),
                  pl.BlockSpec((None, d, tj), lambda j, i: (layer, 0, j)),
                  pl.BlockSpec((None, d, tj), lambda j, i: (layer, 0, nj + j)),
                  pl.BlockSpec((FFN_CONV, tj), lambda j, i: (0, j)),
                  pl.BlockSpec((1, tj), lambda j, i: (0, j)),
                  pl.BlockSpec((FFN_CONV, tj), lambda j, i: (0, nj + j)),
                  pl.BlockSpec((1, tj), lambda j, i: (0, nj + j))],
        out_specs=pl.BlockSpec((tm, tj), lambda j, i: (i, j)),
        out_shape=jax.ShapeDtypeStruct((s, D_FF), BF16),
        scratch_shapes=[pltpu.VMEM((d, tj), BF16), pltpu.VMEM((d, tj), BF16),
                        pltpu.VMEM((tm + SUBLANES, tj), F32),
                        pltpu.VMEM((tm + SUBLANES, tj), F32)],
        compiler_params=_params("arbitrary", "arbitrary"),
        name="ffn_up_glu",
    )(xb, w_up, w_up, conv_w, conv_b, conv_w, conv_b)


def _pad_cols(w, width):
    return jnp.pad(w, ((0, 0), (0, width - w.shape[1])))


IN_OFFS = tuple(sum(IN_SIZES[:n]) for n in range(len(IN_SIZES) + 1))
W_IN_B0 = IN_OFFS[3]
W_IN_DT0 = IN_OFFS[5]
W_IN_C0 = IN_OFFS[6]
W_IN_G0 = IN_OFFS[9]
assert (W_IN_DT0 - W_IN_B0, W_IN_G0 - W_IN_C0, IN_OFFS[-1] - W_IN_G0) == (HB_COLS, HC_COLS, HG_COLS)


def _pack_mla_weights(w_uq, w_ukv):
    wq = w_uq.reshape(MLA_Q_LORA, MLA_HEADS, MLA_DK)
    rope_part = jnp.pad(wq[:, :, :MLA_ROPE], ((0, 0), (0, 0), (0, LANES - MLA_ROPE)))
    nope_part = wq[:, :, MLA_ROPE:]
    wq_p = jnp.concatenate([rope_part.reshape(MLA_Q_LORA, -1), nope_part.reshape(MLA_Q_LORA, -1)], axis=1)
    wkv = w_ukv.reshape(MLA_KV_LORA, MLA_HEADS, MLA_NOPE + MLA_V)
    wkv_p = jnp.concatenate([wkv[:, :, :MLA_NOPE].reshape(MLA_KV_LORA, -1),
                             wkv[:, :, MLA_NOPE:].reshape(MLA_KV_LORA, -1)], axis=1)
    return wq_p.astype(BF16), wkv_p.astype(BF16)


def _rope_tables(s, rot_dim):
    half = rot_dim // 2
    inv_freq = ROPE_THETA ** (-jnp.arange(half, dtype=F32) / half)
    ang = jnp.arange(s, dtype=jnp.int32).astype(F32)[:, None] * inv_freq[None, :]
    cos, sin = jnp.cos(ang), jnp.sin(ang)
    cos_g = jnp.concatenate([cos, cos], axis=1)
    sin_g = jnp.concatenate([-sin, sin], axis=1)
    if rot_dim == MLA_ROPE:
        reps = LANES // rot_dim
        return jnp.tile(cos_g, (1, reps)), jnp.tile(sin_g, (1, reps))
    rest = LANES - rot_dim
    return (jnp.concatenate([cos_g, jnp.ones((s, rest), F32)], axis=1),
            jnp.concatenate([sin_g, jnp.zeros((s, rest), F32)], axis=1))


def _layer(x, xb, p, tabs):
    s = x.shape[0]
    mla_cos, mla_sin, moba_cos, moba_sin, onehot = tabs
    w_in_t = p["w_in_t"]
    layer = p["layer"]
    tm = min(s, 1024)
    h_a = _matmul_wt(xb, w_in_t, layer, [(0, COL_DT), (W_IN_DT0, LANES)], 1, F32, tm, "in_proj_a")
    h_b = _matmul_wt(xb, w_in_t, layer, [(W_IN_B0, 1024)], HB_COLS // 1024, F32, tm, "in_proj_b")
    h_c = _matmul_wt(xb, w_in_t, layer, [(W_IN_C0, 1024)], HC_COLS // 1024, F32, tm, "in_proj_c")
    h_g = _matmul_wt(xb, w_in_t, layer, [(W_IN_G0, 1024)], HG_COLS // 1024, BF16, tm, "in_proj_g")

    qt, k, vt = _mla_prep(h_a, p["mla_q_norm"], p["mla_kv_norm"], p["w_uq"], p["w_ukv"],
                          mla_cos, mla_sin, 256)
    ya = _mla_attn(qt, k, vt, 512)

    yb = _ssd(h_a, h_b, p["ssm_conv_w"], p["ssm_conv_b"], p["ssm_dt_bias"], p["ssm_a_log"],
              p["ssm_d"], p["ssm_norm"])

    mq, mk, mvt, km = _moba_prep(h_c, moba_cos, moba_sin)
    nb = s // MOBA_BLOCK
    km = jnp.pad(km.reshape(nb, MOBA_W), ((0, -nb % BF16_ROWS), (0, 0)))
    yc = _moba_attn(mq, mk, mvt, km, onehot, 512)

    merged = _merge(ya, yb, yc, p["w_branch_a"], p["w_branch_b"], p["w_branch_c"], layer, h_g,
                    p["gate_bias"], tm, 512)
    x1, x1b = _proj_ln(merged, p["w_out"], layer, x, p["ln1_g"], p["ln1_b"], 512, SUB_ROWS,
                       "out_proj_ln")
    a = _ffn_up(x1b, p["ffn_w_up"], layer, p["ffn_conv_w"], p["ffn_conv_b"], tm, 512)
    return _proj_ln(a, p["ffn_w_down"], layer, x1, p["ln2_g"], p["ln2_b"], 256, SUB_ROWS // 2,
                    "ffn_down_ln")


def kernel(x, w_in, mla_q_norm, mla_w_uq, mla_kv_norm, mla_w_ukv, ssm_conv_w, ssm_conv_b, ssm_dt_bias, ssm_a_log, ssm_d, ssm_norm, w_branch_a, w_branch_b, w_branch_c, gate_bias, w_out, ln1_g, ln1_b, ffn_w_up, ffn_conv_w, ffn_conv_b, ffn_w_down, ln2_g, ln2_b):
    b, s, d = x.shape
    assert b == 1 and d == D_MODEL
    assert s % 1024 == 0 and s // MOBA_BLOCK <= LANES
    depth = w_in.shape[0]

    tabs = _rope_tables(s, MLA_ROPE) + _rope_tables(s, MOBA_ROT)
    blk = jnp.arange(s, dtype=jnp.int32)[:, None] // MOBA_BLOCK
    onehot = (blk == jnp.arange(LANES, dtype=jnp.int32)[None, :]).astype(BF16)
    tabs = tabs + (onehot,)

    xf = x.reshape(s, d)
    xb = xf.astype(BF16)
    w_in_t = jnp.transpose(w_in, (0, 2, 1))
    w_out_b = w_out.astype(BF16)
    ffn_w_down_b = ffn_w_down.astype(BF16)
    for l in range(depth):
        wuq, wukv = _pack_mla_weights(mla_w_uq[l], mla_w_ukv[l])
        p = {
            "w_in_t": w_in_t,
            "layer": l,
            "mla_q_norm": mla_q_norm[l].reshape(1, -1),
            "mla_kv_norm": mla_kv_norm[l].reshape(1, -1),
            "w_uq": wuq, "w_ukv": wukv,
            "ssm_conv_w": ssm_conv_w[l],
            "ssm_conv_b": ssm_conv_b[l].reshape(1, -1),
            "ssm_dt_bias": _pad_cols(ssm_dt_bias[l].reshape(1, -1), LANES),
            "ssm_a_log": _pad_cols(ssm_a_log[l].reshape(1, -1), LANES),
            "ssm_d": jnp.repeat(ssm_d[l], SSM_HEADDIM).reshape(1, -1),
            "ssm_norm": ssm_norm[l].reshape(1, -1),
            "w_branch_a": w_branch_a, "w_branch_b": w_branch_b, "w_branch_c": w_branch_c,
            "gate_bias": gate_bias[l].reshape(1, -1),
            "w_out": w_out_b,
            "ln1_g": ln1_g[l].reshape(1, -1), "ln1_b": ln1_b[l].reshape(1, -1),
            "ffn_w_up": ffn_w_up,
            "ffn_conv_w": ffn_conv_w[l],
            "ffn_conv_b": ffn_conv_b[l].reshape(1, -1),
            "ffn_w_down": ffn_w_down_b,
            "ln2_g": ln2_g[l].reshape(1, -1), "ln2_b": ln2_b[l].reshape(1, -1),
        }
        xf, xb = _layer(xf, xb, p, tabs)
    return xf.reshape(b, s, d)
```

```python
import functools

import jax
import jax.numpy as jnp
from jax import lax
from jax.experimental import pallas as pl
from jax.experimental.pallas import tpu as pltpu

F32 = jnp.float32
BF16 = jnp.bfloat16

D_MODEL = 2048
DEPTH = 2
ROPE_THETA = 500000.0
LN_EPS = 1e-5
RMS_EPS = 1e-6
NEG_BIG = -1e30
LOG2E = 1.4426950408889634

MLA_HEADS = 8
MLA_Q_LORA = 512
MLA_KV_LORA = 256
MLA_NOPE = 128
MLA_ROPE = 64
MLA_V = 128
MLA_DK = MLA_NOPE + MLA_ROPE

SSM_D_INNER = D_MODEL
SSM_HEADDIM = 64
SSM_HEADS = SSM_D_INNER // SSM_HEADDIM
SSM_GROUPS = 4
SSM_STATE = 128
SSM_CONV = 4
SSM_CHUNK = 256
SSM_CONV_DIM = SSM_D_INNER + 2 * SSM_GROUPS * SSM_STATE
SSM_GROUP_W = SSM_D_INNER // SSM_GROUPS
SSM_GROUP_HEADS = SSM_HEADS // SSM_GROUPS

MOBA_HEADS = 8
MOBA_HEAD_DIM = 128
MOBA_ROT = MOBA_HEAD_DIM // 4
MOBA_BLOCK = 256
MOBA_TOPK = 3
MOBA_W = MOBA_HEADS * MOBA_HEAD_DIM

D_FF = 5632
FFN_CONV = 3
N_BRANCH = 3
DEEPNORM_ALPHA = (2 * DEPTH) ** 0.25

LANES = 128
SUBLANES = 8
BF16_ROWS = 16
VMEM_LIMIT = 56 * 1024 * 1024

IN_SIZES = (MLA_Q_LORA, MLA_KV_LORA, MLA_ROPE, SSM_D_INNER, SSM_CONV_DIM, SSM_HEADS,
            MOBA_W, MOBA_W, MOBA_W, N_BRANCH * D_MODEL)
COL_CQ = 0
COL_CKV = COL_CQ + MLA_Q_LORA
COL_KR = COL_CKV + MLA_KV_LORA
COL_DT = COL_KR + LANES
HA_COLS = COL_DT + LANES
COL_Z = 0
COL_XS = COL_Z + SSM_D_INNER
COL_B = COL_XS + SSM_D_INNER
COL_C = COL_B + SSM_GROUPS * SSM_STATE
HB_COLS = COL_C + SSM_GROUPS * SSM_STATE
COL_MOBA = 0
HC_COLS = COL_MOBA + 3 * MOBA_W
HG_COLS = N_BRANCH * D_MODEL


def _params(*sem):
    return pltpu.CompilerParams(dimension_semantics=sem, vmem_limit_bytes=VMEM_LIMIT)


def _sigmoid(x):
    return 1.0 / (1.0 + jnp.exp(-x))


def _silu(x):
    return x * _sigmoid(x)


def _split3(a):
    a1 = a.astype(BF16)
    r1 = a - a1.astype(F32)
    a2 = r1.astype(BF16)
    a3 = (r1 - a2.astype(F32)).astype(BF16)
    return a1, a2, a3


def _dot_nn(a, b):
    return jnp.dot(a, b, preferred_element_type=F32)


def _dot_nt(a, b):
    return lax.dot_general(a, b, (((1,), (1,)), ((), ())), preferred_element_type=F32)


def _layer_norm(y, g, b):
    mu = jnp.mean(y, axis=-1, keepdims=True)
    d = y - mu
    var = jnp.mean(d * d, axis=-1, keepdims=True)
    return d * lax.rsqrt(var + LN_EPS) * g + b


def _mm_wt_kernel(x_ref, *refs):
    wt_refs, o_ref, w_scr = refs[:-2], refs[-2], refs[-1]

    @pl.when(pl.program_id(1) == 0)
    def _():
        r = 0
        for wt_ref in wt_refs:
            w_scr[r:r + wt_ref.shape[0], :] = wt_ref[...].astype(BF16)
            r += wt_ref.shape[0]

    o_ref[...] = _dot_nt(x_ref[...], w_scr[...]).astype(o_ref.dtype)


def _matmul_wt(x, wt3, layer, windows, n_tiles, out_dtype, tm, name):
    m, k = x.shape
    tn = sum(rows for _, rows in windows)

    def wspec(r0, rows):
        assert r0 % SUBLANES == 0 and tn % SUBLANES == 0
        return pl.BlockSpec((None, pl.Element(rows), pl.Element(k)),
                            lambda j, i: (layer, pl.multiple_of(r0 + j * tn, SUBLANES), 0))

    return pl.pallas_call(
        _mm_wt_kernel,
        grid=(n_tiles, m // tm),
        in_specs=[pl.BlockSpec((tm, k), lambda j, i: (i, 0))] + [wspec(*w) for w in windows],
        out_specs=pl.BlockSpec((tm, tn), lambda j, i: (i, j)),
        out_shape=jax.ShapeDtypeStruct((m, n_tiles * tn), out_dtype),
        scratch_shapes=[pltpu.VMEM((tn, k), BF16)],
        compiler_params=_params("arbitrary", "arbitrary"),
        name=name,
    )(x, *([wt3] * len(windows)))


def _rope_swap(xs, half):
    lane = lax.broadcasted_iota(jnp.int32, xs.shape, 1)
    first = (lane & (2 * half - 1)) < half
    return jnp.where(first, pltpu.roll(xs, LANES - half, 1), pltpu.roll(xs, half, 1))


def _mla_prep_kernel(h_ref, qn_ref, kvn_ref, wuq_ref, wukv_ref, cos_ref, sin_ref,
                     qt_ref, k_ref, vt_ref):
    hm = h_ref[...]
    cq = hm[:, COL_CQ:COL_CQ + MLA_Q_LORA]
    ckv = hm[:, COL_CKV:COL_CKV + MLA_KV_LORA]
    kr = hm[:, COL_KR:COL_KR + LANES]
    nq = cq * lax.rsqrt(jnp.mean(cq * cq, axis=-1, keepdims=True) + RMS_EPS) * qn_ref[...]
    nkv = ckv * lax.rsqrt(jnp.mean(ckv * ckv, axis=-1, keepdims=True) + RMS_EPS) * kvn_ref[...]
    qu = _dot_nn(nq.astype(BF16), wuq_ref[...])
    kvu = _dot_nn(nkv.astype(BF16), wukv_ref[...])
    cos = cos_ref[...]
    sin = sin_ref[...]

    def rope(xs):
        return xs * cos + _rope_swap(xs, MLA_ROPE // 2) * sin

    scale = MLA_DK ** -0.5 * LOG2E
    kpe = rope(kr)[:, :MLA_ROPE].astype(BF16)
    nope0 = MLA_HEADS * LANES
    for h in range(MLA_HEADS):
        lo, hi = h * LANES, (h + 1) * LANES
        qr = rope(qu[:, lo:hi])
        qt_ref[h, 0:MLA_NOPE, :] = (qu[:, nope0 + lo:nope0 + hi] * scale).T.astype(BF16)
        qt_ref[h, MLA_NOPE:MLA_DK, :] = (qr * scale).T[:MLA_ROPE, :].astype(BF16)
        k_ref[h, :, 0:MLA_NOPE] = kvu[:, lo:hi].astype(BF16)
        k_ref[h, :, MLA_NOPE:MLA_DK] = kpe
        vt_ref[h] = kvu[:, nope0 + lo:nope0 + hi].T.astype(BF16)


def _mla_prep(h, q_norm, kv_norm, wuq_p, wukv_p, cos_t, sin_t, tm):
    s = h.shape[0]
    hd = MLA_HEADS
    return pl.pallas_call(
        _mla_prep_kernel,
        grid=(s // tm,),
        in_specs=[pl.BlockSpec((tm, HA_COLS), lambda i: (i, 0)),
                  pl.BlockSpec((1, MLA_Q_LORA), lambda i: (0, 0)),
                  pl.BlockSpec((1, MLA_KV_LORA), lambda i: (0, 0)),
                  pl.BlockSpec(wuq_p.shape, lambda i: (0, 0)),
                  pl.BlockSpec(wukv_p.shape, lambda i: (0, 0)),
                  pl.BlockSpec((tm, LANES), lambda i: (i, 0)),
                  pl.BlockSpec((tm, LANES), lambda i: (i, 0))],
        out_specs=[pl.BlockSpec((hd, MLA_DK, tm), lambda i: (0, 0, i)),
                   pl.BlockSpec((hd, tm, MLA_DK), lambda i: (0, i, 0)),
                   pl.BlockSpec((hd, MLA_V, tm), lambda i: (0, 0, i))],
        out_shape=[jax.ShapeDtypeStruct((hd, MLA_DK, s), BF16),
                   jax.ShapeDtypeStruct((hd, s, MLA_DK), BF16),
                   jax.ShapeDtypeStruct((hd, MLA_V, s), BF16)],
        compiler_params=_params("parallel"),
        name="mla_prep",
    )(h, q_norm, kv_norm, wuq_p, wukv_p, cos_t, sin_t)


SOFTMAX_ROWS = 16


def _fold_rows(x, op):
    out = x[0:SUBLANES]
    for r in range(1, x.shape[0] // SUBLANES):
        out = op(out, x[r * SUBLANES:(r + 1) * SUBLANES])
    return out


def _softmax_tile(s_ref, p_ref, m, l, t, causal):
    rows = SOFTMAX_ROWS

    def chunk(c):
        blk = s_ref[c * rows:(c + 1) * rows, :]
        if causal:
            key = c * rows + lax.broadcasted_iota(jnp.int32, (rows, t), 0)
            qry = lax.broadcasted_iota(jnp.int32, (rows, t), 1)
            blk = jnp.where(key <= qry, blk, -jnp.inf)
        return blk

    mx = None
    for c in range(t // rows):
        part = _fold_rows(chunk(c), jnp.maximum)
        mx = part if mx is None else jnp.maximum(mx, part)
    m_new = jnp.maximum(m, jnp.max(mx, axis=0, keepdims=True))
    alpha = jnp.exp2(m - m_new)
    tot = None
    for c in range(t // rows):
        p = jnp.exp2(chunk(c) - m_new)
        part = _fold_rows(p, jnp.add)
        tot = part if tot is None else tot + part
        p_ref[c * rows:(c + 1) * rows, :] = p.astype(BF16)
    l_new = alpha * l + jnp.sum(tot, axis=0, keepdims=True)
    return alpha, m_new, l_new


FLASH_HEADS = 4
RESIDENT = pl.Buffered(1)
FLASH_BUFS = 5


def _causal_flash(streams, i, t):
    def accumulate(st, a_prev, j_prev, p_ref):
        _, values, _, scratch = st
        acc_scr = scratch[4]
        acc_scr[...] = a_prev * acc_scr[...] + _dot_nn(values(jnp.maximum(j_prev, 0)), p_ref[...])

    def step_one(st, j, carry, cur):
        scores, _, _, scratch = st
        s_bufs, p_bufs = scratch[0:2], scratch[2:4]
        a_prev, m, l = carry
        s_bufs[1 - cur][...] = scores(j + 1)
        accumulate(st, a_prev, j - 1, p_bufs[1 - cur])
        return _softmax_tile(s_bufs[cur], p_bufs[cur], m, l, t, causal=False)

    def finish_one(st, carry, cur):
        _, _, write_out, scratch = st
        s_bufs, p_bufs, acc_scr = scratch[0:2], scratch[2:4], scratch[4]
        a_prev, m, l = carry
        accumulate(st, a_prev, i - 1, p_bufs[1 - cur])
        alpha, m, l = _softmax_tile(s_bufs[cur], p_bufs[cur], m, l, t, causal=True)
        accumulate(st, alpha, i, p_bufs[cur])
        write_out(acc_scr[...] / l)

    def step(j, carries, cur):
        return tuple(step_one(st, j, c, cur) for st, c in zip(streams, carries))

    def pair(jj, carries):
        return step(2 * jj + 1, step(2 * jj, carries, 0), 1)

    def finish(carries, cur):
        for st, c in zip(streams, carries):
            finish_one(st, c, cur)

    for scores, _, _, scratch in streams:
        scratch[0][...] = scores(0)
        scratch[3][...] = jnp.zeros((t, t), BF16)
        scratch[4][...] = jnp.zeros(scratch[4].shape, F32)

    init = (jnp.ones((1, t), F32), jnp.full((1, t), -jnp.inf, F32), jnp.zeros((1, t), F32))
    carries = lax.fori_loop(0, i // 2, pair, (init,) * len(streams))

    @pl.when(i % 2 == 0)
    def _():
        finish(carries, 0)

    @pl.when(i % 2 == 1)
    def _():
        finish(step(i - 1, carries, 0), 1)


def _flash_scratch(t, dv):
    per_head = [pltpu.VMEM((t, t), F32), pltpu.VMEM((t, t), F32),
                pltpu.VMEM((t, t), BF16), pltpu.VMEM((t, t), BF16), pltpu.VMEM((dv, t), F32)]
    assert len(per_head) == FLASH_BUFS
    return per_head * FLASH_HEADS


def _mla_attn_kernel(qt_ref, k_ref, vt_ref, o_ref, *scratch, t):
    i = pl.program_id(1)

    def stream(a):
        qt = qt_ref[a]

        def scores(j):
            return _dot_nn(k_ref[a, pl.ds(pl.multiple_of(j * t, t), t), :], qt)

        def values(j):
            return vt_ref[a, :, pl.ds(pl.multiple_of(j * t, t), t)]

        def write_out(out):
            o_ref[:, a * MLA_V:(a + 1) * MLA_V] = out.T.astype(o_ref.dtype)

        return scores, values, write_out, scratch[a * FLASH_BUFS:(a + 1) * FLASH_BUFS]

    _causal_flash([stream(a) for a in range(FLASH_HEADS)], i, t)


def _mla_attn(qt, k, vt, t):
    hd, _, s = qt.shape
    nh = FLASH_HEADS
    return pl.pallas_call(
        functools.partial(_mla_attn_kernel, t=t),
        grid=(hd // nh, s // t),
        in_specs=[pl.BlockSpec((nh, MLA_DK, t), lambda h, i: (h, 0, i)),
                  pl.BlockSpec((nh, s, MLA_DK), lambda h, i: (h, 0, 0), pipeline_mode=RESIDENT),
                  pl.BlockSpec((nh, MLA_V, s), lambda h, i: (h, 0, 0), pipeline_mode=RESIDENT)],
        out_specs=pl.BlockSpec((t, nh * MLA_V), lambda h, i: (i, h)),
        out_shape=jax.ShapeDtypeStruct((s, hd * MLA_V), BF16),
        scratch_shapes=_flash_scratch(t, MLA_V),
        compiler_params=_params("parallel", "parallel"),
        name="mla_attn",
    )(qt, k, vt)


def _moba_prep_kernel(h_ref, cos_ref, sin_ref, q_ref, k_ref, vt_ref, km_ref):
    cos = cos_ref[...]
    sin = sin_ref[...]

    def rope(xs):
        return xs * cos + _rope_swap(xs, MOBA_ROT // 2) * sin

    for h in range(MOBA_HEADS):
        lo, hi = h * LANES, (h + 1) * LANES
        q_ref[:, lo:hi] = rope(h_ref[:, lo:hi])
        kr = rope(h_ref[:, MOBA_W + lo:MOBA_W + hi])
        k_ref[:, lo:hi] = kr.astype(BF16)
        km_ref[0, :, lo:hi] = jnp.mean(kr, axis=0, keepdims=True)
        vt_ref[lo:hi, :] = h_ref[:, 2 * MOBA_W + lo:2 * MOBA_W + hi].T.astype(BF16)


def _moba_prep(h, cos_t, sin_t):
    s = h.shape[0]
    t = MOBA_BLOCK
    nb = s // t
    return pl.pallas_call(
        _moba_prep_kernel,
        grid=(nb,),
        in_specs=[pl.BlockSpec((t, 3 * MOBA_W), lambda i: (i, COL_MOBA // (3 * MOBA_W))),
                  pl.BlockSpec((t, LANES), lambda i: (i, 0)),
                  pl.BlockSpec((t, LANES), lambda i: (i, 0))],
        out_specs=[pl.BlockSpec((t, MOBA_W), lambda i: (i, 0)),
                   pl.BlockSpec((t, MOBA_W), lambda i: (i, 0)),
                   pl.BlockSpec((MOBA_W, t), lambda i: (0, i)),
                   pl.BlockSpec((1, 1, MOBA_W), lambda i: (i, 0, 0))],
        out_shape=[jax.ShapeDtypeStruct((s, MOBA_W), F32),
                   jax.ShapeDtypeStruct((s, MOBA_W), BF16),
                   jax.ShapeDtypeStruct((MOBA_W, s), BF16),
                   jax.ShapeDtypeStruct((nb, 1, MOBA_W), F32)],
        compiler_params=_params("parallel"),
        name="moba_prep",
    )(h, cos_t, sin_t)


def _moba_attn_kernel(q_ref, k_ref, vt_ref, km_ref, oh_ref, o_ref, *scratch, t):
    i = pl.program_id(1)
    slots = km_ref.shape[0]
    slot = lax.broadcasted_iota(jnp.int32, (slots, t), 0)
    slot_f = slot.astype(F32)
    qpos = i * t + lax.broadcasted_iota(jnp.int32, (1, t), 1)
    own = jnp.right_shift(qpos, MOBA_BLOCK.bit_length() - 1)

    def stream(a):
        cols = slice(a * LANES, (a + 1) * LANES)
        qt = q_ref[:, cols].T
        q1, q2, q3 = _split3(qt)
        k1, k2, k3 = _split3(km_ref[:, cols])
        gate = (_dot_nn(k1, q1) + _dot_nn(k2, q1) + _dot_nn(k1, q2)
                + _dot_nn(k3, q1) + _dot_nn(k2, q2) + _dot_nn(k1, q3))
        g = jnp.where(slot < own, gate, NEG_BIG)
        picked = slot < 0
        for _ in range(MOBA_TOPK):
            mx = jnp.max(g, axis=0, keepdims=True)
            first = jnp.min(jnp.where(g == mx, slot_f, float(slots)), axis=0, keepdims=True)
            hit = slot_f == first
            picked = picked | hit
            g = jnp.where(hit, -jnp.inf, g)
        visible = (picked & (slot < own)) | (slot == own)
        parts = [qt * (MOBA_HEAD_DIM ** -0.5 * LOG2E), jnp.where(visible, 0.0, NEG_BIG)]
        if slots < LANES:
            parts.append(jnp.zeros((LANES - slots, t), F32))
        qxt = jnp.concatenate(parts, axis=0).astype(BF16)

        def scores(j):
            off = pl.multiple_of(j * t, t)
            kx = jnp.concatenate([k_ref[pl.ds(off, t), cols], oh_ref[pl.ds(off, t), :]], axis=1)
            return _dot_nn(kx, qxt)

        def values(j):
            return vt_ref[cols, pl.ds(pl.multiple_of(j * t, t), t)]

        def write_out(out):
            o_ref[:, cols] = out.T.astype(o_ref.dtype)

        return scores, values, write_out, scratch[a * FLASH_BUFS:(a + 1) * FLASH_BUFS]

    _causal_flash([stream(a) for a in range(FLASH_HEADS)], i, t)


def _moba_attn(q, k, vt, kmean, onehot, t):
    s = q.shape[0]
    assert t % MOBA_BLOCK == 0
    w = FLASH_HEADS * LANES
    return pl.pallas_call(
        functools.partial(_moba_attn_kernel, t=t),
        grid=(MOBA_HEADS // FLASH_HEADS, s // t),
        in_specs=[pl.BlockSpec((t, w), lambda h, i: (i, h)),
                  pl.BlockSpec((s, w), lambda h, i: (0, h), pipeline_mode=RESIDENT),
                  pl.BlockSpec((w, s), lambda h, i: (h, 0), pipeline_mode=RESIDENT),
                  pl.BlockSpec((kmean.shape[0], w), lambda h, i: (0, h)),
                  pl.BlockSpec((s, LANES), lambda h, i: (0, 0), pipeline_mode=RESIDENT)],
        out_specs=pl.BlockSpec((t, w), lambda h, i: (i, h)),
        out_shape=jax.ShapeDtypeStruct((s, MOBA_W), BF16),
        scratch_shapes=_flash_scratch(t, MOBA_HEAD_DIM),
        compiler_params=_params("parallel", "parallel"),
        name="moba_attn",
    )(q, k, vt, kmean, onehot)


def _softplus(x):
    return jnp.maximum(x, 0.0) + jnp.log1p(jnp.exp(-jnp.abs(x)))


def _ssd_kernel(xp_ref, x_ref, bp_ref, b_ref, cp_ref, c_ref, dt_ref, z_ref,
                cwx_ref, cbx_ref, cwb_ref, cbb_ref, cwc_ref, cbc_ref,
                dtb_ref, alog_ref, dexp_ref, nw_ref, o_ref,
                xs_scr, bs_scr, cs_scr, st_scr):
    c = pl.program_id(0)
    g = pl.program_id(1)
    L = SSM_CHUNK
    P = SSM_HEADDIM
    halo = SUBLANES

    def conv_silu(prev_ref, cur_ref, scr, w_ref, bias_ref):
        scr[0:halo, :] = jnp.where(c > 0, prev_ref[...], 0.0)
        scr[halo:halo + L, :] = cur_ref[...]
        acc = bias_ref[...]
        for k in range(SSM_CONV):
            o = halo - (SSM_CONV - 1) + k
            acc = acc + w_ref[k:k + 1, :] * scr[o:o + L, :]
        return _silu(acc)

    xs = conv_silu(xp_ref, x_ref, xs_scr, cwx_ref, cbx_ref)
    bm = conv_silu(bp_ref, b_ref, bs_scr, cwb_ref, cbb_ref)
    cm = conv_silu(cp_ref, c_ref, cs_scr, cwc_ref, cbc_ref)

    dtv = _softplus(dt_ref[...] + dtb_ref[...])
    av = dtv * (-jnp.exp(alog_ref[...]))
    row = lax.broadcasted_iota(jnp.int32, (L, L), 0)
    col = lax.broadcasted_iota(jnp.int32, (L, L), 1)
    tril = col <= row
    ones_tril = jnp.where(tril, 1.0, 0.0).astype(BF16)
    a1, a2, a3 = _split3(av)
    acum = _dot_nn(ones_tril, a1) + _dot_nn(ones_tril, a2) + _dot_nn(ones_tril, a3)

    eh = lax.broadcasted_iota(jnp.int32, (LANES, SSM_GROUP_W), 0)
    ej = lax.broadcasted_iota(jnp.int32, (LANES, SSM_GROUP_W), 1)
    expand = jnp.where(eh == g * SSM_GROUP_HEADS + jnp.right_shift(ej, P.bit_length() - 1),
                       1.0, 0.0).astype(BF16)

    def expand3(v):
        v1, v2, v3 = _split3(v)
        return _dot_nn(v1, expand) + _dot_nn(v2, expand) + _dot_nn(v3, expand)

    dt_e = expand3(dtv)
    ac_e = expand3(acum)
    sr = lax.broadcasted_iota(jnp.int32, (BF16_ROWS, LANES), 0)
    sl = lax.broadcasted_iota(jnp.int32, (BF16_ROWS, LANES), 1)
    pick = jnp.where(sl == g * SSM_GROUP_HEADS + sr, 1.0, 0.0).astype(BF16)
    c1, c2, c3 = _split3(acum)
    ac_t = _dot_nt(pick, c1) + _dot_nt(pick, c2) + _dot_nt(pick, c3)

    bb = bm.astype(BF16)
    cb = cm.astype(BF16)
    gmat = _dot_nt(cb, bb)
    xdt = xs * dt_e
    xdt_b = xdt.astype(BF16)
    lane = lax.broadcasted_iota(jnp.int32, (L, LANES), 1)
    lo_half = lane < P
    parts = []
    for pr in range(SSM_GROUP_HEADS // 2):
        xpair = xdt_b[:, pr * LANES:(pr + 1) * LANES]
        acc = None
        for half in range(2):
            r = 2 * pr + half
            seg = ac_e[:, r * P:r * P + 1] - ac_t[r:r + 1, :]
            mh = (gmat * jnp.exp(jnp.where(tril, seg, -jnp.inf))).astype(BF16)
            xh = jnp.where(lo_half if half == 0 else jnp.logical_not(lo_half), xpair,
                           jnp.zeros_like(xpair))
            term = _dot_nn(mh, xh)
            acc = term if acc is None else acc + term
        parts.append(acc)
    y = jnp.concatenate(parts, axis=1)

    last = ac_e[L - 1:L, :]
    xw = (xdt * jnp.exp(last - ac_e)).astype(BF16)
    s_new = _dot_nn(bm.T.astype(BF16), xw)

    @pl.when(c == 0)
    def _():
        st_scr[g] = jnp.zeros((SSM_STATE, SSM_GROUP_W), F32)

    prev = st_scr[g]
    y = y + _dot_nn(cb, prev.astype(BF16)) * jnp.exp(ac_e)
    st_scr[g] = prev * jnp.exp(last) + s_new

    y = y + xs * dexp_ref[...]
    y = y * _silu(z_ref[...])
    y = y * lax.rsqrt(jnp.mean(y * y, axis=-1, keepdims=True) + RMS_EPS) * nw_ref[...]
    o_ref[...] = y.astype(o_ref.dtype)


def _ssd(h_a, h_b, conv_w, conv_b, dt_bias_p, a_log_p, d_exp, norm_w):
    s = h_b.shape[0]
    L = SSM_CHUNK
    nc = s // L
    gw = SSM_GROUP_W
    n = SSM_STATE
    rb = L // SUBLANES

    def prev_map(cblk):
        return lambda c, g: (jnp.maximum(c * rb - 1, 0), cblk + g)

    def cur_map(cblk):
        return lambda c, g: (c, cblk + g)

    xblk, bblk, cblk = COL_XS // gw, COL_B // n, COL_C // n
    wb, wc = SSM_D_INNER // n, (SSM_D_INNER + SSM_GROUPS * n) // n
    in_specs = [
        pl.BlockSpec((SUBLANES, gw), prev_map(xblk)), pl.BlockSpec((L, gw), cur_map(xblk)),
        pl.BlockSpec((SUBLANES, n), prev_map(bblk)), pl.BlockSpec((L, n), cur_map(bblk)),
        pl.BlockSpec((SUBLANES, n), prev_map(cblk)), pl.BlockSpec((L, n), cur_map(cblk)),
        pl.BlockSpec((L, LANES), lambda c, g: (c, COL_DT // LANES)),
        pl.BlockSpec((L, gw), cur_map(COL_Z // gw)),
        pl.BlockSpec((SSM_CONV, gw), lambda c, g: (0, g)), pl.BlockSpec((1, gw), lambda c, g: (0, g)),
        pl.BlockSpec((SSM_CONV, n), lambda c, g: (0, wb + g)), pl.BlockSpec((1, n), lambda c, g: (0, wb + g)),
        pl.BlockSpec((SSM_CONV, n), lambda c, g: (0, wc + g)), pl.BlockSpec((1, n), lambda c, g: (0, wc + g)),
        pl.BlockSpec((1, LANES), lambda c, g: (0, 0)),
        pl.BlockSpec((1, LANES), lambda c, g: (0, 0)),
        pl.BlockSpec((1, gw), lambda c, g: (0, g)),
        pl.BlockSpec((1, gw), lambda c, g: (0, g)),
    ]
    return pl.pallas_call(
        _ssd_kernel,
        grid=(nc, SSM_GROUPS),
        in_specs=in_specs,
        out_specs=pl.BlockSpec((L, gw), lambda c, g: (c, g)),
        out_shape=jax.ShapeDtypeStruct((s, SSM_D_INNER), BF16),
        scratch_shapes=[pltpu.VMEM((L + SUBLANES, gw), F32),
                        pltpu.VMEM((L + SUBLANES, n), F32),
                        pltpu.VMEM((L + SUBLANES, n), F32),
                        pltpu.VMEM((SSM_GROUPS, n, gw), F32)],
        compiler_params=_params("arbitrary", "arbitrary"),
        name="ssd",
    )(h_b, h_b, h_b, h_b, h_b, h_b, h_a, h_b, conv_w, conv_b, conv_w, conv_b, conv_w, conv_b,
      dt_bias_p, a_log_p, d_exp, norm_w)


SUB_ROWS = 256


def _merge_kernel(ya_ref, yb_ref, yc_ref, wa_ref, wb_ref, wc_ref,
                  g0_ref, g1_ref, g2_ref, b0_ref, b1_ref, b2_ref, o_ref,
                  wa_scr, wb_scr, wc_scr):
    @pl.when(pl.program_id(1) == 0)
    def _():
        wa_scr[...] = wa_ref[...].astype(BF16)
        wb_scr[...] = wb_ref[...].astype(BF16)
        wc_scr[...] = wc_ref[...].astype(BF16)

    def branch(rows, y_ref, w_scr, g_ref, b_ref):
        gate = _sigmoid(g_ref[rows, :].astype(F32) + b_ref[...])
        return gate * _dot_nn(y_ref[rows, :], w_scr[...])

    for r in range(o_ref.shape[0] // SUB_ROWS):
        rows = slice(r * SUB_ROWS, (r + 1) * SUB_ROWS)
        out = (branch(rows, ya_ref, wa_scr, g0_ref, b0_ref)
               + branch(rows, yb_ref, wb_scr, g1_ref, b1_ref)
               + branch(rows, yc_ref, wc_scr, g2_ref, b2_ref))
        o_ref[rows, :] = out.astype(o_ref.dtype)


def _merge(ya, yb, yc, wa, wb, wc, layer, h, gate_bias, tm, tn):
    s = ya.shape[0]
    nt = D_MODEL // tn
    gblk = 0

    def gmap(b):
        return lambda j, i: (i, gblk + b * nt + j)

    def bmap(b):
        return lambda j, i: (0, b * nt + j)

    def wspec(w):
        return pl.BlockSpec((None, w.shape[1], tn), lambda j, i: (layer, 0, j))

    return pl.pallas_call(
        _merge_kernel,
        grid=(nt, s // tm),
        in_specs=[pl.BlockSpec((tm, ya.shape[1]), lambda j, i: (i, 0)),
                  pl.BlockSpec((tm, yb.shape[1]), lambda j, i: (i, 0)),
                  pl.BlockSpec((tm, yc.shape[1]), lambda j, i: (i, 0)),
                  wspec(wa), wspec(wb), wspec(wc),
                  pl.BlockSpec((tm, tn), gmap(0)), pl.BlockSpec((tm, tn), gmap(1)),
                  pl.BlockSpec((tm, tn), gmap(2)),
                  pl.BlockSpec((1, tn), bmap(0)), pl.BlockSpec((1, tn), bmap(1)),
                  pl.BlockSpec((1, tn), bmap(2))],
        out_specs=pl.BlockSpec((tm, tn), lambda j, i: (i, j)),
        out_shape=jax.ShapeDtypeStruct((s, D_MODEL), BF16),
        scratch_shapes=[pltpu.VMEM((w.shape[1], tn), BF16) for w in (wa, wb, wc)],
        compiler_params=_params("arbitrary", "arbitrary"),
        name="branch_merge",
    )(ya, yb, yc, wa, wb, wc, h, h, h, gate_bias, gate_bias, gate_bias)


def _proj_ln_kernel(m_ref, w_ref, x_ref, g_ref, b_ref, of_ref, ob_ref, *, sub):
    for r0 in range(0, of_ref.shape[0], sub):
        rows = slice(r0, r0 + sub)
        y = DEEPNORM_ALPHA * x_ref[rows, :] + _dot_nn(m_ref[rows, :], w_ref[...])
        out = _layer_norm(y, g_ref[...], b_ref[...])
        of_ref[rows, :] = out
        ob_ref[rows, :] = out.astype(BF16)


def _proj_ln(m, w3, layer, x, g, b, tm, sub, name):
    s, d = x.shape
    w = w3
    return pl.pallas_call(
        functools.partial(_proj_ln_kernel, sub=sub),
        grid=(s // tm,),
        in_specs=[pl.BlockSpec((tm, m.shape[1]), lambda i: (i, 0)),
                  pl.BlockSpec((None,) + w.shape[1:], lambda i: (layer, 0, 0),
                               pipeline_mode=RESIDENT),
                  pl.BlockSpec((tm, d), lambda i: (i, 0)),
                  pl.BlockSpec((1, d), lambda i: (0, 0)),
                  pl.BlockSpec((1, d), lambda i: (0, 0))],
        out_specs=[pl.BlockSpec((tm, d), lambda i: (i, 0)),
                   pl.BlockSpec((tm, d), lambda i: (i, 0))],
        out_shape=[jax.ShapeDtypeStruct((s, d), F32), jax.ShapeDtypeStruct((s, d), BF16)],
        compiler_params=_params("parallel"),
        name=name,
    )(m, w, x, g, b)


def _ffn_up_kernel(x_ref, wg_ref, wu_ref, cwg_ref, cbg_ref, cwu_ref, cbu_ref, o_ref,
                   wgb_scr, wub_scr, ug_scr, uu_scr, *, tm):
    i = pl.program_id(1)
    hist = SUBLANES

    @pl.when(i == 0)
    def _():
        wgb_scr[...] = wg_ref[...].astype(BF16)
        wub_scr[...] = wu_ref[...].astype(BF16)
        ug_scr[0:hist, :] = jnp.zeros((hist, ug_scr.shape[1]), F32)
        uu_scr[0:hist, :] = jnp.zeros((hist, uu_scr.shape[1]), F32)

    def conv(scr, w_ref, bias_ref, r0):
        acc = bias_ref[...]
        for k in range(FFN_CONV):
            o = hist + r0 - (FFN_CONV - 1) + k
            acc = acc + w_ref[k:k + 1, :] * scr[o:o + SUB_ROWS, :]
        return acc

    for r0 in range(0, tm, SUB_ROWS):
        xs = x_ref[r0:r0 + SUB_ROWS, :]
        ug_scr[hist + r0:hist + r0 + SUB_ROWS, :] = _dot_nn(xs, wgb_scr[...])
        uu_scr[hist + r0:hist + r0 + SUB_ROWS, :] = _dot_nn(xs, wub_scr[...])
        gate = conv(ug_scr, cwg_ref, cbg_ref, r0)
        up = conv(uu_scr, cwu_ref, cbu_ref, r0)
        o_ref[r0:r0 + SUB_ROWS, :] = (_silu(gate) * up).astype(o_ref.dtype)

    ug_scr[0:hist, :] = ug_scr[tm:tm + hist, :]
    uu_scr[0:hist, :] = uu_scr[tm:tm + hist, :]


def _ffn_up(xb, w_up, layer, conv_w, conv_b, tm, tj):
    s, d = xb.shape
    nj = D_FF // tj
    return pl.pallas_call(
        functools.partial(_ffn_up_kernel, tm=tm),
        grid=(nj, s // tm),
        in_specs=[pl.BlockSpec((tm, d), lambda j, i: (i, 0)),
                  pl.BlockSpec((None, d, tj), lambda j, i: (layer, 0, j)),
                  pl.BlockSpec((None, d, tj), lambda j, i: (layer, 0, nj + j)),
                  pl.BlockSpec((FFN_CONV, tj), lambda j, i: (0, j)),
                  pl.BlockSpec((1, tj), lambda j, i: (0, j)),
                  pl.BlockSpec((FFN_CONV, tj), lambda j, i: (0, nj + j)),
                  pl.BlockSpec((1, tj), lambda j, i: (0, nj + j))],
        out_specs=pl.BlockSpec((tm, tj), lambda j, i: (i, j)),
        out_shape=jax.ShapeDtypeStruct((s, D_FF), BF16),
        scratch_shapes=[pltpu.VMEM((d, tj), BF16), pltpu.VMEM((d, tj), BF16),
                        pltpu.VMEM((tm + SUBLANES, tj), F32),
                        pltpu.VMEM((tm + SUBLANES, tj), F32)],
        compiler_params=_params("arbitrary", "arbitrary"),
        name="ffn_up_glu",
    )(xb, w_up, w_up, conv_w, conv_b, conv_w, conv_b)


def _pad_cols(w, width):
    return jnp.pad(w, ((0, 0), (0, width - w.shape[1])))


IN_OFFS = tuple(sum(IN_SIZES[:n]) for n in range(len(IN_SIZES) + 1))
W_IN_B0 = IN_OFFS[3]
W_IN_DT0 = IN_OFFS[5]
W_IN_C0 = IN_OFFS[6]
W_IN_G0 = IN_OFFS[9]
assert (W_IN_DT0 - W_IN_B0, W_IN_G0 - W_IN_C0, IN_OFFS[-1] - W_IN_G0) == (HB_COLS, HC_COLS, HG_COLS)


def _pack_mla_weights(w_uq, w_ukv):
    wq = w_uq.reshape(MLA_Q_LORA, MLA_HEADS, MLA_DK)
    rope_part = jnp.pad(wq[:, :, :MLA_ROPE], ((0, 0), (0, 0), (0, LANES - MLA_ROPE)))
    nope_part = wq[:, :, MLA_ROPE:]
    wq_p = jnp.concatenate([rope_part.reshape(MLA_Q_LORA, -1), nope_part.reshape(MLA_Q_LORA, -1)], axis=1)
    wkv = w_ukv.reshape(MLA_KV_LORA, MLA_HEADS, MLA_NOPE + MLA_V)
    wkv_p = jnp.concatenate([wkv[:, :, :MLA_NOPE].reshape(MLA_KV_LORA, -1),
                             wkv[:, :, MLA_NOPE:].reshape(MLA_KV_LORA, -1)], axis=1)
    return wq_p.astype(BF16), wkv_p.astype(BF16)


def _rope_tables(s, rot_dim):
    half = rot_dim // 2
    inv_freq = ROPE_THETA ** (-jnp.arange(half, dtype=F32) / half)
    ang = jnp.arange(s, dtype=jnp.int32).astype(F32)[:, None] * inv_freq[None, :]
    cos, sin = jnp.cos(ang), jnp.sin(ang)
    cos_g = jnp.concatenate([cos, cos], axis=1)
    sin_g = jnp.concatenate([-sin, sin], axis=1)
    if rot_dim == MLA_ROPE:
        reps = LANES // rot_dim
        return jnp.tile(cos_g, (1, reps)), jnp.tile(sin_g, (1, reps))
    rest = LANES - rot_dim
    return (jnp.concatenate([cos_g, jnp.ones((s, rest), F32)], axis=1),
            jnp.concatenate([sin_g, jnp.zeros((s, rest), F32)], axis=1))


ROW_TILE = 1024
COL_TILE = 1024
MLA_PREP_ROWS = 256
ATTN_TILE = 512
MERGE_COLS = 512
OUT_PROJ_ROWS = 512
FFN_COLS = 512
FFN_UP_ROWS = 2048
FFN_DOWN_ROWS = 256


def _layer(x, xb, p, tabs):
    s = x.shape[0]
    mla_cos, mla_sin, moba_cos, moba_sin, onehot = tabs
    w_in_t = p["w_in_t"]
    layer = p["layer"]
    tm = min(s, ROW_TILE)

    def in_proj(windows, cols, dtype, name):
        return _matmul_wt(xb, w_in_t, layer, windows, cols // COL_TILE, dtype, tm, name)

    h_a = in_proj([(0, COL_DT), (W_IN_DT0, LANES)], HA_COLS, F32, "in_proj_a")
    h_b = in_proj([(W_IN_B0, COL_TILE)], HB_COLS, F32, "in_proj_b")
    h_c = in_proj([(W_IN_C0, COL_TILE)], HC_COLS, F32, "in_proj_c")
    h_g = in_proj([(W_IN_G0, COL_TILE)], HG_COLS, BF16, "in_proj_g")

    qt, k, vt = _mla_prep(h_a, p["mla_q_norm"], p["mla_kv_norm"], p["w_uq"], p["w_ukv"],
                          mla_cos, mla_sin, MLA_PREP_ROWS)
    ya = _mla_attn(qt, k, vt, ATTN_TILE)

    yb = _ssd(h_a, h_b, p["ssm_conv_w"], p["ssm_conv_b"], p["ssm_dt_bias"], p["ssm_a_log"],
              p["ssm_d"], p["ssm_norm"])

    mq, mk, mvt, km = _moba_prep(h_c, moba_cos, moba_sin)
    nb = s // MOBA_BLOCK
    km = jnp.pad(km.reshape(nb, MOBA_W), ((0, -nb % BF16_ROWS), (0, 0)))
    yc = _moba_attn(mq, mk, mvt, km, onehot, ATTN_TILE)

    merged = _merge(ya, yb, yc, p["w_branch_a"], p["w_branch_b"], p["w_branch_c"], layer, h_g,
                    p["gate_bias"], tm, MERGE_COLS)
    x1, x1b = _proj_ln(merged, p["w_out"], layer, x, p["ln1_g"], p["ln1_b"], OUT_PROJ_ROWS,
                       SUB_ROWS, "out_proj_ln")
    a = _ffn_up(x1b, p["ffn_w_up"], layer, p["ffn_conv_w"], p["ffn_conv_b"],
                FFN_UP_ROWS if s % FFN_UP_ROWS == 0 else tm, FFN_COLS)
    return _proj_ln(a, p["ffn_w_down"], layer, x1, p["ln2_g"], p["ln2_b"], FFN_DOWN_ROWS,
                    SUB_ROWS // 2, "ffn_down_ln")


def kernel(x, w_in, mla_q_norm, mla_w_uq, mla_kv_norm, mla_w_ukv, ssm_conv_w, ssm_conv_b, ssm_dt_bias, ssm_a_log, ssm_d, ssm_norm, w_branch_a, w_branch_b, w_branch_c, gate_bias, w_out, ln1_g, ln1_b, ffn_w_up, ffn_conv_w, ffn_conv_b, ffn_w_down, ln2_g, ln2_b):
    b, s, d = x.shape
    assert b == 1 and d == D_MODEL
    assert s % 1024 == 0 and s // MOBA_BLOCK <= LANES
    depth = w_in.shape[0]

    tabs = _rope_tables(s, MLA_ROPE) + _rope_tables(s, MOBA_ROT)
    blk = jnp.arange(s, dtype=jnp.int32)[:, None] // MOBA_BLOCK
    onehot = (blk == jnp.arange(LANES, dtype=jnp.int32)[None, :]).astype(BF16)
    tabs = tabs + (onehot,)

    xf = x.reshape(s, d)
    xb = xf.astype(BF16)
    w_in_t = jnp.transpose(w_in, (0, 2, 1))
    w_out_b = w_out.astype(BF16)
    ffn_w_down_b = ffn_w_down.astype(BF16)
    for l in range(depth):
        wuq, wukv = _pack_mla_weights(mla_w_uq[l], mla_w_ukv[l])
        p = {
            "w_in_t": w_in_t,
            "layer": l,
            "mla_q_norm": mla_q_norm[l].reshape(1, -1),
            "mla_kv_norm": mla_kv_norm[l].reshape(1, -1),
            "w_uq": wuq, "w_ukv": wukv,
            "ssm_conv_w": ssm_conv_w[l],
            "ssm_conv_b": ssm_conv_b[l].reshape(1, -1),
            "ssm_dt_bias": _pad_cols(ssm_dt_bias[l].reshape(1, -1), LANES),
            "ssm_a_log": _pad_cols(ssm_a_log[l].reshape(1, -1), LANES),
            "ssm_d": jnp.repeat(ssm_d[l], SSM_HEADDIM).reshape(1, -1),
            "ssm_norm": ssm_norm[l].reshape(1, -1),
            "w_branch_a": w_branch_a, "w_branch_b": w_branch_b, "w_branch_c": w_branch_c,
            "gate_bias": gate_bias[l].reshape(1, -1),
            "w_out": w_out_b,
            "ln1_g": ln1_g[l].reshape(1, -1), "ln1_b": ln1_b[l].reshape(1, -1),
            "ffn_w_up": ffn_w_up,
            "ffn_conv_w": ffn_conv_w[l],
            "ffn_conv_b": ffn_conv_b[l].reshape(1, -1),
            "ffn_w_down": ffn_w_down_b,
            "ln2_g": ln2_g[l].reshape(1, -1), "ln2_b": ln2_b[l].reshape(1, -1),
        }
        xf, xb = _layer(xf, xb, p, tabs)
    return xf.reshape(b, s, d)
```
